```python
import jax, jax.numpy as jnp
from jax import lax
import numpy as np

D_MODEL = 1024
BATCH = 4
SEQ = 4096
DEPTH = 4

D_LRU = D_MODEL // 2
D_SC = D_MODEL - D_LRU
D_MIX = D_LRU + D_SC
LRU_HEADS = 8
LRU_HEAD_DIM = D_LRU // LRU_HEADS
SC_GROUPS = 8
LRU_CONV = 4
SC_CONV = 3
LRU_C = 8.0
D_IN = 2 * D_LRU + 3 * D_SC
D_FF = 2816
N_EXPERTS = 8
TOP_K = 2
D_FF_EXPERT = 1408
N_DENSE = (DEPTH + 1) // 2
N_MOE = DEPTH // 2
EPS = 1e-6

kernel_name = "hymba_rglru_shortconv_moe_adaln"


def rms_norm(x, g):
    xf = x.astype(jnp.float32)
    y = xf * lax.rsqrt(jnp.mean(xf * xf, axis=-1, keepdims=True) + EPS)
    return (y * g.astype(jnp.float32)).astype(x.dtype)


def group_rms_norm(y, g, n_groups):
    b, s, ch = y.shape
    yf = y.astype(jnp.float32).reshape(b, s, n_groups, ch // n_groups)
    yf = yf * lax.rsqrt(jnp.mean(yf * yf, axis=-1, keepdims=True) + EPS)
    return (yf.reshape(b, s, ch) * g.astype(jnp.float32)).astype(y.dtype)


def causal_depthwise_conv(x, w):
    k, ch = w.shape
    return lax.conv_general_dilated(
        x, w[:, None, :].astype(x.dtype), window_strides=(1,), padding=[(k - 1, 0)],
        dimension_numbers=("NWC", "WIO", "NWC"), feature_group_count=ch)


def adaln(c, w_mod, b_mod):
    m = jax.nn.silu(c) @ w_mod + b_mod
    shift, scale, gate = jnp.split(m, 3, axis=-1)
    return shift[:, None, :], scale[:, None, :], gate[:, None, :]


def rg_lru(x, w_a, b_a, w_i, b_i, lam):
    b, s, _ = x.shape
    xh = x.reshape(b, s, LRU_HEADS, LRU_HEAD_DIM)
    r = jax.nn.sigmoid(jnp.einsum("bshi,hij->bshj", xh, w_a) + b_a).reshape(b, s, D_LRU)
    i = jax.nn.sigmoid(jnp.einsum("bshi,hij->bshj", xh, w_i) + b_i).reshape(b, s, D_LRU)
    log_a = -LRU_C * r.astype(jnp.float32) * jax.nn.softplus(-lam.astype(jnp.float32))
    a = jnp.exp(log_a)
    u = jnp.sqrt(-jnp.expm1(2.0 * log_a)) * (i * x).astype(jnp.float32)

    def combine(left, right):
        a1, b1 = left
        a2, b2 = right
        return a1 * a2, a2 * b1 + b2

    _, h = lax.associative_scan(combine, (a, u), axis=1)
    return h.astype(x.dtype)


def hybrid_mixer(h, w_in, lru_conv_w, lru_conv_b, lru_wa, lru_ba, lru_wi, lru_bi, lru_lambda,
                 sc_conv_w, gn_lru, gn_sc, w_out):
    u = h @ w_in
    x_lru, g_lru, b_sc, c_sc, x_sc = jnp.split(
        u, [D_LRU, 2 * D_LRU, 2 * D_LRU + D_SC, 2 * D_LRU + 2 * D_SC], axis=-1)
    xc = causal_depthwise_conv(x_lru, lru_conv_w) + lru_conv_b
    y_lru = jax.nn.gelu(g_lru) * rg_lru(xc, lru_wa, lru_ba, lru_wi, lru_bi, lru_lambda)
    y_sc = b_sc * causal_depthwise_conv(c_sc * x_sc, sc_conv_w)
    y = jnp.concatenate([group_rms_norm(y_lru, gn_lru, LRU_HEADS),
                         group_rms_norm(y_sc, gn_sc, SC_GROUPS)], axis=-1)
    return y @ w_out


def swiglu(h, w1, w3, w2):
    return (jax.nn.silu(h @ w1) * (h @ w3)) @ w2


def moe_swiglu(h, w_router, b_router, w1, w3, w2):
    b, s, d = h.shape
    t = h.reshape(b * s, d)
    logits = t.astype(jnp.float32) @ w_router.astype(jnp.float32) + b_router.astype(jnp.float32)
    vals, idx = lax.top_k(logits, TOP_K)
    wts = jax.nn.softmax(vals, axis=-1)
    combine = jnp.sum(jax.nn.one_hot(idx, N_EXPERTS, dtype=jnp.float32) * wts[..., None],
                      axis=1).astype(h.dtype)
    out = jnp.zeros_like(t)
    for e in range(N_EXPERTS):
        out = out + combine[:, e:e + 1] * swiglu(t, w1[e], w3[e], w2[e])
    return out.reshape(b, s, d)


def setup_inputs(seed: int = 0) -> dict:
    key = jax.random.key(seed)
    ks = iter(jax.random.split(key, 40))
    f32 = jnp.float32

    def nrm(shape, scale):
        return jax.random.normal(next(ks), shape, f32) * scale

    def gain(shape):
        return 1.0 + nrm(shape, 0.02)

    x = nrm((BATCH, SEQ, D_MODEL), 1.0)
    c = nrm((BATCH, D_MODEL), 1.0)
    mix_norm = gain((DEPTH, D_MODEL))
    mix_mod_w = nrm((DEPTH, D_MODEL, 3 * D_MODEL), 0.5 * D_MODEL ** -0.5)
    mix_mod_b = nrm((DEPTH, 3 * D_MODEL), 0.02)
    w_in = nrm((DEPTH, D_MODEL, D_IN), D_MODEL ** -0.5)
    lru_conv_w = nrm((DEPTH, LRU_CONV, D_LRU), LRU_CONV ** -0.5)
    lru_conv_b = nrm((DEPTH, D_LRU), 0.02)
    lru_wa = nrm((DEPTH, LRU_HEADS, LRU_HEAD_DIM, LRU_HEAD_DIM), LRU_HEAD_DIM ** -0.5)
    lru_ba = nrm((DEPTH, LRU_HEADS, LRU_HEAD_DIM), 0.02)
    lru_wi = nrm((DEPTH, LRU_HEADS, LRU_HEAD_DIM, LRU_HEAD_DIM), LRU_HEAD_DIM ** -0.5)
    lru_bi = nrm((DEPTH, LRU_HEADS, LRU_HEAD_DIM), 0.02)
    a_c = jax.random.uniform(next(ks), (DEPTH, D_LRU), f32, 0.9, 0.999)
    a_base = a_c ** (1.0 / LRU_C)
    lru_lambda = jnp.log(a_base) - jnp.log1p(-a_base)
    sc_conv_w = nrm((DEPTH, SC_CONV, D_SC), SC_CONV ** -0.5)
    gn_lru = gain((DEPTH, D_LRU))
    gn_sc = gain((DEPTH, D_SC))
    w_out = nrm((DEPTH, D_MIX, D_MODEL), D_MIX ** -0.5)
    ffn_norm = gain((DEPTH, D_MODEL))
    ffn_mod_w = nrm((DEPTH, D_MODEL, 3 * D_MODEL), 0.5 * D_MODEL ** -0.5)
    ffn_mod_b = nrm((DEPTH, 3 * D_MODEL), 0.02)
    dense_w1 = nrm((N_DENSE, D_MODEL, D_FF), D_MODEL ** -0.5)
    dense_w3 = nrm((N_DENSE, D_MODEL, D_FF), D_MODEL ** -0.5)
    dense_w2 = nrm((N_DENSE, D_FF, D_MODEL), D_FF ** -0.5)
    router_w = nrm((N_MOE, D_MODEL, N_EXPERTS), D_MODEL ** -0.5)
    router_b = nrm((N_MOE, N_EXPERTS), 0.01)
    exp_w1 = nrm((N_MOE, N_EXPERTS, D_MODEL, D_FF_EXPERT), D_MODEL ** -0.5)
    exp_w3 = nrm((N_MOE, N_EXPERTS, D_MODEL, D_FF_EXPERT), D_MODEL ** -0.5)
    exp_w2 = nrm((N_MOE, N_EXPERTS, D_FF_EXPERT, D_MODEL), D_FF_EXPERT ** -0.5)
    final_norm = gain((D_MODEL,))
    return {"x": x, "c": c, "mix_norm": mix_norm, "mix_mod_w": mix_mod_w, "mix_mod_b": mix_mod_b,
            "w_in": w_in, "lru_conv_w": lru_conv_w, "lru_conv_b": lru_conv_b,
            "lru_wa": lru_wa, "lru_ba": lru_ba, "lru_wi": lru_wi, "lru_bi": lru_bi,
            "lru_lambda": lru_lambda, "sc_conv_w": sc_conv_w, "gn_lru": gn_lru, "gn_sc": gn_sc,
            "w_out": w_out, "ffn_norm": ffn_norm, "ffn_mod_w": ffn_mod_w, "ffn_mod_b": ffn_mod_b,
            "dense_w1": dense_w1, "dense_w3": dense_w3, "dense_w2": dense_w2,
            "router_w": router_w, "router_b": router_b,
            "exp_w1": exp_w1, "exp_w3": exp_w3, "exp_w2": exp_w2, "final_norm": final_norm}


def reference(x, c, mix_norm, mix_mod_w, mix_mod_b, w_in, lru_conv_w, lru_conv_b,
              lru_wa, lru_ba, lru_wi, lru_bi, lru_lambda, sc_conv_w, gn_lru, gn_sc, w_out,
              ffn_norm, ffn_mod_w, ffn_mod_b, dense_w1, dense_w3, dense_w2,
              router_w, router_b, exp_w1, exp_w3, exp_w2, final_norm):
    for l in range(DEPTH):
        shift, scale, gate = adaln(c, mix_mod_w[l], mix_mod_b[l])
        h = rms_norm(x, mix_norm[l]) * (1.0 + scale) + shift
        y = hybrid_mixer(h, w_in[l], lru_conv_w[l], lru_conv_b[l], lru_wa[l], lru_ba[l],
                         lru_wi[l], lru_bi[l], lru_lambda[l], sc_conv_w[l], gn_lru[l], gn_sc[l],
                         w_out[l])
        x = x + gate * y
        shift, scale, gate = adaln(c, ffn_mod_w[l], ffn_mod_b[l])
        h = rms_norm(x, ffn_norm[l]) * (1.0 + scale) + shift
        j = l // 2
        if l % 2 == 0:
            f = swiglu(h, dense_w1[j], dense_w3[j], dense_w2[j])
        else:
            f = moe_swiglu(h, router_w[j], router_b[j], exp_w1[j], exp_w3[j], exp_w2[j])
        x = x + gate * f
    return rms_norm(x, final_norm)
```

```python
import functools

import jax
import jax.numpy as jnp
from jax import lax
from jax.experimental import pallas as pl
from jax.experimental.pallas import tpu as pltpu

F32 = jnp.float32
BF16 = jnp.bfloat16

D_MODEL = 1024
D_LRU = 512
D_SC = 512
D_IN = 2 * D_LRU + 3 * D_SC
HEAD_DIM = 64
LRU_CONV = 4
SC_CONV = 3
LRU_C = 8.0
N_EXPERTS = 8
EPS = 1e-6

SUBLANES = 8
LANES = 128
MXU_DIM = 256
TS = SUBLANES ** 3
PAD_B = 8
VMEM_LIMIT = 56 * 1024 * 1024


def _mod_norm(x, g, scale, shift):
    ms = jnp.mean(x * x, axis=-1, keepdims=True)
    return (x * lax.rsqrt(ms + EPS) * g) * (1.0 + scale) + shift


def _split_mod(mod):
    return mod[:, :D_MODEL], mod[:, D_MODEL:2 * D_MODEL], mod[:, 2 * D_MODEL:]


def _mod_kernel(c_ref, w_ref, b_ref, o_ref):
    c = c_ref[...]
    s = (c * jax.nn.sigmoid(c)).astype(BF16)
    o_ref[0] = jnp.dot(s, w_ref[0].astype(BF16), preferred_element_type=F32) + b_ref[0]


def _modulation(c_pad, w, b):
    depth = w.shape[0]
    n_col = 3 * D_MODEL // D_MODEL
    return pl.pallas_call(
        _mod_kernel,
        grid=(depth, n_col),
        in_specs=[
            pl.BlockSpec((PAD_B, D_MODEL), lambda l, j: (0, 0)),
            pl.BlockSpec((1, D_MODEL, D_MODEL), lambda l, j: (l, 0, j)),
            pl.BlockSpec((1, 1, D_MODEL), lambda l, j: (l, 0, j)),
        ],
        out_specs=pl.BlockSpec((1, PAD_B, D_MODEL), lambda l, j: (l, 0, j)),
        out_shape=jax.ShapeDtypeStruct((depth, PAD_B, 3 * D_MODEL), F32),
        compiler_params=pltpu.CompilerParams(
            dimension_semantics=("arbitrary", "arbitrary"), vmem_limit_bytes=VMEM_LIMIT),
        name="adaln_mod",
    )(c_pad, w, b.reshape(depth, 1, 3 * D_MODEL))


def _scan8(a, u, row_in_group):
    for d in (1, 2, 4):
        keep = row_in_group >= d
        a_sh = jnp.where(keep, pltpu.roll(a, d, axis=0), 1.0)
        u_sh = jnp.where(keep, pltpu.roll(u, d, axis=0), 0.0)
        u = a * u_sh + u
        a = a * a_sh
    return a, u


def _row_in_group(rows, cols):
    return lax.broadcasted_iota(jnp.int32, (rows, cols), 0) % SUBLANES


def _shift_rows_down(x, first_row):
    row = lax.broadcasted_iota(jnp.int32, x.shape, 0)
    return jnp.where(row == 0, first_row, pltpu.roll(x, 1, axis=0))


def _group_rms(y, gmat_ref, gain):
    y2 = (y * y).astype(BF16)
    ms = jnp.concatenate(
        [jnp.dot(y2[:, :MXU_DIM], gmat_ref[...], preferred_element_type=F32),
         jnp.dot(y2[:, MXU_DIM:], gmat_ref[...], preferred_element_type=F32)], axis=-1)
    return y * lax.rsqrt(ms + EPS) * gain


def _mixer_kernel(x_ref, mod_ref, g_ref, win_ref, lcw_ref, lcb_ref, wg_ref, bg_ref, lam_ref,
                  scw_ref, gnl_ref, gns_ref, gmat_ref, wout_ref, o_ref,
                  xl_buf, v_buf, a0_ref, u0_ref, a1_ref, u1_ref, p1_ref, p2_ref, carry_ref):
    ts = TS
    n1 = ts // SUBLANES
    n2 = n1 // SUBLANES

    @pl.when(pl.program_id(1) == 0)
    def _():
        xl_buf[0:SUBLANES, :] = jnp.zeros((SUBLANES, D_LRU), F32)
        v_buf[0:SUBLANES, :] = jnp.zeros((SUBLANES, D_SC), F32)
        carry_ref[...] = jnp.zeros(carry_ref.shape, F32)

    x = x_ref[0]
    shift, scale, gate = _split_mod(mod_ref[0])
    h = _mod_norm(x, g_ref[...], scale, shift).astype(BF16)
    u = jnp.dot(h, win_ref[...], preferred_element_type=F32)
    x_lru = u[:, 0:D_LRU]
    g_lru = u[:, D_LRU:2 * D_LRU]
    b_sc = u[:, 2 * D_LRU:2 * D_LRU + D_SC]
    c_sc = u[:, 2 * D_LRU + D_SC:2 * D_LRU + 2 * D_SC]
    x_sc = u[:, 2 * D_LRU + 2 * D_SC:]

    xl_buf[SUBLANES:SUBLANES + ts, :] = x_lru
    xc = lcb_ref[...]
    for k in range(LRU_CONV):
        off = SUBLANES - (LRU_CONV - 1) + k
        xc = xc + lcw_ref[k:k + 1, :] * xl_buf[pl.ds(off, ts), :]
    xl_buf[0:SUBLANES, :] = xl_buf[ts:ts + SUBLANES, :]

    v_buf[SUBLANES:SUBLANES + ts, :] = c_sc * x_sc
    cv = jnp.zeros((ts, D_SC), F32)
    for k in range(SC_CONV):
        off = SUBLANES - (SC_CONV - 1) + k
        cv = cv + scw_ref[k:k + 1, :] * v_buf[pl.ds(off, ts), :]
    v_buf[0:SUBLANES, :] = v_buf[ts:ts + SUBLANES, :]
    y_sc = b_sc * cv

    xcb = xc.astype(BF16)
    g0 = jnp.dot(xcb[:, :MXU_DIM], wg_ref[0], preferred_element_type=F32)
    g1 = jnp.dot(xcb[:, MXU_DIM:], wg_ref[1], preferred_element_type=F32)
    r = jax.nn.sigmoid(jnp.concatenate([g0[:, :MXU_DIM], g1[:, :MXU_DIM]], axis=-1) + bg_ref[0:1, :])
    i = jax.nn.sigmoid(jnp.concatenate([g0[:, MXU_DIM:], g1[:, MXU_DIM:]], axis=-1) + bg_ref[1:2, :])
    log_a = (-LRU_C) * r * jax.nn.softplus(-lam_ref[...])
    a = jnp.exp(log_a)
    uin = jnp.sqrt(-jnp.tanh(log_a) * (1.0 + a * a)) * (i * xc)

    last = SUBLANES - 1
    hs_blocks = []
    for lb in range(D_LRU // LANES):
        lanes = slice(lb * LANES, (lb + 1) * LANES)
        h_prev = carry_ref[lb, last:last + 1, :]
        a0, u0 = _scan8(a[:, lanes], uin[:, lanes], _row_in_group(ts, LANES))
        a0_ref[lb] = a0
        u0_ref[lb] = u0
        a1, u1 = _scan8(a0_ref[lb, pl.ds(last, n1, stride=SUBLANES), :],
                        u0_ref[lb, pl.ds(last, n1, stride=SUBLANES), :],
                        _row_in_group(n1, LANES))
        a1_ref[lb] = a1
        u1_ref[lb] = u1
        a2, u2 = _scan8(a1_ref[lb, pl.ds(last, n2, stride=SUBLANES), :],
                        u1_ref[lb, pl.ds(last, n2, stride=SUBLANES), :],
                        _row_in_group(n2, LANES))
        h2 = u2 + a2 * h_prev
        carry_ref[lb] = h2
        p2_ref[lb] = _shift_rows_down(h2, h_prev)
        p2x = jnp.concatenate(
            [jnp.broadcast_to(p2_ref[lb, j:j + 1, :], (SUBLANES, LANES)) for j in range(n2)],
            axis=0)
        h1 = u1 + a1 * p2x
        p1_ref[lb] = _shift_rows_down(h1, h_prev)
        for g in range(n1):
            rows = pl.ds(g * SUBLANES, SUBLANES)
            u0_ref[lb, rows, :] = (u0_ref[lb, rows, :]
                                   + a0_ref[lb, rows, :] * p1_ref[lb, g:g + 1, :])
        hs_blocks.append(u0_ref[lb])
    hs = jnp.concatenate(hs_blocks, axis=-1)

    y_lru = jax.nn.gelu(g_lru) * hs
    yn = jnp.concatenate([_group_rms(y_lru, gmat_ref, gnl_ref[...]),
                          _group_rms(y_sc, gmat_ref, gns_ref[...])], axis=-1).astype(BF16)
    out = jnp.dot(yn, wout_ref[...], preferred_element_type=F32)
    o_ref[0] = x + gate * out


def _const_spec(shape):
    nd = len(shape)
    return pl.BlockSpec(shape, lambda b, t: (0,) * nd)


def _mixer(x, mods, layer, g, w_in, lcw, lcb, wg, bg, lam, scw, gnl, gns, gmat, w_out):
    batch, seq, _ = x.shape
    ts = TS
    nlb = D_LRU // LANES
    return pl.pallas_call(
        _mixer_kernel,
        grid=(batch, seq // ts),
        in_specs=[
            pl.BlockSpec((1, ts, D_MODEL), lambda b, t: (b, t, 0)),
            pl.BlockSpec((1, 1, 3 * D_MODEL), lambda b, t: (layer * PAD_B + b, 0, 0)),
            _const_spec((1, D_MODEL)),
            _const_spec((D_MODEL, D_IN)),
            _const_spec((LRU_CONV, D_LRU)),
            _const_spec((1, D_LRU)),
            _const_spec((2, MXU_DIM, 2 * MXU_DIM)),
            _const_spec((2, D_LRU)),
            _const_spec((1, D_LRU)),
            _const_spec((SC_CONV, D_SC)),
            _const_spec((1, D_LRU)),
            _const_spec((1, D_SC)),
            _const_spec((MXU_DIM, MXU_DIM)),
            _const_spec((D_LRU + D_SC, D_MODEL)),
        ],
        out_specs=pl.BlockSpec((1, ts, D_MODEL), lambda b, t: (b, t, 0)),
        out_shape=jax.ShapeDtypeStruct(x.shape, F32),
        scratch_shapes=[
            pltpu.VMEM((ts + SUBLANES, D_LRU), F32),
            pltpu.VMEM((ts + SUBLANES, D_SC), F32),
            pltpu.VMEM((nlb, ts, LANES), F32),
            pltpu.VMEM((nlb, ts, LANES), F32),
            pltpu.VMEM((nlb, ts // SUBLANES, LANES), F32),
            pltpu.VMEM((nlb, ts // SUBLANES, LANES), F32),
            pltpu.VMEM((nlb, ts // SUBLANES, LANES), F32),
            pltpu.VMEM((nlb, SUBLANES, LANES), F32),
            pltpu.VMEM((nlb, SUBLANES, LANES), F32),
        ],
        compiler_params=pltpu.CompilerParams(
            dimension_semantics=("arbitrary", "arbitrary"), vmem_limit_bytes=VMEM_LIMIT),
        name="mixer",
    )(x, mods, g, w_in, lcw, lcb, wg, bg, lam, scw, gnl, gns, gmat, w_out)


def _swiglu(h, w1, w3, w2):
    a = jnp.dot(h, w1, preferred_element_type=F32)
    b = jnp.dot(h, w3, preferred_element_type=F32)
    g = (a * jax.nn.sigmoid(a) * b).astype(BF16)
    return jnp.dot(g, w2, preferred_element_type=F32)


def _dense_ffn_kernel(x_ref, mod_ref, g_ref, w1_ref, w3_ref, w2_ref, o_ref, *, n_chunks):
    x = x_ref[0]
    shift, scale, gate = _split_mod(mod_ref[0])
    h = _mod_norm(x, g_ref[...], scale, shift).astype(BF16)
    fc = w1_ref.shape[1] // n_chunks
    f = jnp.zeros(x.shape, F32)
    for c in range(n_chunks):
        cols = slice(c * fc, (c + 1) * fc)
        f = f + _swiglu(h, w1_ref[:, cols], w3_ref[:, cols], w2_ref[cols, :])
    o_ref[0] = x + gate * f


def _dense_ffn(x, mods, layer, g, w1, w3, w2, tm=512, n_chunks=2):
    batch, seq, _ = x.shape
    d_ff = w1.shape[1]
    return pl.pallas_call(
        functools.partial(_dense_ffn_kernel, n_chunks=n_chunks),
        grid=(batch, seq // tm),
        in_specs=[
            pl.BlockSpec((1, tm, D_MODEL), lambda b, t: (b, t, 0)),
            pl.BlockSpec((1, 1, 3 * D_MODEL), lambda b, t: (layer * PAD_B + b, 0, 0)),
            _const_spec((1, D_MODEL)),
            _const_spec((D_MODEL, d_ff)),
            _const_spec((D_MODEL, d_ff)),
            _const_spec((d_ff, D_MODEL)),
        ],
        out_specs=pl.BlockSpec((1, tm, D_MODEL), lambda b, t: (b, t, 0)),
        out_shape=jax.ShapeDtypeStruct(x.shape, F32),
        compiler_params=pltpu.CompilerParams(
            dimension_semantics=("arbitrary", "arbitrary"), vmem_limit_bytes=VMEM_LIMIT),
        name="dense_ffn",
    )(x, mods, g, w1, w3, w2)


def _top2_combine(logits):
    lane = lax.broadcasted_iota(jnp.int32, logits.shape, 1)
    m1 = jnp.max(logits, axis=-1, keepdims=True)
    i1 = jnp.min(jnp.where(logits == m1, lane, LANES), axis=-1, keepdims=True)
    rest = jnp.where(lane == i1, -jnp.inf, logits)
    m2 = jnp.max(rest, axis=-1, keepdims=True)
    i2 = jnp.min(jnp.where(rest == m2, lane, LANES), axis=-1, keepdims=True)
    e2 = jnp.exp(m2 - m1)
    w_first = 1.0 / (1.0 + e2)
    w_second = e2 / (1.0 + e2)
    return jnp.where(lane == i1, w_first, 0.0) + jnp.where(lane == i2, w_second, 0.0)


def _moe_kernel(x_ref, mod_ref, g_ref, wr_ref, br_ref, w1_ref, w3_ref, w2_ref, fg_ref, o_ref,
                h_ref, comb_ref, acc_ref, *, final):
    e = pl.program_id(2)

    @pl.when(e == 0)
    def _():
        x = x_ref[0]
        shift, scale, _ = _split_mod(mod_ref[0])
        h = _mod_norm(x, g_ref[...], scale, shift)
        logits = jnp.dot(h, wr_ref[...], preferred_element_type=F32,
                         precision=lax.Precision.HIGHEST) + br_ref[...]
        comb_ref[...] = _top2_combine(logits)
        h_ref[...] = h.astype(BF16)
        acc_ref[...] = jnp.zeros(acc_ref.shape, F32)

    lane = lax.broadcasted_iota(jnp.int32, comb_ref.shape, 1)
    col = jnp.sum(jnp.where(lane == e, comb_ref[...], 0.0), axis=-1, keepdims=True)
    acc_ref[...] += col * _swiglu(h_ref[...], w1_ref[0], w3_ref[0], w2_ref[0])

    @pl.when(e == N_EXPERTS - 1)
    def _():
        gate = mod_ref[0][:, 2 * D_MODEL:]
        y = x_ref[0] + gate * acc_ref[...]
        if final:
            ms = jnp.mean(y * y, axis=-1, keepdims=True)
            y = y * lax.rsqrt(ms + EPS) * fg_ref[...]
        o_ref[0] = y


def _moe_ffn(x, mods, layer, g, wr, br, w1, w3, w2, final_gain, final, tm=1024):
    batch, seq, _ = x.shape
    d_ff = w1.shape[2]
    const = lambda shape: pl.BlockSpec(shape, lambda b, t, e: (0,) * len(shape))
    return pl.pallas_call(
        functools.partial(_moe_kernel, final=final),
        grid=(batch, seq // tm, N_EXPERTS),
        in_specs=[
            pl.BlockSpec((1, tm, D_MODEL), lambda b, t, e: (b, t, 0)),
            pl.BlockSpec((1, 1, 3 * D_MODEL), lambda b, t, e: (layer * PAD_B + b, 0, 0)),
            const((1, D_MODEL)),
            const((D_MODEL, LANES)),
            const((1, LANES)),
            pl.BlockSpec((1, D_MODEL, d_ff), lambda b, t, e: (e, 0, 0)),
            pl.BlockSpec((1, D_MODEL, d_ff), lambda b, t, e: (e, 0, 0)),
            pl.BlockSpec((1, d_ff, D_MODEL), lambda b, t, e: (e, 0, 0)),
            const((1, D_MODEL)),
        ],
        out_specs=pl.BlockSpec((1, tm, D_MODEL), lambda b, t, e: (b, t, 0)),
        out_shape=jax.ShapeDtypeStruct(x.shape, F32),
        scratch_shapes=[
            pltpu.VMEM((tm, D_MODEL), BF16),
            pltpu.VMEM((tm, LANES), F32),
            pltpu.VMEM((tm, D_MODEL), F32),
        ],
        compiler_params=pltpu.CompilerParams(
            dimension_semantics=("arbitrary", "arbitrary", "arbitrary"),
            vmem_limit_bytes=VMEM_LIMIT),
        name="moe_ffn",
    )(x, mods, g, wr, br, w1, w3, w2, final_gain)


def _block_diag_halves(w):
    depth, heads, hd, _ = w.shape
    per_tile = MXU_DIM // hd
    w = w.reshape(depth, heads // per_tile, per_tile, hd, hd)
    eye = jnp.eye(per_tile, dtype=w.dtype)
    bd = jnp.einsum("dtiab,ij->dtiajb", w, eye)
    return bd.reshape(depth, heads // per_tile, MXU_DIM, MXU_DIM)


def kernel(x, c, mix_norm, mix_mod_w, mix_mod_b, w_in, lru_conv_w, lru_conv_b, lru_wa, lru_ba,
           lru_wi, lru_bi, lru_lambda, sc_conv_w, gn_lru, gn_sc, w_out, ffn_norm, ffn_mod_w,
           ffn_mod_b, dense_w1, dense_w3, dense_w2, router_w, router_b, exp_w1, exp_w3, exp_w2,
           final_norm):
    depth = w_in.shape[0]
    batch = x.shape[0]
    c_pad = jnp.pad(c, ((0, PAD_B - batch), (0, 0)))
    mix_mods = _modulation(c_pad, mix_mod_w, mix_mod_b).reshape(depth * PAD_B, 1, 3 * D_MODEL)
    ffn_mods = _modulation(c_pad, ffn_mod_w, ffn_mod_b).reshape(depth * PAD_B, 1, 3 * D_MODEL)

    wg = jnp.concatenate([_block_diag_halves(lru_wa), _block_diag_halves(lru_wi)],
                         axis=-1).astype(BF16)
    bg = jnp.stack([lru_ba.reshape(depth, D_LRU), lru_bi.reshape(depth, D_LRU)], axis=1)
    head = jnp.arange(MXU_DIM) // HEAD_DIM
    gmat = jnp.where(head[:, None] == head[None, :], 1.0 / HEAD_DIM, 0.0).astype(BF16)
    w_in_b = w_in.astype(BF16)
    w_out_b = w_out.astype(BF16)
    dense_w1_b, dense_w3_b, dense_w2_b = (w.astype(BF16) for w in (dense_w1, dense_w3, dense_w2))
    exp_w1_b, exp_w3_b, exp_w2_b = (w.astype(BF16) for w in (exp_w1, exp_w3, exp_w2))
    n_moe = router_w.shape[0]
    wr = jnp.pad(router_w, ((0, 0), (0, 0), (0, LANES - N_EXPERTS)))
    br = jnp.pad(router_b, ((0, 0), (0, LANES - N_EXPERTS)),
                 constant_values=-jnp.inf).reshape(n_moe, 1, LANES)
    final_gain = final_norm.reshape(1, D_MODEL)

    for l in range(depth):
        x = _mixer(x, mix_mods, l, mix_norm[l].reshape(1, D_MODEL), w_in_b[l], lru_conv_w[l],
                   lru_conv_b[l].reshape(1, D_LRU), wg[l], bg[l], lru_lambda[l].reshape(1, D_LRU),
                   sc_conv_w[l], gn_lru[l].reshape(1, D_LRU), gn_sc[l].reshape(1, D_SC), gmat,
                   w_out_b[l])
        j = l // 2
        g = ffn_norm[l].reshape(1, D_MODEL)
        if l % 2 == 0:
            x = _dense_ffn(x, ffn_mods, l, g, dense_w1_b[j], dense_w3_b[j], dense_w2_b[j])
        else:
            x = _moe_ffn(x, ffn_mods, l, g, wr[j], br[j], exp_w1_b[j], exp_w3_b[j], exp_w2_b[j],
                         final_gain, final=(l == depth - 1))
    if depth % 2 == 1:
        raise NotImplementedError("final norm is fused into the last MoE layer")
    return x
```

```python
import functools

import jax
import jax.numpy as jnp
from jax import lax
from jax.experimental import pallas as pl
from jax.experimental.pallas import tpu as pltpu

F32 = jnp.float32
BF16 = jnp.bfloat16

D_MODEL = 1024
D_LRU = 512
D_SC = 512
D_IN = 2 * D_LRU + 3 * D_SC
HEAD_DIM = 64
LRU_CONV = 4
SC_CONV = 3
LRU_C = 8.0
N_EXPERTS = 8
EPS = 1e-6

SUBLANES = 8
LANES = 128
MXU_DIM = 256
TS = SUBLANES ** 3
PAD_B = 8
VMEM_LIMIT = 56 * 1024 * 1024


def _mod_norm(x, g, scale, shift):
    ms = jnp.mean(x * x, axis=-1, keepdims=True)
    return (x * lax.rsqrt(ms + EPS) * g) * (1.0 + scale) + shift


def _split_mod(mod):
    return mod[:, :D_MODEL], mod[:, D_MODEL:2 * D_MODEL], mod[:, 2 * D_MODEL:]


def _mod_kernel(c_ref, w_ref, b_ref, o_ref):
    c = c_ref[...]
    s = (c * jax.nn.sigmoid(c)).astype(BF16)
    o_ref[0] = jnp.dot(s, w_ref[0].astype(BF16), preferred_element_type=F32) + b_ref[0]


def _modulation(c_pad, w, b):
    depth = w.shape[0]
    n_col = 3 * D_MODEL // D_MODEL
    return pl.pallas_call(
        _mod_kernel,
        grid=(depth, n_col),
        in_specs=[
            pl.BlockSpec((PAD_B, D_MODEL), lambda l, j: (0, 0)),
            pl.BlockSpec((1, D_MODEL, D_MODEL), lambda l, j: (l, 0, j)),
            pl.BlockSpec((1, 1, D_MODEL), lambda l, j: (l, 0, j)),
        ],
        out_specs=pl.BlockSpec((1, PAD_B, D_MODEL), lambda l, j: (l, 0, j)),
        out_shape=jax.ShapeDtypeStruct((depth, PAD_B, 3 * D_MODEL), F32),
        compiler_params=pltpu.CompilerParams(
            dimension_semantics=("arbitrary", "arbitrary"), vmem_limit_bytes=VMEM_LIMIT),
        name="adaln_mod",
    )(c_pad, w, b.reshape(depth, 1, 3 * D_MODEL))


def _scan8(a, u, row_in_group):
    for d in (1, 2, 4):
        keep = row_in_group >= d
        a_sh = jnp.where(keep, pltpu.roll(a, d, axis=0), 1.0)
        u_sh = jnp.where(keep, pltpu.roll(u, d, axis=0), 0.0)
        u = a * u_sh + u
        a = a * a_sh
    return a, u


def _row_in_group(rows, cols):
    return lax.broadcasted_iota(jnp.int32, (rows, cols), 0) % SUBLANES


def _shift_rows_down(x, first_row):
    row = lax.broadcasted_iota(jnp.int32, x.shape, 0)
    return jnp.where(row == 0, first_row, pltpu.roll(x, 1, axis=0))


def _group_rms(y, gmat_ref, gain):
    y2 = (y * y).astype(BF16)
    ms = jnp.concatenate(
        [jnp.dot(y2[:, :MXU_DIM], gmat_ref[...], preferred_element_type=F32),
         jnp.dot(y2[:, MXU_DIM:], gmat_ref[...], preferred_element_type=F32)], axis=-1)
    return y * lax.rsqrt(ms + EPS) * gain


def _mixer_kernel(x_ref, mod_ref, g_ref, win_ref, lcw_ref, lcb_ref, wg_ref, bg_ref, lam_ref,
                  scw_ref, gnl_ref, gns_ref, gmat_ref, wout_ref, o_ref,
                  xl_buf, v_buf, a0_ref, u0_ref, a1_ref, u1_ref, p1_ref, p2_ref, carry_ref):
    ts = TS
    n1 = ts // SUBLANES
    n2 = n1 // SUBLANES

    @pl.when(pl.program_id(1) == 0)
    def _():
        xl_buf[0:SUBLANES, :] = jnp.zeros((SUBLANES, D_LRU), F32)
        v_buf[0:SUBLANES, :] = jnp.zeros((SUBLANES, D_SC), F32)
        carry_ref[...] = jnp.zeros(carry_ref.shape, F32)

    x = x_ref[0]
    shift, scale, gate = _split_mod(mod_ref[0])
    h = _mod_norm(x, g_ref[...], scale, shift).astype(BF16)
    u = jnp.dot(h, win_ref[...], preferred_element_type=F32)
    x_lru = u[:, 0:D_LRU]
    g_lru = u[:, D_LRU:2 * D_LRU]
    b_sc = u[:, 2 * D_LRU:2 * D_LRU + D_SC]
    c_sc = u[:, 2 * D_LRU + D_SC:2 * D_LRU + 2 * D_SC]
    x_sc = u[:, 2 * D_LRU + 2 * D_SC:]

    xl_buf[SUBLANES:SUBLANES + ts, :] = x_lru
    xc = lcb_ref[...]
    for k in range(LRU_CONV):
        off = SUBLANES - (LRU_CONV - 1) + k
        xc = xc + lcw_ref[k:k + 1, :] * xl_buf[pl.ds(off, ts), :]
    xl_buf[0:SUBLANES, :] = xl_buf[ts:ts + SUBLANES, :]

    v_buf[SUBLANES:SUBLANES + ts, :] = c_sc * x_sc
    cv = jnp.zeros((ts, D_SC), F32)
    for k in range(SC_CONV):
        off = SUBLANES - (SC_CONV - 1) + k
        cv = cv + scw_ref[k:k + 1, :] * v_buf[pl.ds(off, ts), :]
    v_buf[0:SUBLANES, :] = v_buf[ts:ts + SUBLANES, :]
    y_sc = b_sc * cv

    xcb = xc.astype(BF16)
    g0 = jnp.dot(xcb[:, :MXU_DIM], wg_ref[0], preferred_element_type=F32)
    g1 = jnp.dot(xcb[:, MXU_DIM:], wg_ref[1], preferred_element_type=F32)
    r = jax.nn.sigmoid(jnp.concatenate([g0[:, :MXU_DIM], g1[:, :MXU_DIM]], axis=-1) + bg_ref[0:1, :])
    i = jax.nn.sigmoid(jnp.concatenate([g0[:, MXU_DIM:], g1[:, MXU_DIM:]], axis=-1) + bg_ref[1:2, :])
    log_a = (-LRU_C) * r * jax.nn.softplus(-lam_ref[...])
    a = jnp.exp(log_a)
    uin = jnp.sqrt(-jnp.tanh(log_a) * (1.0 + a * a)) * (i * xc)

    last = SUBLANES - 1
    hs_blocks = []
    for lb in range(D_LRU // LANES):
        lanes = slice(lb * LANES, (lb + 1) * LANES)
        h_prev = carry_ref[lb, last:last + 1, :]
        a0, u0 = _scan8(a[:, lanes], uin[:, lanes], _row_in_group(ts, LANES))
        a0_ref[lb] = a0
        u0_ref[lb] = u0
        a1, u1 = _scan8(a0_ref[lb, pl.ds(last, n1, stride=SUBLANES), :],
                        u0_ref[lb, pl.ds(last, n1, stride=SUBLANES), :],
                        _row_in_group(n1, LANES))
        a1_ref[lb] = a1
        u1_ref[lb] = u1
        a2, u2 = _scan8(a1_ref[lb, pl.ds(last, n2, stride=SUBLANES), :],
                        u1_ref[lb, pl.ds(last, n2, stride=SUBLANES), :],
                        _row_in_group(n2, LANES))
        h2 = u2 + a2 * h_prev
        carry_ref[lb] = h2
        p2_ref[lb] = _shift_rows_down(h2, h_prev)
        p2x = jnp.concatenate(
            [jnp.broadcast_to(p2_ref[lb, j:j + 1, :], (SUBLANES, LANES)) for j in range(n2)],
            axis=0)
        h1 = u1 + a1 * p2x
        p1_ref[lb] = _shift_rows_down(h1, h_prev)
        for g in range(n1):
            rows = pl.ds(g * SUBLANES, SUBLANES)
            u0_ref[lb, rows, :] = (u0_ref[lb, rows, :]
                                   + a0_ref[lb, rows, :] * p1_ref[lb, g:g + 1, :])
        hs_blocks.append(u0_ref[lb])
    hs = jnp.concatenate(hs_blocks, axis=-1)

    y_lru = jax.nn.gelu(g_lru) * hs
    yn = jnp.concatenate([_group_rms(y_lru, gmat_ref, gnl_ref[...]),
                          _group_rms(y_sc, gmat_ref, gns_ref[...])], axis=-1).astype(BF16)
    out = jnp.dot(yn, wout_ref[...], preferred_element_type=F32)
    o_ref[0] = x + gate * out


def _const_spec(shape):
    nd = len(shape)
    return pl.BlockSpec(shape, lambda b, t: (0,) * nd)


def _mixer(x, mods, layer, g, w_in, lcw, lcb, wg, bg, lam, scw, gnl, gns, gmat, w_out):
    batch, seq, _ = x.shape
    ts = TS
    nlb = D_LRU // LANES
    return pl.pallas_call(
        _mixer_kernel,
        grid=(batch, seq // ts),
        in_specs=[
            pl.BlockSpec((1, ts, D_MODEL), lambda b, t: (b, t, 0)),
            pl.BlockSpec((1, 1, 3 * D_MODEL), lambda b, t: (layer * PAD_B + b, 0, 0)),
            _const_spec((1, D_MODEL)),
            _const_spec((D_MODEL, D_IN)),
            _const_spec((LRU_CONV, D_LRU)),
            _const_spec((1, D_LRU)),
            _const_spec((2, MXU_DIM, 2 * MXU_DIM)),
            _const_spec((2, D_LRU)),
            _const_spec((1, D_LRU)),
            _const_spec((SC_CONV, D_SC)),
            _const_spec((1, D_LRU)),
            _const_spec((1, D_SC)),
            _const_spec((MXU_DIM, MXU_DIM)),
            _const_spec((D_LRU + D_SC, D_MODEL)),
        ],
        out_specs=pl.BlockSpec((1, ts, D_MODEL), lambda b, t: (b, t, 0)),
        out_shape=jax.ShapeDtypeStruct(x.shape, F32),
        scratch_shapes=[
            pltpu.VMEM((ts + SUBLANES, D_LRU), F32),
            pltpu.VMEM((ts + SUBLANES, D_SC), F32),
            pltpu.VMEM((nlb, ts, LANES), F32),
            pltpu.VMEM((nlb, ts, LANES), F32),
            pltpu.VMEM((nlb, ts // SUBLANES, LANES), F32),
            pltpu.VMEM((nlb, ts // SUBLANES, LANES), F32),
            pltpu.VMEM((nlb, ts // SUBLANES, LANES), F32),
            pltpu.VMEM((nlb, SUBLANES, LANES), F32),
            pltpu.VMEM((nlb, SUBLANES, LANES), F32),
        ],
        compiler_params=pltpu.CompilerParams(
            dimension_semantics=("arbitrary", "arbitrary"), vmem_limit_bytes=VMEM_LIMIT),
        name="mixer",
    )(x, mods, g, w_in, lcw, lcb, wg, bg, lam, scw, gnl, gns, gmat, w_out)


def _swiglu(h, w1, w3, w2):
    a = jnp.dot(h, w1, preferred_element_type=F32)
    b = jnp.dot(h, w3, preferred_element_type=F32)
    g = (a * jax.nn.sigmoid(a) * b).astype(BF16)
    return jnp.dot(g, w2, preferred_element_type=F32)


def _dense_ffn_kernel(x_ref, mod_ref, g_ref, w1_ref, w3_ref, w2_ref, o_ref, *, n_chunks):
    x = x_ref[0]
    shift, scale, gate = _split_mod(mod_ref[0])
    h = _mod_norm(x, g_ref[...], scale, shift).astype(BF16)
    fc = w1_ref.shape[1] // n_chunks
    f = jnp.zeros(x.shape, F32)
    for c in range(n_chunks):
        cols = slice(c * fc, (c + 1) * fc)
        f = f + _swiglu(h, w1_ref[:, cols], w3_ref[:, cols], w2_ref[cols, :])
    o_ref[0] = x + gate * f


def _dense_ffn(x, mods, layer, g, w1, w3, w2, tm=512, n_chunks=2):
    batch, seq, _ = x.shape
    d_ff = w1.shape[1]
    return pl.pallas_call(
        functools.partial(_dense_ffn_kernel, n_chunks=n_chunks),
        grid=(batch, seq // tm),
        in_specs=[
            pl.BlockSpec((1, tm, D_MODEL), lambda b, t: (b, t, 0)),
            pl.BlockSpec((1, 1, 3 * D_MODEL), lambda b, t: (layer * PAD_B + b, 0, 0)),
            _const_spec((1, D_MODEL)),
            _const_spec((D_MODEL, d_ff)),
            _const_spec((D_MODEL, d_ff)),
            _const_spec((d_ff, D_MODEL)),
        ],
        out_specs=pl.BlockSpec((1, tm, D_MODEL), lambda b, t: (b, t, 0)),
        out_shape=jax.ShapeDtypeStruct(x.shape, F32),
        compiler_params=pltpu.CompilerParams(
            dimension_semantics=("arbitrary", "arbitrary"), vmem_limit_bytes=VMEM_LIMIT),
        name="dense_ffn",
    )(x, mods, g, w1, w3, w2)


ROUTE_TILE = 1024
CNT_CHUNK = 128
SLOT_TILE = 256
GATHER_CHUNK = 256
ROW_ALIGN = 16
WINDOW = CNT_CHUNK + ROW_ALIGN


def _top2(logits):
    lane = lax.broadcasted_iota(jnp.int32, logits.shape, 1)
    m1 = jnp.max(logits, axis=-1, keepdims=True)
    i1 = jnp.min(jnp.where(logits == m1, lane, LANES), axis=-1, keepdims=True)
    rest = jnp.where(lane == i1, -jnp.inf, logits)
    m2 = jnp.max(rest, axis=-1, keepdims=True)
    i2 = jnp.min(jnp.where(rest == m2, lane, LANES), axis=-1, keepdims=True)
    e2 = jnp.exp(m2 - m1)
    w_first = 1.0 / (1.0 + e2)
    w_second = e2 / (1.0 + e2)
    first = lane == i1
    second = lane == i2
    weights = jnp.where(first, w_first, 0.0) + jnp.where(second, w_second, 0.0)
    return weights, first | second


def _router_kernel(x_ref, mod_ref, g_ref, wr_ref, br_ref,
                   h_ref, rank_ref, w_ref, rank_row_ref, cnt_ref, carry_ref):
    n = ROUTE_TILE

    @pl.when(pl.program_id(0) == 0)
    def _():
        carry_ref[...] = jnp.zeros(carry_ref.shape, F32)

    shift, scale, _ = _split_mod(mod_ref[0])
    h = _mod_norm(x_ref[...], g_ref[...], scale, shift)
    logits = jnp.dot(h, wr_ref[...], preferred_element_type=F32,
                     precision=lax.Precision.HIGHEST) + br_ref[...]
    weights, mask = _top2(logits)
    m = jnp.where(mask, 1.0, 0.0).astype(BF16)
    before = carry_ref[0:1, :]
    row = lax.broadcasted_iota(jnp.int32, (n, n), 0)
    col = lax.broadcasted_iota(jnp.int32, (n, n), 1)
    strictly_lower = jnp.where(row > col, 1.0, 0.0).astype(BF16)
    rank = jnp.where(mask, jnp.dot(strictly_lower, m, preferred_element_type=F32) + before, -1.0)
    chunk = lax.broadcasted_iota(jnp.int32, (n // CNT_CHUNK, n), 0)
    tok = lax.broadcasted_iota(jnp.int32, (n // CNT_CHUNK, n), 1)
    upto_chunk_end = jnp.where(tok < (chunk + 1) * CNT_CHUNK, 1.0, 0.0).astype(BF16)
    cnt = jnp.dot(upto_chunk_end, m, preferred_element_type=F32) + before
    carry_ref[...] = jnp.broadcast_to(cnt[n // CNT_CHUNK - 1:, :], carry_ref.shape)
    h_ref[...] = h.astype(BF16)
    rank_ref[...] = rank
    w_ref[...] = weights
    rank_row_ref[...] = rank.T[:N_EXPERTS, :]
    cnt_ref[...] = cnt.astype(jnp.int32)


def _dispatch_kernel(tile_e_ref, ls0_ref, c_lo_ref, n_c_ref, h_ref, rank_ref, o_ref, acc_ref):
    i = pl.program_id(0)
    e = tile_e_ref[i]
    slot = (ls0_ref[i] + lax.broadcasted_iota(jnp.int32, (SLOT_TILE, GATHER_CHUNK), 0)).astype(F32)
    acc_ref[...] = jnp.zeros(acc_ref.shape, F32)

    def body(k, carry):
        c = c_lo_ref[i] + k
        r = rank_ref[e, pl.ds(c, 1), :]
        p = jnp.where(r == slot, 1.0, 0.0).astype(BF16)
        rows = pl.ds(pl.multiple_of(c * GATHER_CHUNK, GATHER_CHUNK), GATHER_CHUNK)
        acc_ref[...] += jnp.dot(p, h_ref[rows, :], preferred_element_type=F32)
        return carry

    lax.fori_loop(0, n_c_ref[i], body, 0)
    o_ref[...] = acc_ref[...].astype(BF16)


def _expert_kernel(tile_e_ref, x_ref, w1_ref, w3_ref, w2_ref, o_ref):
    o_ref[...] = _swiglu(x_ref[...], w1_ref[0], w3_ref[0], w2_ref[0]).astype(BF16)


def _combine_kernel(wstart_ref, base_ref, x_ref, mod_ref, rank_ref, w_ref, fg_ref, *rest, final):
    y_refs, o_ref = rest[:N_EXPERTS], rest[N_EXPERTS]
    i = pl.program_id(0)
    rank = rank_ref[...]
    wts = w_ref[...]
    lane = lax.broadcasted_iota(jnp.int32, (CNT_CHUNK, WINDOW), 1)
    acc = jnp.zeros((CNT_CHUNK, D_MODEL), F32)
    for e in range(N_EXPERTS):
        rk = rank[:, e:e + 1]
        slot = rk + base_ref[e].astype(F32)
        window_slot = (wstart_ref[i * N_EXPERTS + e] + lane).astype(F32)
        p = jnp.where((rk >= 0.0) & (slot == window_slot), 1.0, 0.0).astype(BF16)
        acc = acc + wts[:, e:e + 1] * jnp.dot(p, y_refs[e][...], preferred_element_type=F32)
    gate = mod_ref[0][:, 2 * D_MODEL:]
    y = x_ref[...] + gate * acc
    if final:
        ms = jnp.mean(y * y, axis=-1, keepdims=True)
        y = y * lax.rsqrt(ms + EPS) * fg_ref[...]
    o_ref[...] = y


def _route_metadata(cnt_end, n_tok):
    i32 = jnp.int32
    cnt_end = cnt_end[:, :N_EXPERTS]
    cnt_start = jnp.concatenate([jnp.zeros((1, N_EXPERTS), i32), cnt_end[:-1]], axis=0)
    total = cnt_end[-1]
    n_tiles = (total + SLOT_TILE - 1) // SLOT_TILE
    tile_end = jnp.cumsum(n_tiles)
    tile_start = tile_end - n_tiles
    base = (tile_start * SLOT_TILE).astype(i32)
    max_tiles = 2 * n_tok // SLOT_TILE + N_EXPERTS
    i = jnp.arange(max_tiles, dtype=i32)
    tile_e = jnp.minimum(jnp.sum(i[:, None] >= tile_end[None, :], axis=1), N_EXPERTS - 1)
    valid = i < tile_end[-1]
    ls0 = (i - tile_start[tile_e]) * SLOT_TILE
    step = GATHER_CHUNK // CNT_CHUNK
    g_end = cnt_end[step - 1::step][:, tile_e].T
    g_start = cnt_start[::step][:, tile_e].T
    n_chunks = g_end.shape[1]
    c_lo = jnp.sum(g_end <= ls0[:, None], axis=1)
    c_hi = jnp.sum(g_start < (ls0 + SLOT_TILE)[:, None], axis=1) - 1
    n_c = jnp.where(valid, jnp.maximum(c_hi - c_lo + 1, 0), 0)
    c_lo = jnp.minimum(c_lo, n_chunks - 1)
    n_slots = max_tiles * SLOT_TILE
    wstart = jnp.minimum((base[None, :] + cnt_start) // ROW_ALIGN * ROW_ALIGN, n_slots - WINDOW)
    as_i32 = lambda a: a.astype(i32)
    return (as_i32(tile_e), as_i32(ls0), as_i32(c_lo), as_i32(n_c), base,
            as_i32(wstart).reshape(-1))


def _moe_ffn(x, mods, layer, g, wr, br, w1, w3, w2, final_gain, final):
    batch, seq, _ = x.shape
    n_tok = batch * seq
    d_ff = w1.shape[2]
    xt = x.reshape(n_tok, D_MODEL)
    params = lambda: pltpu.CompilerParams(
        dimension_semantics=("arbitrary",), vmem_limit_bytes=VMEM_LIMIT)

    route_tiles_per_seq = seq // ROUTE_TILE
    const = lambda shape: pl.BlockSpec(shape, lambda i: (0,) * len(shape))
    h, rank, wts, rank_row, cnt_end = pl.pallas_call(
        _router_kernel,
        grid=(n_tok // ROUTE_TILE,),
        in_specs=[
            pl.BlockSpec((ROUTE_TILE, D_MODEL), lambda i: (i, 0)),
            pl.BlockSpec((1, 1, 3 * D_MODEL),
                         lambda i: (layer * PAD_B + i // route_tiles_per_seq, 0, 0)),
            const((1, D_MODEL)),
            const((D_MODEL, LANES)),
            const((1, LANES)),
        ],
        out_specs=[
            pl.BlockSpec((ROUTE_TILE, D_MODEL), lambda i: (i, 0)),
            pl.BlockSpec((ROUTE_TILE, LANES), lambda i: (i, 0)),
            pl.BlockSpec((ROUTE_TILE, LANES), lambda i: (i, 0)),
            pl.BlockSpec((N_EXPERTS, ROUTE_TILE), lambda i: (0, i)),
            pl.BlockSpec((ROUTE_TILE // CNT_CHUNK, LANES), lambda i: (i, 0)),
        ],
        out_shape=[
            jax.ShapeDtypeStruct((n_tok, D_MODEL), BF16),
            jax.ShapeDtypeStruct((n_tok, LANES), F32),
            jax.ShapeDtypeStruct((n_tok, LANES), F32),
            jax.ShapeDtypeStruct((N_EXPERTS, n_tok), F32),
            jax.ShapeDtypeStruct((n_tok // CNT_CHUNK, LANES), jnp.int32),
        ],
        scratch_shapes=[pltpu.VMEM((SUBLANES, LANES), F32)],
        compiler_params=params(),
        name="moe_router",
    )(xt, mods, g, wr, br)

    tile_e, ls0, c_lo, n_c, base, wstart = _route_metadata(cnt_end, n_tok)
    max_tiles = tile_e.shape[0]
    n_slots = max_tiles * SLOT_TILE

    x_sorted = pl.pallas_call(
        _dispatch_kernel,
        grid_spec=pltpu.PrefetchScalarGridSpec(
            num_scalar_prefetch=4,
            grid=(max_tiles,),
            in_specs=[pl.BlockSpec(memory_space=pltpu.VMEM),
                      pl.BlockSpec(memory_space=pltpu.VMEM)],
            out_specs=pl.BlockSpec((SLOT_TILE, D_MODEL), lambda i, *_: (i, 0)),
            scratch_shapes=[pltpu.VMEM((SLOT_TILE, D_MODEL), F32)],
        ),
        out_shape=jax.ShapeDtypeStruct((n_slots, D_MODEL), BF16),
        compiler_params=params(),
        name="moe_dispatch",
    )(tile_e, ls0, c_lo, n_c, h,
      rank_row.reshape(N_EXPERTS, n_tok // GATHER_CHUNK, GATHER_CHUNK))

    y_sorted = pl.pallas_call(
        _expert_kernel,
        grid_spec=pltpu.PrefetchScalarGridSpec(
            num_scalar_prefetch=1,
            grid=(max_tiles,),
            in_specs=[
                pl.BlockSpec((SLOT_TILE, D_MODEL), lambda i, te: (i, 0)),
                pl.BlockSpec((1, D_MODEL, d_ff), lambda i, te: (te[i], 0, 0)),
                pl.BlockSpec((1, D_MODEL, d_ff), lambda i, te: (te[i], 0, 0)),
                pl.BlockSpec((1, d_ff, D_MODEL), lambda i, te: (te[i], 0, 0)),
            ],
            out_specs=pl.BlockSpec((SLOT_TILE, D_MODEL), lambda i, te: (i, 0)),
        ),
        out_shape=jax.ShapeDtypeStruct((n_slots, D_MODEL), BF16),
        compiler_params=params(),
        name="moe_experts",
    )(tile_e, x_sorted, w1, w3, w2)

    chunks_per_seq = seq // CNT_CHUNK
    window_spec = lambda e: pl.BlockSpec(
        (pl.Element(WINDOW), pl.Element(D_MODEL)),
        lambda i, ws, bs: (pl.multiple_of(ws[i * N_EXPERTS + e], ROW_ALIGN), 0))
    out = pl.pallas_call(
        functools.partial(_combine_kernel, final=final),
        grid_spec=pltpu.PrefetchScalarGridSpec(
            num_scalar_prefetch=2,
            grid=(n_tok // CNT_CHUNK,),
            in_specs=[
                pl.BlockSpec((CNT_CHUNK, D_MODEL), lambda i, ws, bs: (i, 0)),
                pl.BlockSpec((1, 1, 3 * D_MODEL),
                             lambda i, ws, bs: (layer * PAD_B + i // chunks_per_seq, 0, 0)),
                pl.BlockSpec((CNT_CHUNK, LANES), lambda i, ws, bs: (i, 0)),
                pl.BlockSpec((CNT_CHUNK, LANES), lambda i, ws, bs: (i, 0)),
                pl.BlockSpec((1, D_MODEL), lambda i, ws, bs: (0, 0)),
            ] + [window_spec(e) for e in range(N_EXPERTS)],
            out_specs=pl.BlockSpec((CNT_CHUNK, D_MODEL), lambda i, ws, bs: (i, 0)),
        ),
        out_shape=jax.ShapeDtypeStruct((n_tok, D_MODEL), F32),
        compiler_params=params(),
        name="moe_combine",
    )(wstart, base, xt, mods, rank, wts, final_gain, *([y_sorted] * N_EXPERTS))
    return out.reshape(x.shape)


def _block_diag_halves(w):
    depth, heads, hd, _ = w.shape
    per_tile = MXU_DIM // hd
    w = w.reshape(depth, heads // per_tile, per_tile, hd, hd)
    eye = jnp.eye(per_tile, dtype=w.dtype)
    bd = jnp.einsum("dtiab,ij->dtiajb", w, eye)
    return bd.reshape(depth, heads // per_tile, MXU_DIM, MXU_DIM)


def kernel(x, c, mix_norm, mix_mod_w, mix_mod_b, w_in, lru_conv_w, lru_conv_b, lru_wa, lru_ba,
           lru_wi, lru_bi, lru_lambda, sc_conv_w, gn_lru, gn_sc, w_out, ffn_norm, ffn_mod_w,
           ffn_mod_b, dense_w1, dense_w3, dense_w2, router_w, router_b, exp_w1, exp_w3, exp_w2,
           final_norm):
    depth = w_in.shape[0]
    batch = x.shape[0]
    c_pad = jnp.pad(c, ((0, PAD_B - batch), (0, 0)))
    mix_mods = _modulation(c_pad, mix_mod_w, mix_mod_b).reshape(depth * PAD_B, 1, 3 * D_MODEL)
    ffn_mods = _modulation(c_pad, ffn_mod_w, ffn_mod_b).reshape(depth * PAD_B, 1, 3 * D_MODEL)

    wg = jnp.concatenate([_block_diag_halves(lru_wa), _block_diag_halves(lru_wi)],
                         axis=-1).astype(BF16)
    bg = jnp.stack([lru_ba.reshape(depth, D_LRU), lru_bi.reshape(depth, D_LRU)], axis=1)
    head = jnp.arange(MXU_DIM) // HEAD_DIM
    gmat = jnp.where(head[:, None] == head[None, :], 1.0 / HEAD_DIM, 0.0).astype(BF16)
    w_in_b = w_in.astype(BF16)
    w_out_b = w_out.astype(BF16)
    dense_w1_b, dense_w3_b, dense_w2_b = (w.astype(BF16) for w in (dense_w1, dense_w3, dense_w2))
    exp_w1_b, exp_w3_b, exp_w2_b = (w.astype(BF16) for w in (exp_w1, exp_w3, exp_w2))
    n_moe = router_w.shape[0]
    wr = jnp.pad(router_w, ((0, 0), (0, 0), (0, LANES - N_EXPERTS)))
    br = jnp.pad(router_b, ((0, 0), (0, LANES - N_EXPERTS)),
                 constant_values=-jnp.inf).reshape(n_moe, 1, LANES)
    final_gain = final_norm.reshape(1, D_MODEL)

    for l in range(depth):
        x = _mixer(x, mix_mods, l, mix_norm[l].reshape(1, D_MODEL), w_in_b[l], lru_conv_w[l],
                   lru_conv_b[l].reshape(1, D_LRU), wg[l], bg[l], lru_lambda[l].reshape(1, D_LRU),
                   sc_conv_w[l], gn_lru[l].reshape(1, D_LRU), gn_sc[l].reshape(1, D_SC), gmat,
                   w_out_b[l])
        j = l // 2
        g = ffn_norm[l].reshape(1, D_MODEL)
        if l % 2 == 0:
            x = _dense_ffn(x, ffn_mods, l, g, dense_w1_b[j], dense_w3_b[j], dense_w2_b[j])
        else:
            x = _moe_ffn(x, ffn_mods, l, g, wr[j], br[j], exp_w1_b[j], exp_w3_b[j], exp_w2_b[j],
                         final_gain, final=(l == depth - 1))
    if depth % 2 == 1:
        raise NotImplementedError("final norm is fused into the last MoE layer")
    return x
```

```python
import functools

import jax
import jax.numpy as jnp
from jax import lax
from jax.experimental import pallas as pl
from jax.experimental.pallas import tpu as pltpu

F32 = jnp.float32
BF16 = jnp.bfloat16

D_MODEL = 1024
D_LRU = 512
D_SC = 512
D_IN = 2 * D_LRU + 3 * D_SC
HEAD_DIM = 64
LRU_CONV = 4
SC_CONV = 3
LRU_C = 8.0
N_EXPERTS = 8
EPS = 1e-6

SUBLANES = 8
LANES = 128
MXU_DIM = 256
TS = SUBLANES ** 3
PAD_B = 8
VMEM_LIMIT = 56 * 1024 * 1024


def _mod_norm(x, g, scale, shift):
    ms = jnp.mean(x * x, axis=-1, keepdims=True)
    return x * lax.rsqrt(ms + EPS) * (g * (1.0 + scale)) + shift


def _split_mod(mod):
    return mod[:, :D_MODEL], mod[:, D_MODEL:2 * D_MODEL], mod[:, 2 * D_MODEL:]


def _layer_spec(shape, layer):
    zeros = (0,) * len(shape)
    return pl.BlockSpec((1, *shape), lambda *_: (layer, *zeros))


def _mod_kernel(c_ref, w_ref, b_ref, o_ref):
    c = c_ref[...]
    s = (c * jax.nn.sigmoid(c)).astype(BF16)
    o_ref[0] = jnp.dot(s, w_ref[0].astype(BF16), preferred_element_type=F32) + b_ref[0]


def _modulation(c_pad, w, b):
    depth = w.shape[0]
    n_col = 3 * D_MODEL // D_MODEL
    return pl.pallas_call(
        _mod_kernel,
        grid=(depth, n_col),
        in_specs=[
            pl.BlockSpec((PAD_B, D_MODEL), lambda l, j: (0, 0)),
            pl.BlockSpec((1, D_MODEL, D_MODEL), lambda l, j: (l, 0, j)),
            pl.BlockSpec((1, 1, D_MODEL), lambda l, j: (l, 0, j)),
        ],
        out_specs=pl.BlockSpec((1, PAD_B, D_MODEL), lambda l, j: (l, 0, j)),
        out_shape=jax.ShapeDtypeStruct((depth, PAD_B, 3 * D_MODEL), F32),
        compiler_params=pltpu.CompilerParams(
            dimension_semantics=("arbitrary", "arbitrary"), vmem_limit_bytes=VMEM_LIMIT),
        name="adaln_mod",
    )(c_pad, w, b.reshape(depth, 1, 3 * D_MODEL))


def _scan8(a, u, row_in_group):
    for d in (1, 2, 4):
        keep = row_in_group >= d
        a_sh = jnp.where(keep, pltpu.roll(a, d, axis=0), 1.0)
        u_sh = jnp.where(keep, pltpu.roll(u, d, axis=0), 0.0)
        u = a * u_sh + u
        a = a * a_sh
    return a, u


def _row_in_group(rows, cols):
    return lax.broadcasted_iota(jnp.int32, (rows, cols), 0) % SUBLANES


def _shift_rows_down(x, first_row):
    row = lax.broadcasted_iota(jnp.int32, x.shape, 0)
    return jnp.where(row == 0, first_row, pltpu.roll(x, 1, axis=0))


def _group_rms(y, gmat_ref, gain):
    y2 = (y * y).astype(BF16)
    ms = jnp.concatenate(
        [jnp.dot(y2[:, :MXU_DIM], gmat_ref[...], preferred_element_type=F32),
         jnp.dot(y2[:, MXU_DIM:], gmat_ref[...], preferred_element_type=F32)], axis=-1)
    return y * lax.rsqrt(ms + EPS) * gain


def _mixer_kernel(x_ref, mod_ref, g_ref, win_ref, lcw_ref, lcb_ref, wg_ref, bg_ref, lam_ref,
                  scw_ref, gnl_ref, gns_ref, gmat_ref, wout_ref, o_ref,
                  xl_buf, v_buf, a0_ref, u0_ref, a1_ref, u1_ref, p1_ref, p2_ref, carry_ref):
    ts = TS
    n1 = ts // SUBLANES
    n2 = n1 // SUBLANES

    @pl.when(pl.program_id(1) == 0)
    def _():
        xl_buf[0:SUBLANES, :] = jnp.zeros((SUBLANES, D_LRU), F32)
        v_buf[0:SUBLANES, :] = jnp.zeros((SUBLANES, D_SC), F32)
        carry_ref[...] = jnp.zeros(carry_ref.shape, F32)

    x = x_ref[0]
    shift, scale, gate = _split_mod(mod_ref[0])
    h = _mod_norm(x, g_ref[0], scale, shift).astype(BF16)
    u = jnp.dot(h, win_ref[0], preferred_element_type=F32)
    x_lru = u[:, 0:D_LRU]
    g_lru = u[:, D_LRU:2 * D_LRU]
    b_sc = u[:, 2 * D_LRU:2 * D_LRU + D_SC]
    c_sc = u[:, 2 * D_LRU + D_SC:2 * D_LRU + 2 * D_SC]
    x_sc = u[:, 2 * D_LRU + 2 * D_SC:]

    xl_buf[SUBLANES:SUBLANES + ts, :] = x_lru
    xc = lcb_ref[0]
    for k in range(LRU_CONV):
        off = SUBLANES - (LRU_CONV - 1) + k
        xc = xc + lcw_ref[0, k:k + 1, :] * xl_buf[pl.ds(off, ts), :]
    xl_buf[0:SUBLANES, :] = xl_buf[ts:ts + SUBLANES, :]

    v_buf[SUBLANES:SUBLANES + ts, :] = c_sc * x_sc
    cv = jnp.zeros((ts, D_SC), F32)
    for k in range(SC_CONV):
        off = SUBLANES - (SC_CONV - 1) + k
        cv = cv + scw_ref[0, k:k + 1, :] * v_buf[pl.ds(off, ts), :]
    v_buf[0:SUBLANES, :] = v_buf[ts:ts + SUBLANES, :]
    y_sc = b_sc * cv

    xcb = xc.astype(BF16)
    g0 = jnp.dot(xcb[:, :MXU_DIM], wg_ref[0, 0], preferred_element_type=F32)
    g1 = jnp.dot(xcb[:, MXU_DIM:], wg_ref[0, 1], preferred_element_type=F32)
    r = jax.nn.sigmoid(jnp.concatenate([g0[:, :MXU_DIM], g1[:, :MXU_DIM]], axis=-1)
                       + bg_ref[0, 0:1, :])
    i = jax.nn.sigmoid(jnp.concatenate([g0[:, MXU_DIM:], g1[:, MXU_DIM:]], axis=-1)
                       + bg_ref[0, 1:2, :])
    log_a = r * ((-LRU_C) * jax.nn.softplus(-lam_ref[0]))
    a = jnp.exp(log_a)
    uin = jnp.sqrt(-jnp.tanh(log_a) * (1.0 + a * a)) * (i * xc)

    last = SUBLANES - 1
    hs_blocks = []
    for lb in range(D_LRU // LANES):
        lanes = slice(lb * LANES, (lb + 1) * LANES)
        h_prev = carry_ref[lb, last:last + 1, :]
        a0, u0 = _scan8(a[:, lanes], uin[:, lanes], _row_in_group(ts, LANES))
        a0_ref[lb] = a0
        u0_ref[lb] = u0
        a1, u1 = _scan8(a0_ref[lb, pl.ds(last, n1, stride=SUBLANES), :],
                        u0_ref[lb, pl.ds(last, n1, stride=SUBLANES), :],
                        _row_in_group(n1, LANES))
        a1_ref[lb] = a1
        u1_ref[lb] = u1
        a2, u2 = _scan8(a1_ref[lb, pl.ds(last, n2, stride=SUBLANES), :],
                        u1_ref[lb, pl.ds(last, n2, stride=SUBLANES), :],
                        _row_in_group(n2, LANES))
        h2 = u2 + a2 * h_prev
        carry_ref[lb] = h2
        p2_ref[lb] = _shift_rows_down(h2, h_prev)
        p2x = jnp.concatenate(
            [jnp.broadcast_to(p2_ref[lb, j:j + 1, :], (SUBLANES, LANES)) for j in range(n2)],
            axis=0)
        h1 = u1 + a1 * p2x
        p1_ref[lb] = _shift_rows_down(h1, h_prev)
        for g in range(n1):
            rows = pl.ds(g * SUBLANES, SUBLANES)
            u0_ref[lb, rows, :] = (u0_ref[lb, rows, :]
                                   + a0_ref[lb, rows, :] * p1_ref[lb, g:g + 1, :])
        hs_blocks.append(u0_ref[lb])
    hs = jnp.concatenate(hs_blocks, axis=-1)

    y_lru = jax.nn.gelu(g_lru) * hs
    yn = jnp.concatenate([_group_rms(y_lru, gmat_ref, gnl_ref[0]),
                          _group_rms(y_sc, gmat_ref, gns_ref[0])], axis=-1).astype(BF16)
    out = jnp.dot(yn, wout_ref[0], preferred_element_type=F32)
    o_ref[0] = x + gate * out


def _mixer(x, mods, layer, g, w_in, lcw, lcb, wg, bg, lam, scw, gnl, gns, gmat, w_out):
    batch, seq, _ = x.shape
    ts = TS
    nlb = D_LRU // LANES
    per_layer = lambda shape: _layer_spec(shape, layer)
    return pl.pallas_call(
        _mixer_kernel,
        grid=(batch, seq // ts),
        in_specs=[
            pl.BlockSpec((1, ts, D_MODEL), lambda b, t: (b, t, 0)),
            pl.BlockSpec((1, 1, 3 * D_MODEL), lambda b, t: (layer * PAD_B + b, 0, 0)),
            per_layer((1, D_MODEL)),
            per_layer((D_MODEL, D_IN)),
            per_layer((LRU_CONV, D_LRU)),
            per_layer((1, D_LRU)),
            per_layer((2, MXU_DIM, 2 * MXU_DIM)),
            per_layer((2, D_LRU)),
            per_layer((1, D_LRU)),
            per_layer((SC_CONV, D_SC)),
            per_layer((1, D_LRU)),
            per_layer((1, D_SC)),
            pl.BlockSpec((MXU_DIM, MXU_DIM), lambda b, t: (0, 0)),
            per_layer((D_LRU + D_SC, D_MODEL)),
        ],
        out_specs=pl.BlockSpec((1, ts, D_MODEL), lambda b, t: (b, t, 0)),
        out_shape=jax.ShapeDtypeStruct(x.shape, F32),
        scratch_shapes=[
            pltpu.VMEM((ts + SUBLANES, D_LRU), F32),
            pltpu.VMEM((ts + SUBLANES, D_SC), F32),
            pltpu.VMEM((nlb, ts, LANES), F32),
            pltpu.VMEM((nlb, ts, LANES), F32),
            pltpu.VMEM((nlb, ts // SUBLANES, LANES), F32),
            pltpu.VMEM((nlb, ts // SUBLANES, LANES), F32),
            pltpu.VMEM((nlb, ts // SUBLANES, LANES), F32),
            pltpu.VMEM((nlb, SUBLANES, LANES), F32),
            pltpu.VMEM((nlb, SUBLANES, LANES), F32),
        ],
        compiler_params=pltpu.CompilerParams(
            dimension_semantics=("arbitrary", "arbitrary"), vmem_limit_bytes=VMEM_LIMIT),
        name="mixer",
    )(x, mods, g, w_in, lcw, lcb, wg, bg, lam, scw, gnl, gns, gmat, w_out)


def _swiglu(h, w1, w3, w2):
    a = jnp.dot(h, w1, preferred_element_type=F32)
    b = jnp.dot(h, w3, preferred_element_type=F32)
    g = (a * jax.nn.sigmoid(a) * b).astype(BF16)
    return jnp.dot(g, w2, preferred_element_type=F32)


def _dense_ffn_kernel(x_ref, mod_ref, g_ref, w1_ref, w3_ref, w2_ref, o_ref, *, n_chunks):
    x = x_ref[0]
    shift, scale, gate = _split_mod(mod_ref[0])
    h = _mod_norm(x, g_ref[0], scale, shift).astype(BF16)
    fc = w1_ref.shape[2] // n_chunks
    f = jnp.zeros(x.shape, F32)
    for c in range(n_chunks):
        cols = slice(c * fc, (c + 1) * fc)
        f = f + _swiglu(h, w1_ref[0, :, cols], w3_ref[0, :, cols], w2_ref[0, cols, :])
    o_ref[0] = x + gate * f


def _dense_ffn(x, mods, layer, g, w1, w3, w2, tm=512, n_chunks=2):
    batch, seq, _ = x.shape
    d_ff = w1.shape[2]
    j = layer // 2
    return pl.pallas_call(
        functools.partial(_dense_ffn_kernel, n_chunks=n_chunks),
        grid=(batch, seq // tm),
        in_specs=[
            pl.BlockSpec((1, tm, D_MODEL), lambda b, t: (b, t, 0)),
            pl.BlockSpec((1, 1, 3 * D_MODEL), lambda b, t: (layer * PAD_B + b, 0, 0)),
            _layer_spec((1, D_MODEL), layer),
            _layer_spec((D_MODEL, d_ff), j),
            _layer_spec((D_MODEL, d_ff), j),
            _layer_spec((d_ff, D_MODEL), j),
        ],
        out_specs=pl.BlockSpec((1, tm, D_MODEL), lambda b, t: (b, t, 0)),
        out_shape=jax.ShapeDtypeStruct(x.shape, F32),
        compiler_params=pltpu.CompilerParams(
            dimension_semantics=("arbitrary", "arbitrary"), vmem_limit_bytes=VMEM_LIMIT),
        name="dense_ffn",
    )(x, mods, g, w1, w3, w2)


ROUTE_TILE = 1024
CNT_CHUNK = 128
COMBINE_CHUNKS = 4
SLOT_TILE = 512
GATHER_SLOTS = 128
GATHER_CHUNKS = 6
ROW_ALIGN = 16
WINDOW = CNT_CHUNK + ROW_ALIGN


def _top2(logits):
    lane = lax.broadcasted_iota(jnp.int32, logits.shape, 1)
    m1 = jnp.max(logits, axis=-1, keepdims=True)
    i1 = jnp.min(jnp.where(logits == m1, lane, LANES), axis=-1, keepdims=True)
    rest = jnp.where(lane == i1, -jnp.inf, logits)
    m2 = jnp.max(rest, axis=-1, keepdims=True)
    i2 = jnp.min(jnp.where(rest == m2, lane, LANES), axis=-1, keepdims=True)
    e2 = jnp.exp(m2 - m1)
    w_first = 1.0 / (1.0 + e2)
    w_second = e2 / (1.0 + e2)
    first = lane == i1
    second = lane == i2
    weights = jnp.where(first, w_first, 0.0) + jnp.where(second, w_second, 0.0)
    return weights, first | second


def _split_bf16(v):
    hi = v.astype(BF16)
    return hi, (v - hi.astype(F32)).astype(BF16)


def _router_kernel(x_ref, mod_ref, g_ref, wr_ref, br_ref,
                   h_ref, rank_ref, w_ref, rank_row_ref, cnt_ref, carry_ref):
    n = ROUTE_TILE
    n_chunks = n // CNT_CHUNK

    @pl.when(pl.program_id(0) == 0)
    def _():
        carry_ref[...] = jnp.zeros(carry_ref.shape, F32)

    shift, scale, _ = _split_mod(mod_ref[0])
    h = _mod_norm(x_ref[...], g_ref[0], scale, shift)
    h_hi, h_lo = _split_bf16(h)
    w_hi, w_lo = _split_bf16(wr_ref[0])
    both = jnp.dot(h_hi, jnp.concatenate([w_hi, w_lo], axis=1), preferred_element_type=F32)
    logits = (both[:, :LANES] + both[:, LANES:]
              + jnp.dot(h_lo, w_hi, preferred_element_type=F32)) + br_ref[0]
    weights, mask = _top2(logits)
    m = jnp.where(mask, 1.0, 0.0).astype(BF16)
    before = carry_ref[0:1, :]
    chunk = lax.broadcasted_iota(jnp.int32, (n_chunks, n), 0)
    tok = lax.broadcasted_iota(jnp.int32, (n_chunks, n), 1)
    in_earlier_chunk = jnp.where(tok < chunk * CNT_CHUNK, 1.0, 0.0).astype(BF16)
    upto_chunk_end = jnp.where(tok < (chunk + 1) * CNT_CHUNK, 1.0, 0.0).astype(BF16)
    chunk_start = jnp.dot(in_earlier_chunk, m, preferred_element_type=F32) + before
    cnt = jnp.dot(upto_chunk_end, m, preferred_element_type=F32) + before
    row = lax.broadcasted_iota(jnp.int32, (CNT_CHUNK, CNT_CHUNK), 0)
    col = lax.broadcasted_iota(jnp.int32, (CNT_CHUNK, CNT_CHUNK), 1)
    strictly_lower = jnp.where(row > col, 1.0, 0.0).astype(BF16)
    in_chunk = jnp.concatenate(
        [jnp.dot(strictly_lower, m[j * CNT_CHUNK:(j + 1) * CNT_CHUNK, :],
                 preferred_element_type=F32) + chunk_start[j:j + 1, :]
         for j in range(n_chunks)], axis=0)
    rank = jnp.where(mask, in_chunk, -1.0)
    carry_ref[...] = jnp.broadcast_to(cnt[n_chunks - 1:, :], carry_ref.shape)
    h_ref[...] = h_hi
    rank_ref[...] = rank
    w_ref[...] = weights
    rank_row_ref[...] = rank.T[:N_EXPERTS, :]
    cnt_ref[...] = cnt.astype(jnp.int32)


def _dispatch_kernel(tile_e_ref, ls0_ref, c_lo_ref, n_g_ref, h_ref, rank_ref, o_ref):
    i = pl.program_id(0)
    e = tile_e_ref[i]
    n_chunks = rank_ref.shape[1]
    span = GATHER_CHUNKS * CNT_CHUNK
    subs = SLOT_TILE // GATHER_SLOTS

    def gather(sub, k):
        q = i * subs + sub
        slot = (ls0_ref[q]
                + lax.broadcasted_iota(jnp.int32, (GATHER_SLOTS, CNT_CHUNK), 0)).astype(F32)
        wanted = c_lo_ref[q] + k * GATHER_CHUNKS
        start = jnp.minimum(wanted, n_chunks - GATHER_CHUNKS)
        p = jnp.concatenate(
            [jnp.where((start + j >= wanted) & (rank_ref[e, pl.ds(start + j, 1), :] == slot),
                       1.0, 0.0) for j in range(GATHER_CHUNKS)], axis=1).astype(BF16)
        tokens = pl.ds(pl.multiple_of(start * CNT_CHUNK, CNT_CHUNK), span)
        return jnp.dot(p, h_ref[tokens, :], preferred_element_type=F32).astype(BF16)

    for sub in range(subs):
        o_ref[sub * GATHER_SLOTS:(sub + 1) * GATHER_SLOTS, :] = gather(sub, 0)
    for sub in range(subs):
        def more(k, carry, sub=sub):
            o_ref[sub * GATHER_SLOTS:(sub + 1) * GATHER_SLOTS, :] += gather(sub, k)
            return carry

        lax.fori_loop(1, n_g_ref[i * subs + sub], more, 0)


def _expert_kernel(tile_e_ref, x_ref, w1_ref, w3_ref, w2_ref, o_ref, w1b_ref, w3b_ref, w2b_ref):
    i = pl.program_id(0)

    @pl.when((i == 0) | (tile_e_ref[i] != tile_e_ref[jnp.maximum(i - 1, 0)]))
    def _():
        w1b_ref[...] = w1_ref[0, 0].astype(BF16)
        w3b_ref[...] = w3_ref[0, 0].astype(BF16)
        w2b_ref[...] = w2_ref[0, 0].astype(BF16)

    x = x_ref[...]
    d_ff = w1b_ref.shape[1]
    split = pl.cdiv(d_ff // 2, MXU_DIM) * MXU_DIM
    f = jnp.zeros((x.shape[0], D_MODEL), F32)
    for cols in (slice(0, split), slice(split, d_ff)):
        f = f + _swiglu(x, w1b_ref[:, cols], w3b_ref[:, cols], w2b_ref[cols, :])
    o_ref[...] = f.astype(BF16)


def _combine_kernel(wstart_ref, base_ref, x_ref, mod_ref, rank_ref, w_ref, fg_ref, *rest, final):
    n_win = COMBINE_CHUNKS * N_EXPERTS
    y_refs, o_ref = rest[:n_win], rest[n_win]
    i = pl.program_id(0)
    lane = lax.broadcasted_iota(jnp.int32, (CNT_CHUNK, WINDOW), 1)
    gate = mod_ref[0][:, 2 * D_MODEL:]
    for s in range(COMBINE_CHUNKS):
        rows = slice(s * CNT_CHUNK, (s + 1) * CNT_CHUNK)
        rank = rank_ref[rows, :]
        wts = w_ref[rows, :]
        acc = jnp.zeros((CNT_CHUNK, D_MODEL), F32)
        for e in range(N_EXPERTS):
            w = s * N_EXPERTS + e
            rk = rank[:, e:e + 1]
            slot = rk + base_ref[e].astype(F32)
            window_slot = (wstart_ref[i * n_win + w] + lane).astype(F32)
            hit = (rk >= 0.0) & (slot == window_slot)
            p = jnp.where(hit, wts[:, e:e + 1], 0.0).astype(BF16)
            acc = acc + jnp.dot(p, y_refs[w][...], preferred_element_type=F32)
        y = x_ref[rows, :] + gate * acc
        if final:
            ms = jnp.mean(y * y, axis=-1, keepdims=True)
            y = y * lax.rsqrt(ms + EPS) * fg_ref[...]
        o_ref[rows, :] = y


def _route_metadata(cnt_end, n_tok):
    i32 = jnp.int32
    cnt_end = cnt_end[:, :N_EXPERTS]
    cnt_start = jnp.concatenate([jnp.zeros((1, N_EXPERTS), i32), cnt_end[:-1]], axis=0)
    n_chunks = cnt_end.shape[0]
    total = cnt_end[-1]
    n_tiles = (total + SLOT_TILE - 1) // SLOT_TILE
    tile_end = jnp.cumsum(n_tiles)
    tile_start = tile_end - n_tiles
    base = (tile_start * SLOT_TILE).astype(i32)
    max_tiles = 2 * n_tok // SLOT_TILE + N_EXPERTS
    i = jnp.arange(max_tiles, dtype=i32)
    tile_e = jnp.minimum(jnp.sum(i[:, None] >= tile_end[None, :], axis=1), N_EXPERTS - 1)
    subs = SLOT_TILE // GATHER_SLOTS
    q = jnp.arange(max_tiles * subs, dtype=i32)
    q_e = tile_e[q // subs]
    ls0 = (q // subs - tile_start[q_e]) * SLOT_TILE + (q % subs) * GATHER_SLOTS
    live = (q // subs < tile_end[-1]) & (ls0 < total[q_e])
    ends = cnt_end[:, q_e].T
    starts = cnt_start[:, q_e].T
    c_lo = jnp.sum(ends <= ls0[:, None], axis=1)
    c_hi = jnp.sum(starts < (ls0 + GATHER_SLOTS)[:, None], axis=1) - 1
    n_g = jnp.where(live, (c_hi - c_lo + GATHER_CHUNKS) // GATHER_CHUNKS, 0)
    c_lo = jnp.minimum(c_lo, n_chunks - 1)
    n_slots = max_tiles * SLOT_TILE
    wstart = jnp.minimum((base[None, :] + cnt_start) // ROW_ALIGN * ROW_ALIGN, n_slots - WINDOW)
    as_i32 = lambda a: a.astype(i32)
    return (as_i32(tile_e), as_i32(ls0), as_i32(c_lo), as_i32(n_g), base,
            as_i32(wstart).reshape(-1))


def _moe_ffn(x, mods, layer, g, wr, br, w1, w3, w2, final_gain, final):
    batch, seq, _ = x.shape
    n_tok = batch * seq
    d_ff = w1.shape[3]
    j = layer // 2
    xt = x.reshape(n_tok, D_MODEL)
    params = lambda: pltpu.CompilerParams(
        dimension_semantics=("arbitrary",), vmem_limit_bytes=VMEM_LIMIT)

    route_tiles_per_seq = seq // ROUTE_TILE
    h, rank, wts, rank_row, cnt_end = pl.pallas_call(
        _router_kernel,
        grid=(n_tok // ROUTE_TILE,),
        in_specs=[
            pl.BlockSpec((ROUTE_TILE, D_MODEL), lambda i: (i, 0)),
            pl.BlockSpec((1, 1, 3 * D_MODEL),
                         lambda i: (layer * PAD_B + i // route_tiles_per_seq, 0, 0)),
            _layer_spec((1, D_MODEL), layer),
            _layer_spec((D_MODEL, LANES), j),
            _layer_spec((1, LANES), j),
        ],
        out_specs=[
            pl.BlockSpec((ROUTE_TILE, D_MODEL), lambda i: (i, 0)),
            pl.BlockSpec((ROUTE_TILE, LANES), lambda i: (i, 0)),
            pl.BlockSpec((ROUTE_TILE, LANES), lambda i: (i, 0)),
            pl.BlockSpec((N_EXPERTS, ROUTE_TILE), lambda i: (0, i)),
            pl.BlockSpec((ROUTE_TILE // CNT_CHUNK, LANES), lambda i: (i, 0)),
        ],
        out_shape=[
            jax.ShapeDtypeStruct((n_tok, D_MODEL), BF16),
            jax.ShapeDtypeStruct((n_tok, LANES), F32),
            jax.ShapeDtypeStruct((n_tok, LANES), F32),
            jax.ShapeDtypeStruct((N_EXPERTS, n_tok), F32),
            jax.ShapeDtypeStruct((n_tok // CNT_CHUNK, LANES), jnp.int32),
        ],
        scratch_shapes=[pltpu.VMEM((SUBLANES, LANES), F32)],
        compiler_params=params(),
        name="moe_router",
    )(xt, mods, g, wr, br)

    tile_e, ls0, c_lo, n_g, base, wstart = _route_metadata(cnt_end, n_tok)
    max_tiles = tile_e.shape[0]
    n_slots = max_tiles * SLOT_TILE

    x_sorted = pl.pallas_call(
        _dispatch_kernel,
        grid_spec=pltpu.PrefetchScalarGridSpec(
            num_scalar_prefetch=4,
            grid=(max_tiles,),
            in_specs=[pl.BlockSpec(memory_space=pltpu.VMEM),
                      pl.BlockSpec(memory_space=pltpu.VMEM)],
            out_specs=pl.BlockSpec((SLOT_TILE, D_MODEL), lambda i, *_: (i, 0)),
        ),
        out_shape=jax.ShapeDtypeStruct((n_slots, D_MODEL), BF16),
        compiler_params=params(),
        name="moe_dispatch",
    )(tile_e, ls0, c_lo, n_g, h, rank_row.reshape(N_EXPERTS, n_tok // CNT_CHUNK, CNT_CHUNK))

    y_sorted = pl.pallas_call(
        _expert_kernel,
        grid_spec=pltpu.PrefetchScalarGridSpec(
            num_scalar_prefetch=1,
            grid=(max_tiles,),
            in_specs=[
                pl.BlockSpec((SLOT_TILE, D_MODEL), lambda i, te: (i, 0)),
                pl.BlockSpec((1, 1, D_MODEL, d_ff), lambda i, te: (j, te[i], 0, 0)),
                pl.BlockSpec((1, 1, D_MODEL, d_ff), lambda i, te: (j, te[i], 0, 0)),
                pl.BlockSpec((1, 1, d_ff, D_MODEL), lambda i, te: (j, te[i], 0, 0)),
            ],
            out_specs=pl.BlockSpec((SLOT_TILE, D_MODEL), lambda i, te: (i, 0)),
            scratch_shapes=[pltpu.VMEM((D_MODEL, d_ff), BF16),
                            pltpu.VMEM((D_MODEL, d_ff), BF16),
                            pltpu.VMEM((d_ff, D_MODEL), BF16)],
        ),
        out_shape=jax.ShapeDtypeStruct((n_slots, D_MODEL), BF16),
        compiler_params=params(),
        name="moe_experts",
    )(tile_e, x_sorted, w1, w3, w2)

    rows = COMBINE_CHUNKS * CNT_CHUNK
    steps_per_seq = seq // rows
    n_win = COMBINE_CHUNKS * N_EXPERTS
    window_spec = lambda w: pl.BlockSpec(
        (pl.Element(WINDOW), pl.Element(D_MODEL)),
        lambda i, ws, bs: (pl.multiple_of(ws[i * n_win + w], ROW_ALIGN), 0))
    out = pl.pallas_call(
        functools.partial(_combine_kernel, final=final),
        grid_spec=pltpu.PrefetchScalarGridSpec(
            num_scalar_prefetch=2,
            grid=(n_tok // rows,),
            in_specs=[
                pl.BlockSpec((rows, D_MODEL), lambda i, ws, bs: (i, 0)),
                pl.BlockSpec((1, 1, 3 * D_MODEL),
                             lambda i, ws, bs: (layer * PAD_B + i // steps_per_seq, 0, 0)),
                pl.BlockSpec((rows, LANES), lambda i, ws, bs: (i, 0)),
                pl.BlockSpec((rows, LANES), lambda i, ws, bs: (i, 0)),
                pl.BlockSpec((1, D_MODEL), lambda i, ws, bs: (0, 0)),
            ] + [window_spec(w) for w in range(n_win)],
            out_specs=pl.BlockSpec((rows, D_MODEL), lambda i, ws, bs: (i, 0)),
        ),
        out_shape=jax.ShapeDtypeStruct((n_tok, D_MODEL), F32),
        compiler_params=params(),
        name="moe_combine",
    )(wstart, base, xt, mods, rank, wts, final_gain, *([y_sorted] * n_win))
    return out.reshape(x.shape)


def _block_diag_halves(w):
    depth, heads, hd, _ = w.shape
    per_tile = MXU_DIM // hd
    w = w.reshape(depth, heads // per_tile, per_tile, hd, hd)
    eye = jnp.eye(per_tile, dtype=w.dtype)
    bd = jnp.einsum("dtiab,ij->dtiajb", w, eye)
    return bd.reshape(depth, heads // per_tile, MXU_DIM, MXU_DIM)


def kernel(x, c, mix_norm, mix_mod_w, mix_mod_b, w_in, lru_conv_w, lru_conv_b, lru_wa, lru_ba,
           lru_wi, lru_bi, lru_lambda, sc_conv_w, gn_lru, gn_sc, w_out, ffn_norm, ffn_mod_w,
           ffn_mod_b, dense_w1, dense_w3, dense_w2, router_w, router_b, exp_w1, exp_w3, exp_w2,
           final_norm):
    depth = w_in.shape[0]
    batch = x.shape[0]
    c_pad = jnp.pad(c, ((0, PAD_B - batch), (0, 0)))
    mix_mods = _modulation(c_pad, mix_mod_w, mix_mod_b).reshape(depth * PAD_B, 1, 3 * D_MODEL)
    ffn_mods = _modulation(c_pad, ffn_mod_w, ffn_mod_b).reshape(depth * PAD_B, 1, 3 * D_MODEL)

    row = lambda p: p.reshape(depth, 1, p.shape[-1])
    wg = jnp.concatenate([_block_diag_halves(lru_wa), _block_diag_halves(lru_wi)],
                         axis=-1).astype(BF16)
    bg = jnp.stack([lru_ba.reshape(depth, D_LRU), lru_bi.reshape(depth, D_LRU)], axis=1)
    head = jnp.arange(MXU_DIM) // HEAD_DIM
    gmat = jnp.where(head[:, None] == head[None, :], 1.0 / HEAD_DIM, 0.0).astype(BF16)
    w_in_b = w_in.astype(BF16)
    w_out_b = w_out.astype(BF16)
    dense_w1_b, dense_w3_b, dense_w2_b = (w.astype(BF16) for w in (dense_w1, dense_w3, dense_w2))
    n_moe = router_w.shape[0]
    wr = jnp.pad(router_w, ((0, 0), (0, 0), (0, LANES - N_EXPERTS)))
    br = jnp.pad(router_b, ((0, 0), (0, LANES - N_EXPERTS)),
                 constant_values=-jnp.inf).reshape(n_moe, 1, LANES)
    final_gain = final_norm.reshape(1, D_MODEL)
    mix_g, ffn_g = row(mix_norm), row(ffn_norm)
    lcb, lam, gnl, gns = row(lru_conv_b), row(lru_lambda), row(gn_lru), row(gn_sc)

    for l in range(depth):
        x = _mixer(x, mix_mods, l, mix_g, w_in_b, lru_conv_w, lcb, wg, bg, lam, sc_conv_w, gnl,
                   gns, gmat, w_out_b)
        if l % 2 == 0:
            x = _dense_ffn(x, ffn_mods, l, ffn_g, dense_w1_b, dense_w3_b, dense_w2_b)
        else:
            x = _moe_ffn(x, ffn_mods, l, ffn_g, wr, br, exp_w1, exp_w3, exp_w2, final_gain,
                         final=(l == depth - 1))
    if depth % 2 == 1:
        raise NotImplementedError("final norm is fused into the last MoE layer")
    return x
```

```python
import functools

import jax
import jax.numpy as jnp
from jax import lax
from jax.experimental import pallas as pl
from jax.experimental.pallas import tpu as pltpu

F32 = jnp.float32
BF16 = jnp.bfloat16

D_MODEL = 1024
D_LRU = 512
D_SC = 512
D_IN = 2 * D_LRU + 3 * D_SC
HEAD_DIM = 64
LRU_CONV = 4
SC_CONV = 3
LRU_C = 8.0
N_EXPERTS = 8
EPS = 1e-6

SUBLANES = 8
LANES = 128
MXU_DIM = 256
TS = SUBLANES ** 3
PAD_B = 8
VMEM_LIMIT = 56 * 1024 * 1024


def _mod_norm(x, g, scale, shift):
    ms = jnp.mean(x * x, axis=-1, keepdims=True)
    return x * lax.rsqrt(ms + EPS) * (g * (1.0 + scale)) + shift


def _split_mod(mod):
    return mod[:, :D_MODEL], mod[:, D_MODEL:2 * D_MODEL], mod[:, 2 * D_MODEL:]


def _layer_spec(shape, layer):
    zeros = (0,) * len(shape)
    return pl.BlockSpec((1, *shape), lambda *_: (layer, *zeros))


def _mod_kernel(c_ref, w_ref, b_ref, o_ref):
    c = c_ref[...]
    s = (c * jax.nn.sigmoid(c)).astype(BF16)
    o_ref[0] = jnp.dot(s, w_ref[0].astype(BF16), preferred_element_type=F32) + b_ref[0]


def _modulation(c_pad, w, b):
    depth = w.shape[0]
    n_col = 3 * D_MODEL // D_MODEL
    return pl.pallas_call(
        _mod_kernel,
        grid=(depth, n_col),
        in_specs=[
            pl.BlockSpec((PAD_B, D_MODEL), lambda l, j: (0, 0)),
            pl.BlockSpec((1, D_MODEL, D_MODEL), lambda l, j: (l, 0, j)),
            pl.BlockSpec((1, 1, D_MODEL), lambda l, j: (l, 0, j)),
        ],
        out_specs=pl.BlockSpec((1, PAD_B, D_MODEL), lambda l, j: (l, 0, j)),
        out_shape=jax.ShapeDtypeStruct((depth, PAD_B, 3 * D_MODEL), F32),
        compiler_params=pltpu.CompilerParams(
            dimension_semantics=("arbitrary", "arbitrary"), vmem_limit_bytes=VMEM_LIMIT),
        name="adaln_mod",
    )(c_pad, w, b.reshape(depth, 1, 3 * D_MODEL))


def _scan8(a, u, row_in_group):
    for d in (1, 2, 4):
        keep = row_in_group >= d
        a_sh = jnp.where(keep, pltpu.roll(a, d, axis=0), 1.0)
        u_sh = jnp.where(keep, pltpu.roll(u, d, axis=0), 0.0)
        u = a * u_sh + u
        a = a * a_sh
    return a, u


def _scan_groups(a_ref, u_ref, lb, n):
    acc_a = acc_u = None
    for s in range(SUBLANES):
        rows = pl.ds(s, n, stride=SUBLANES)
        a_s = a_ref[lb, rows, :]
        u_s = u_ref[lb, rows, :]
        if s == 0:
            acc_a, acc_u = a_s, u_s
        else:
            acc_u = a_s * acc_u + u_s
            acc_a = a_s * acc_a
            a_ref[lb, rows, :] = acc_a
            u_ref[lb, rows, :] = acc_u
    return acc_a, acc_u


def _apply_prefix(a_ref, u_ref, lb, n, prefix):
    for s in range(SUBLANES):
        rows = pl.ds(s, n, stride=SUBLANES)
        u_ref[lb, rows, :] = u_ref[lb, rows, :] + a_ref[lb, rows, :] * prefix


def _row_in_group(rows, cols):
    return lax.broadcasted_iota(jnp.int32, (rows, cols), 0) % SUBLANES


def _shift_rows_down(x, first_row):
    row = lax.broadcasted_iota(jnp.int32, x.shape, 0)
    return jnp.where(row == 0, first_row, pltpu.roll(x, 1, axis=0))


def _group_rms(y, gmat_ref, gain):
    y2 = (y * y).astype(BF16)
    ms = jnp.concatenate(
        [jnp.dot(y2[:, :MXU_DIM], gmat_ref[...], preferred_element_type=F32),
         jnp.dot(y2[:, MXU_DIM:], gmat_ref[...], preferred_element_type=F32)], axis=-1)
    return y * lax.rsqrt(ms + EPS) * gain


def _mixer_kernel(x_ref, mod_ref, g_ref, win_ref, lcw_ref, lcb_ref, wg_ref, bg_ref, lam_ref,
                  scw_ref, gnl_ref, gns_ref, gmat_ref, wout_ref, o_ref,
                  xl_buf, v_buf, a0_ref, u0_ref, a1_ref, u1_ref, carry_ref):
    ts = TS
    n1 = ts // SUBLANES
    n2 = n1 // SUBLANES

    @pl.when(pl.program_id(1) == 0)
    def _():
        xl_buf[0:SUBLANES, :] = jnp.zeros((SUBLANES, D_LRU), F32)
        v_buf[0:SUBLANES, :] = jnp.zeros((SUBLANES, D_SC), F32)
        carry_ref[...] = jnp.zeros(carry_ref.shape, F32)

    x = x_ref[0]
    shift, scale, gate = _split_mod(mod_ref[0])
    h = _mod_norm(x, g_ref[0], scale, shift).astype(BF16)
    def in_proj(lo, hi):
        return jnp.dot(h, win_ref[0, :, lo:hi], preferred_element_type=F32)

    x_lru = in_proj(0, D_LRU)
    xl_buf[SUBLANES:SUBLANES + ts, :] = x_lru
    xc = lcb_ref[0]
    for k in range(LRU_CONV):
        off = SUBLANES - (LRU_CONV - 1) + k
        xc = xc + lcw_ref[0, k:k + 1, :] * xl_buf[pl.ds(off, ts), :]
    xl_buf[0:SUBLANES, :] = xl_buf[ts:ts + SUBLANES, :]

    cx_sc = in_proj(2 * D_LRU + D_SC, D_IN)
    xcb = xc.astype(BF16)
    g0 = jnp.dot(xcb[:, :MXU_DIM], wg_ref[0, 0], preferred_element_type=F32)
    g1 = jnp.dot(xcb[:, MXU_DIM:], wg_ref[0, 1], preferred_element_type=F32)
    v_buf[SUBLANES:SUBLANES + ts, :] = cx_sc[:, :D_SC] * cx_sc[:, D_SC:]
    cv = jnp.zeros((ts, D_SC), F32)
    for k in range(SC_CONV):
        off = SUBLANES - (SC_CONV - 1) + k
        cv = cv + scw_ref[0, k:k + 1, :] * v_buf[pl.ds(off, ts), :]
    v_buf[0:SUBLANES, :] = v_buf[ts:ts + SUBLANES, :]

    b_sc = in_proj(2 * D_LRU, 2 * D_LRU + D_SC)
    r = jax.nn.sigmoid(jnp.concatenate([g0[:, :MXU_DIM], g1[:, :MXU_DIM]], axis=-1)
                       + bg_ref[0, 0:1, :])
    i = jax.nn.sigmoid(jnp.concatenate([g0[:, MXU_DIM:], g1[:, MXU_DIM:]], axis=-1)
                       + bg_ref[0, 1:2, :])
    log_a = r * ((-LRU_C) * jax.nn.softplus(-lam_ref[0]))
    a = jnp.exp(log_a)
    uin = jnp.sqrt(-jnp.tanh(log_a) * (1.0 + a * a)) * (i * xc)
    yn_sc = _group_rms(b_sc * cv, gmat_ref, gns_ref[0]).astype(BF16)
    out_sc = jnp.dot(yn_sc, wout_ref[0, D_LRU:, :], preferred_element_type=F32)

    gate_lru = jax.nn.gelu(in_proj(D_LRU, 2 * D_LRU))

    last = SUBLANES - 1
    hs_blocks = []
    for lb in range(D_LRU // LANES):
        lanes = slice(lb * LANES, (lb + 1) * LANES)
        h_prev = carry_ref[lb, last:last + 1, :]
        a0_ref[lb] = a[:, lanes]
        u0_ref[lb] = uin[:, lanes]
        a1, u1 = _scan_groups(a0_ref, u0_ref, lb, n1)
        a1_ref[lb] = a1
        u1_ref[lb] = u1
        a2, u2 = _scan_groups(a1_ref, u1_ref, lb, n2)
        a2, u2 = _scan8(a2, u2, _row_in_group(n2, LANES))
        h2 = u2 + a2 * h_prev
        carry_ref[lb] = h2
        _apply_prefix(a1_ref, u1_ref, lb, n2, _shift_rows_down(h2, h_prev))
        h1 = u1_ref[lb]
        _apply_prefix(a0_ref, u0_ref, lb, n1, _shift_rows_down(h1, h_prev))
        hs_blocks.append(u0_ref[lb])
    hs = jnp.concatenate(hs_blocks, axis=-1)

    yn_lru = _group_rms(gate_lru * hs, gmat_ref, gnl_ref[0]).astype(BF16)
    out = out_sc + jnp.dot(yn_lru, wout_ref[0, :D_LRU, :], preferred_element_type=F32)
    o_ref[0] = x + gate * out


def _mixer(x, mods, layer, g, w_in, lcw, lcb, wg, bg, lam, scw, gnl, gns, gmat, w_out):
    batch, seq, _ = x.shape
    ts = TS
    nlb = D_LRU // LANES
    per_layer = lambda shape: _layer_spec(shape, layer)
    return pl.pallas_call(
        _mixer_kernel,
        grid=(batch, seq // ts),
        in_specs=[
            pl.BlockSpec((1, ts, D_MODEL), lambda b, t: (b, t, 0)),
            pl.BlockSpec((1, 1, 3 * D_MODEL), lambda b, t: (layer * PAD_B + b, 0, 0)),
            per_layer((1, D_MODEL)),
            per_layer((D_MODEL, D_IN)),
            per_layer((LRU_CONV, D_LRU)),
            per_layer((1, D_LRU)),
            per_layer((2, MXU_DIM, 2 * MXU_DIM)),
            per_layer((2, D_LRU)),
            per_layer((1, D_LRU)),
            per_layer((SC_CONV, D_SC)),
            per_layer((1, D_LRU)),
            per_layer((1, D_SC)),
            pl.BlockSpec((MXU_DIM, MXU_DIM), lambda b, t: (0, 0)),
            per_layer((D_LRU + D_SC, D_MODEL)),
        ],
        out_specs=pl.BlockSpec((1, ts, D_MODEL), lambda b, t: (b, t, 0)),
        out_shape=jax.ShapeDtypeStruct(x.shape, F32),
        scratch_shapes=[
            pltpu.VMEM((ts + SUBLANES, D_LRU), F32),
            pltpu.VMEM((ts + SUBLANES, D_SC), F32),
            pltpu.VMEM((nlb, ts, LANES), F32),
            pltpu.VMEM((nlb, ts, LANES), F32),
            pltpu.VMEM((nlb, ts // SUBLANES, LANES), F32),
            pltpu.VMEM((nlb, ts // SUBLANES, LANES), F32),
            pltpu.VMEM((nlb, SUBLANES, LANES), F32),
        ],
        compiler_params=pltpu.CompilerParams(
            dimension_semantics=("arbitrary", "arbitrary"), vmem_limit_bytes=VMEM_LIMIT),
        name="mixer",
    )(x, mods, g, w_in, lcw, lcb, wg, bg, lam, scw, gnl, gns, gmat, w_out)


def _swiglu(h, w1, w3, w2):
    a = jnp.dot(h, w1, preferred_element_type=F32)
    b = jnp.dot(h, w3, preferred_element_type=F32)
    g = (a * jax.nn.sigmoid(a) * b).astype(BF16)
    return jnp.dot(g, w2, preferred_element_type=F32)


def _dense_ffn_kernel(x_ref, mod_ref, g_ref, w1_ref, w3_ref, w2_ref, o_ref, *, n_chunks):
    x = x_ref[0]
    shift, scale, gate = _split_mod(mod_ref[0])
    h = _mod_norm(x, g_ref[0], scale, shift).astype(BF16)
    fc = w1_ref.shape[2] // n_chunks
    f = jnp.zeros(x.shape, F32)
    for c in range(n_chunks):
        cols = slice(c * fc, (c + 1) * fc)
        f = f + _swiglu(h, w1_ref[0, :, cols], w3_ref[0, :, cols], w2_ref[0, cols, :])
    o_ref[0] = x + gate * f


def _dense_ffn(x, mods, layer, g, w1, w3, w2, tm=512, n_chunks=2):
    batch, seq, _ = x.shape
    d_ff = w1.shape[2]
    j = layer // 2
    return pl.pallas_call(
        functools.partial(_dense_ffn_kernel, n_chunks=n_chunks),
        grid=(batch, seq // tm),
        in_specs=[
            pl.BlockSpec((1, tm, D_MODEL), lambda b, t: (b, t, 0)),
            pl.BlockSpec((1, 1, 3 * D_MODEL), lambda b, t: (layer * PAD_B + b, 0, 0)),
            _layer_spec((1, D_MODEL), layer),
            _layer_spec((D_MODEL, d_ff), j),
            _layer_spec((D_MODEL, d_ff), j),
            _layer_spec((d_ff, D_MODEL), j),
        ],
        out_specs=pl.BlockSpec((1, tm, D_MODEL), lambda b, t: (b, t, 0)),
        out_shape=jax.ShapeDtypeStruct(x.shape, F32),
        compiler_params=pltpu.CompilerParams(
            dimension_semantics=("arbitrary", "arbitrary"), vmem_limit_bytes=VMEM_LIMIT),
        name="dense_ffn",
    )(x, mods, g, w1, w3, w2)


ROUTE_TILE = 1024
CNT_CHUNK = 128
COMBINE_CHUNKS = 4
SLOT_TILE = 512
GATHER_SLOTS = 128
GATHER_CHUNKS = 6
ROW_ALIGN = 16
WINDOW = CNT_CHUNK + ROW_ALIGN


def _top2(logits):
    lane = lax.broadcasted_iota(jnp.int32, logits.shape, 1)
    m1 = jnp.max(logits, axis=-1, keepdims=True)
    i1 = jnp.min(jnp.where(logits == m1, lane, LANES), axis=-1, keepdims=True)
    rest = jnp.where(lane == i1, -jnp.inf, logits)
    m2 = jnp.max(rest, axis=-1, keepdims=True)
    i2 = jnp.min(jnp.where(rest == m2, lane, LANES), axis=-1, keepdims=True)
    e2 = jnp.exp(m2 - m1)
    w_first = 1.0 / (1.0 + e2)
    w_second = e2 / (1.0 + e2)
    first = lane == i1
    second = lane == i2
    weights = jnp.where(first, w_first, 0.0) + jnp.where(second, w_second, 0.0)
    return weights, first | second


def _split_bf16(v):
    hi = v.astype(BF16)
    return hi, (v - hi.astype(F32)).astype(BF16)


def _router_kernel(x_ref, mod_ref, g_ref, wr_ref, br_ref,
                   h_ref, rank_ref, w_ref, rank_row_ref, cnt_ref, carry_ref):
    n = ROUTE_TILE
    n_chunks = n // CNT_CHUNK

    @pl.when(pl.program_id(0) == 0)
    def _():
        carry_ref[...] = jnp.zeros(carry_ref.shape, F32)

    shift, scale, _ = _split_mod(mod_ref[0])
    h = _mod_norm(x_ref[...], g_ref[0], scale, shift)
    h_hi, h_lo = _split_bf16(h)
    w_hi, w_lo = _split_bf16(wr_ref[0])
    both = jnp.dot(h_hi, jnp.concatenate([w_hi, w_lo], axis=1), preferred_element_type=F32)
    logits = (both[:, :LANES] + both[:, LANES:]
              + jnp.dot(h_lo, w_hi, preferred_element_type=F32)) + br_ref[0]
    weights, mask = _top2(logits)
    m = jnp.where(mask, 1.0, 0.0).astype(BF16)
    before = carry_ref[0:1, :]
    chunk = lax.broadcasted_iota(jnp.int32, (n_chunks, n), 0)
    tok = lax.broadcasted_iota(jnp.int32, (n_chunks, n), 1)
    in_earlier_chunk = jnp.where(tok < chunk * CNT_CHUNK, 1.0, 0.0).astype(BF16)
    upto_chunk_end = jnp.where(tok < (chunk + 1) * CNT_CHUNK, 1.0, 0.0).astype(BF16)
    chunk_start = jnp.dot(in_earlier_chunk, m, preferred_element_type=F32) + before
    cnt = jnp.dot(upto_chunk_end, m, preferred_element_type=F32) + before
    row = lax.broadcasted_iota(jnp.int32, (CNT_CHUNK, CNT_CHUNK), 0)
    col = lax.broadcasted_iota(jnp.int32, (CNT_CHUNK, CNT_CHUNK), 1)
    strictly_lower = jnp.where(row > col, 1.0, 0.0).astype(BF16)
    in_chunk = jnp.concatenate(
        [jnp.dot(strictly_lower, m[j * CNT_CHUNK:(j + 1) * CNT_CHUNK, :],
                 preferred_element_type=F32) + chunk_start[j:j + 1, :]
         for j in range(n_chunks)], axis=0)
    rank = jnp.where(mask, in_chunk, -1.0)
    carry_ref[...] = jnp.broadcast_to(cnt[n_chunks - 1:, :], carry_ref.shape)
    h_ref[...] = h_hi
    rank_ref[...] = rank
    w_ref[...] = weights
    rank_row_ref[...] = rank.T[:N_EXPERTS, :]
    cnt_ref[...] = cnt.astype(jnp.int32)


def _dispatch_kernel(tile_e_ref, ls0_ref, c_lo_ref, n_g_ref, h_ref, rank_ref, o_ref):
    i = pl.program_id(0)
    e = tile_e_ref[i]
    n_chunks = rank_ref.shape[1]
    span = GATHER_CHUNKS * CNT_CHUNK
    subs = SLOT_TILE // GATHER_SLOTS

    def gather(sub, k):
        q = i * subs + sub
        slot = (ls0_ref[q]
                + lax.broadcasted_iota(jnp.int32, (GATHER_SLOTS, CNT_CHUNK), 0)).astype(F32)
        wanted = c_lo_ref[q] + k * GATHER_CHUNKS
        start = jnp.minimum(wanted, n_chunks - GATHER_CHUNKS)
        p = jnp.concatenate(
            [jnp.where((start + j >= wanted) & (rank_ref[e, pl.ds(start + j, 1), :] == slot),
                       1.0, 0.0) for j in range(GATHER_CHUNKS)], axis=1).astype(BF16)
        tokens = pl.ds(pl.multiple_of(start * CNT_CHUNK, CNT_CHUNK), span)
        return jnp.dot(p, h_ref[tokens, :], preferred_element_type=F32).astype(BF16)

    for sub in range(subs):
        o_ref[sub * GATHER_SLOTS:(sub + 1) * GATHER_SLOTS, :] = gather(sub, 0)
    for sub in range(subs):
        def more(k, carry, sub=sub):
            o_ref[sub * GATHER_SLOTS:(sub + 1) * GATHER_SLOTS, :] += gather(sub, k)
            return carry

        lax.fori_loop(1, n_g_ref[i * subs + sub], more, 0)


def _expert_kernel(tile_e_ref, n_live_ref, x_ref, w1_ref, w3_ref, w2_ref, o_ref,
                   w1b_ref, w3b_ref, w2b_ref):
    i = pl.program_id(0)
    live = i < n_live_ref[0]

    @pl.when((i == 0) | (tile_e_ref[i] != tile_e_ref[jnp.maximum(i - 1, 0)]))
    def _():
        w1b_ref[...] = w1_ref[0, 0].astype(BF16)
        w3b_ref[...] = w3_ref[0, 0].astype(BF16)
        w2b_ref[...] = w2_ref[0, 0].astype(BF16)

    @pl.when(live)
    def _():
        x = x_ref[...]
        d_ff = w1b_ref.shape[1]
        split = pl.cdiv(d_ff // 2, MXU_DIM) * MXU_DIM
        f = jnp.zeros((x.shape[0], D_MODEL), F32)
        for cols in (slice(0, split), slice(split, d_ff)):
            f = f + _swiglu(x, w1b_ref[:, cols], w3b_ref[:, cols], w2b_ref[cols, :])
        o_ref[...] = f.astype(BF16)

    @pl.when(jnp.logical_not(live))
    def _():
        o_ref[...] = jnp.zeros(o_ref.shape, BF16)


def _combine_kernel(wstart_ref, base_ref, x_ref, mod_ref, rank_ref, w_ref, fg_ref, *rest, final):
    n_win = COMBINE_CHUNKS * N_EXPERTS
    y_refs, o_ref = rest[:n_win], rest[n_win]
    i = pl.program_id(0)
    lane = lax.broadcasted_iota(jnp.int32, (CNT_CHUNK, WINDOW), 1)
    gate = mod_ref[0][:, 2 * D_MODEL:]
    for s in range(COMBINE_CHUNKS):
        rows = slice(s * CNT_CHUNK, (s + 1) * CNT_CHUNK)
        rank = rank_ref[rows, :]
        wts = w_ref[rows, :]
        acc = jnp.zeros((CNT_CHUNK, D_MODEL), F32)
        for e in range(N_EXPERTS):
            w = s * N_EXPERTS + e
            rk = rank[:, e:e + 1]
            slot = rk + base_ref[e].astype(F32)
            window_slot = (wstart_ref[i * n_win + w] + lane).astype(F32)
            hit = (rk >= 0.0) & (slot == window_slot)
            p = jnp.where(hit, wts[:, e:e + 1], 0.0).astype(BF16)
            acc = acc + jnp.dot(p, y_refs[w][...], preferred_element_type=F32)
        y = x_ref[rows, :] + gate * acc
        if final:
            ms = jnp.mean(y * y, axis=-1, keepdims=True)
            y = y * lax.rsqrt(ms + EPS) * fg_ref[...]
        o_ref[rows, :] = y


def _route_metadata(cnt_end, n_tok):
    i32 = jnp.int32
    cnt_end = cnt_end[:, :N_EXPERTS]
    cnt_start = jnp.concatenate([jnp.zeros((1, N_EXPERTS), i32), cnt_end[:-1]], axis=0)
    n_chunks = cnt_end.shape[0]
    total = cnt_end[-1]
    n_tiles = (total + SLOT_TILE - 1) // SLOT_TILE
    tile_end = jnp.cumsum(n_tiles)
    tile_start = tile_end - n_tiles
    base = (tile_start * SLOT_TILE).astype(i32)
    max_tiles = 2 * n_tok // SLOT_TILE + N_EXPERTS
    i = jnp.arange(max_tiles, dtype=i32)
    tile_e = jnp.minimum(jnp.sum(i[:, None] >= tile_end[None, :], axis=1), N_EXPERTS - 1)
    subs = SLOT_TILE // GATHER_SLOTS
    q = jnp.arange(max_tiles * subs, dtype=i32)
    q_e = tile_e[q // subs]
    ls0 = (q // subs - tile_start[q_e]) * SLOT_TILE + (q % subs) * GATHER_SLOTS
    live = (q // subs < tile_end[-1]) & (ls0 < total[q_e])
    ends = cnt_end[:, q_e].T
    starts = cnt_start[:, q_e].T
    c_lo = jnp.sum(ends <= ls0[:, None], axis=1)
    c_hi = jnp.sum(starts < (ls0 + GATHER_SLOTS)[:, None], axis=1) - 1
    n_g = jnp.where(live, (c_hi - c_lo + GATHER_CHUNKS) // GATHER_CHUNKS, 0)
    c_lo = jnp.minimum(c_lo, n_chunks - 1)
    n_slots = max_tiles * SLOT_TILE
    wstart = jnp.minimum((base[None, :] + cnt_start) // ROW_ALIGN * ROW_ALIGN, n_slots - WINDOW)
    as_i32 = lambda a: a.astype(i32)
    return (as_i32(tile_e), as_i32(tile_end[-1:]), as_i32(ls0), as_i32(c_lo), as_i32(n_g), base,
            as_i32(wstart).reshape(-1))


def _moe_ffn(x, mods, layer, g, wr, br, w1, w3, w2, final_gain, final):
    batch, seq, _ = x.shape
    n_tok = batch * seq
    d_ff = w1.shape[3]
    j = layer // 2
    xt = x.reshape(n_tok, D_MODEL)
    params = lambda: pltpu.CompilerParams(
        dimension_semantics=("arbitrary",), vmem_limit_bytes=VMEM_LIMIT)

    route_tiles_per_seq = seq // ROUTE_TILE
    h, rank, wts, rank_row, cnt_end = pl.pallas_call(
        _router_kernel,
        grid=(n_tok // ROUTE_TILE,),
        in_specs=[
            pl.BlockSpec((ROUTE_TILE, D_MODEL), lambda i: (i, 0)),
            pl.BlockSpec((1, 1, 3 * D_MODEL),
                         lambda i: (layer * PAD_B + i // route_tiles_per_seq, 0, 0)),
            _layer_spec((1, D_MODEL), layer),
            _layer_spec((D_MODEL, LANES), j),
            _layer_spec((1, LANES), j),
        ],
        out_specs=[
            pl.BlockSpec((ROUTE_TILE, D_MODEL), lambda i: (i, 0)),
            pl.BlockSpec((ROUTE_TILE, LANES), lambda i: (i, 0)),
            pl.BlockSpec((ROUTE_TILE, LANES), lambda i: (i, 0)),
            pl.BlockSpec((N_EXPERTS, ROUTE_TILE), lambda i: (0, i)),
            pl.BlockSpec((ROUTE_TILE // CNT_CHUNK, LANES), lambda i: (i, 0)),
        ],
        out_shape=[
            jax.ShapeDtypeStruct((n_tok, D_MODEL), BF16),
            jax.ShapeDtypeStruct((n_tok, LANES), F32),
            jax.ShapeDtypeStruct((n_tok, LANES), F32),
            jax.ShapeDtypeStruct((N_EXPERTS, n_tok), F32),
            jax.ShapeDtypeStruct((n_tok // CNT_CHUNK, LANES), jnp.int32),
        ],
        scratch_shapes=[pltpu.VMEM((SUBLANES, LANES), F32)],
        compiler_params=params(),
        name="moe_router",
    )(xt, mods, g, wr, br)

    tile_e, n_live, ls0, c_lo, n_g, base, wstart = _route_metadata(cnt_end, n_tok)
    max_tiles = tile_e.shape[0]
    n_slots = max_tiles * SLOT_TILE

    x_sorted = pl.pallas_call(
        _dispatch_kernel,
        grid_spec=pltpu.PrefetchScalarGridSpec(
            num_scalar_prefetch=4,
            grid=(max_tiles,),
            in_specs=[pl.BlockSpec(memory_space=pltpu.VMEM),
                      pl.BlockSpec(memory_space=pltpu.VMEM)],
            out_specs=pl.BlockSpec((SLOT_TILE, D_MODEL), lambda i, *_: (i, 0)),
        ),
        out_shape=jax.ShapeDtypeStruct((n_slots, D_MODEL), BF16),
        compiler_params=params(),
        name="moe_dispatch",
    )(tile_e, ls0, c_lo, n_g, h, rank_row.reshape(N_EXPERTS, n_tok // CNT_CHUNK, CNT_CHUNK))

    y_sorted = pl.pallas_call(
        _expert_kernel,
        grid_spec=pltpu.PrefetchScalarGridSpec(
            num_scalar_prefetch=2,
            grid=(max_tiles,),
            in_specs=[
                pl.BlockSpec((SLOT_TILE, D_MODEL), lambda i, te, nl: (i, 0)),
                pl.BlockSpec((1, 1, D_MODEL, d_ff), lambda i, te, nl: (j, te[i], 0, 0)),
                pl.BlockSpec((1, 1, D_MODEL, d_ff), lambda i, te, nl: (j, te[i], 0, 0)),
                pl.BlockSpec((1, 1, d_ff, D_MODEL), lambda i, te, nl: (j, te[i], 0, 0)),
            ],
            out_specs=pl.BlockSpec((SLOT_TILE, D_MODEL), lambda i, te, nl: (i, 0)),
            scratch_shapes=[pltpu.VMEM((D_MODEL, d_ff), BF16),
                            pltpu.VMEM((D_MODEL, d_ff), BF16),
                            pltpu.VMEM((d_ff, D_MODEL), BF16)],
        ),
        out_shape=jax.ShapeDtypeStruct((n_slots, D_MODEL), BF16),
        compiler_params=params(),
        name="moe_experts",
    )(tile_e, n_live, x_sorted, w1, w3, w2)

    rows = COMBINE_CHUNKS * CNT_CHUNK
    steps_per_seq = seq // rows
    n_win = COMBINE_CHUNKS * N_EXPERTS
    window_spec = lambda w: pl.BlockSpec(
        (pl.Element(WINDOW), pl.Element(D_MODEL)),
        lambda i, ws, bs: (pl.multiple_of(ws[i * n_win + w], ROW_ALIGN), 0))
    out = pl.pallas_call(
        functools.partial(_combine_kernel, final=final),
        grid_spec=pltpu.PrefetchScalarGridSpec(
            num_scalar_prefetch=2,
            grid=(n_tok // rows,),
            in_specs=[
                pl.BlockSpec((rows, D_MODEL), lambda i, ws, bs: (i, 0)),
                pl.BlockSpec((1, 1, 3 * D_MODEL),
                             lambda i, ws, bs: (layer * PAD_B + i // steps_per_seq, 0, 0)),
                pl.BlockSpec((rows, LANES), lambda i, ws, bs: (i, 0)),
                pl.BlockSpec((rows, LANES), lambda i, ws, bs: (i, 0)),
                pl.BlockSpec((1, D_MODEL), lambda i, ws, bs: (0, 0)),
            ] + [window_spec(w) for w in range(n_win)],
            out_specs=pl.BlockSpec((rows, D_MODEL), lambda i, ws, bs: (i, 0)),
        ),
        out_shape=jax.ShapeDtypeStruct((n_tok, D_MODEL), F32),
        compiler_params=params(),
        name="moe_combine",
    )(wstart, base, xt, mods, rank, wts, final_gain, *([y_sorted] * n_win))
    return out.reshape(x.shape)


def _block_diag_halves(w):
    depth, heads, hd, _ = w.shape
    per_tile = MXU_DIM // hd
    w = w.reshape(depth, heads // per_tile, per_tile, hd, hd)
    eye = jnp.eye(per_tile, dtype=w.dtype)
    bd = jnp.einsum("dtiab,ij->dtiajb", w, eye)
    return bd.reshape(depth, heads // per_tile, MXU_DIM, MXU_DIM)


def kernel(x, c, mix_norm, mix_mod_w, mix_mod_b, w_in, lru_conv_w, lru_conv_b, lru_wa, lru_ba,
           lru_wi, lru_bi, lru_lambda, sc_conv_w, gn_lru, gn_sc, w_out, ffn_norm, ffn_mod_w,
           ffn_mod_b, dense_w1, dense_w3, dense_w2, router_w, router_b, exp_w1, exp_w3, exp_w2,
           final_norm):
    depth = w_in.shape[0]
    batch = x.shape[0]
    c_pad = jnp.pad(c, ((0, PAD_B - batch), (0, 0)))
    mix_mods = _modulation(c_pad, mix_mod_w, mix_mod_b).reshape(depth * PAD_B, 1, 3 * D_MODEL)
    ffn_mods = _modulation(c_pad, ffn_mod_w, ffn_mod_b).reshape(depth * PAD_B, 1, 3 * D_MODEL)

    row = lambda p: p.reshape(depth, 1, p.shape[-1])
    wg = jnp.concatenate([_block_diag_halves(lru_wa), _block_diag_halves(lru_wi)],
                         axis=-1).astype(BF16)
    bg = jnp.stack([lru_ba.reshape(depth, D_LRU), lru_bi.reshape(depth, D_LRU)], axis=1)
    head = jnp.arange(MXU_DIM) // HEAD_DIM
    gmat = jnp.where(head[:, None] == head[None, :], 1.0 / HEAD_DIM, 0.0).astype(BF16)
    w_in_b = w_in.astype(BF16)
    w_out_b = w_out.astype(BF16)
    dense_w1_b, dense_w3_b, dense_w2_b = (w.astype(BF16) for w in (dense_w1, dense_w3, dense_w2))
    n_moe = router_w.shape[0]
    wr = jnp.pad(router_w, ((0, 0), (0, 0), (0, LANES - N_EXPERTS)))
    br = jnp.pad(router_b, ((0, 0), (0, LANES - N_EXPERTS)),
                 constant_values=-jnp.inf).reshape(n_moe, 1, LANES)
    final_gain = final_norm.reshape(1, D_MODEL)
    mix_g, ffn_g = row(mix_norm), row(ffn_norm)
    lcb, lam, gnl, gns = row(lru_conv_b), row(lru_lambda), row(gn_lru), row(gn_sc)

    for l in range(depth):
        x = _mixer(x, mix_mods, l, mix_g, w_in_b, lru_conv_w, lcb, wg, bg, lam, sc_conv_w, gnl,
                   gns, gmat, w_out_b)
        if l % 2 == 0:
            x = _dense_ffn(x, ffn_mods, l, ffn_g, dense_w1_b, dense_w3_b, dense_w2_b)
        else:
            x = _moe_ffn(x, ffn_mods, l, ffn_g, wr, br, exp_w1, exp_w3, exp_w2, final_gain,
                         final=(l == depth - 1))
    if depth % 2 == 1:
        raise NotImplementedError("final norm is fused into the last MoE layer")
    return x
```

```python
import functools

import jax
import jax.numpy as jnp
from jax import lax
from jax.experimental import pallas as pl
from jax.experimental.pallas import tpu as pltpu

F32 = jnp.float32
BF16 = jnp.bfloat16

D_MODEL = 1024
D_LRU = 512
D_SC = 512
D_IN = 2 * D_LRU + 3 * D_SC
HEAD_DIM = 64
LRU_CONV = 4
SC_CONV = 3
LRU_C = 8.0
N_EXPERTS = 8
EPS = 1e-6

SUBLANES = 8
LANES = 128
MXU_DIM = 256
TS = SUBLANES ** 3
PAD_B = 8
VMEM_LIMIT = 56 * 1024 * 1024


def _mod_norm(x, g, scale, shift):
    ms = jnp.mean(x * x, axis=-1, keepdims=True)
    return x * lax.rsqrt(ms + EPS) * (g * (1.0 + scale)) + shift


def _split_mod(mod):
    return mod[:, :D_MODEL], mod[:, D_MODEL:2 * D_MODEL], mod[:, 2 * D_MODEL:]


def _layer_spec(shape, layer):
    zeros = (0,) * len(shape)
    return pl.BlockSpec((1, *shape), lambda *_: (layer, *zeros))


def _mod_kernel(c_ref, w_ref, b_ref, o_ref):
    c = c_ref[...]
    s = (c * jax.nn.sigmoid(c)).astype(BF16)
    o_ref[0] = jnp.dot(s, w_ref[0].astype(BF16), preferred_element_type=F32) + b_ref[0]


def _modulation(c_pad, w, b):
    depth = w.shape[0]
    n_col = 3 * D_MODEL // D_MODEL
    return pl.pallas_call(
        _mod_kernel,
        grid=(depth, n_col),
        in_specs=[
            pl.BlockSpec((PAD_B, D_MODEL), lambda l, j: (0, 0)),
            pl.BlockSpec((1, D_MODEL, D_MODEL), lambda l, j: (l, 0, j)),
            pl.BlockSpec((1, 1, D_MODEL), lambda l, j: (l, 0, j)),
        ],
        out_specs=pl.BlockSpec((1, PAD_B, D_MODEL), lambda l, j: (l, 0, j)),
        out_shape=jax.ShapeDtypeStruct((depth, PAD_B, 3 * D_MODEL), F32),
        compiler_params=pltpu.CompilerParams(
            dimension_semantics=("arbitrary", "arbitrary"), vmem_limit_bytes=VMEM_LIMIT),
        name="adaln_mod",
    )(c_pad, w, b.reshape(depth, 1, 3 * D_MODEL))


def _scan8(a, u, row_in_group):
    for d in (1, 2, 4):
        keep = row_in_group >= d
        a_sh = jnp.where(keep, pltpu.roll(a, d, axis=0), 1.0)
        u_sh = jnp.where(keep, pltpu.roll(u, d, axis=0), 0.0)
        u = a * u_sh + u
        a = a * a_sh
    return a, u


def _scan_groups(a_ref, u_ref, lb, n):
    acc_a = acc_u = None
    for s in range(SUBLANES):
        rows = pl.ds(s, n, stride=SUBLANES)
        a_s = a_ref[lb, rows, :]
        u_s = u_ref[lb, rows, :]
        if s == 0:
            acc_a, acc_u = a_s, u_s
        else:
            acc_u = a_s * acc_u + u_s
            acc_a = a_s * acc_a
            a_ref[lb, rows, :] = acc_a
            u_ref[lb, rows, :] = acc_u
    return acc_a, acc_u


def _apply_prefix(a_ref, u_ref, lb, n, prefix):
    for s in range(SUBLANES):
        rows = pl.ds(s, n, stride=SUBLANES)
        u_ref[lb, rows, :] = u_ref[lb, rows, :] + a_ref[lb, rows, :] * prefix


def _row_in_group(rows, cols):
    return lax.broadcasted_iota(jnp.int32, (rows, cols), 0) % SUBLANES


def _shift_rows_down(x, first_row):
    row = lax.broadcasted_iota(jnp.int32, x.shape, 0)
    return jnp.where(row == 0, first_row, pltpu.roll(x, 1, axis=0))


def _group_rms(y, gmat_ref, gain):
    y2 = (y * y).astype(BF16)
    ms = jnp.concatenate(
        [jnp.dot(y2[:, :MXU_DIM], gmat_ref[...], preferred_element_type=F32),
         jnp.dot(y2[:, MXU_DIM:], gmat_ref[...], preferred_element_type=F32)], axis=-1)
    return y * lax.rsqrt(ms + EPS) * gain


def _mixer_kernel(x_ref, mod_ref, g_ref, win_ref, lcw_ref, lcb_ref, wg_ref, bg_ref, lam_ref,
                  scw_ref, gnl_ref, gns_ref, gmat_ref, wout_ref, o_ref,
                  xl_buf, v_buf, a0_ref, u0_ref, a1_ref, u1_ref, carry_ref):
    ts = TS
    n1 = ts // SUBLANES
    n2 = n1 // SUBLANES

    @pl.when(pl.program_id(1) == 0)
    def _():
        xl_buf[0:SUBLANES, :] = jnp.zeros((SUBLANES, D_LRU), F32)
        v_buf[0:SUBLANES, :] = jnp.zeros((SUBLANES, D_SC), F32)
        carry_ref[...] = jnp.zeros(carry_ref.shape, F32)

    x = x_ref[0]
    shift, scale, gate = _split_mod(mod_ref[0])
    h = _mod_norm(x, g_ref[0], scale, shift).astype(BF16)
    def in_proj(lo, hi):
        return jnp.dot(h, win_ref[0, :, lo:hi], preferred_element_type=F32)

    x_lru = in_proj(0, D_LRU)
    xl_buf[SUBLANES:SUBLANES + ts, :] = x_lru
    xc = lcb_ref[0]
    for k in range(LRU_CONV):
        off = SUBLANES - (LRU_CONV - 1) + k
        xc = xc + lcw_ref[0, k:k + 1, :] * xl_buf[pl.ds(off, ts), :]
    xl_buf[0:SUBLANES, :] = xl_buf[ts:ts + SUBLANES, :]

    cx_sc = in_proj(2 * D_LRU + D_SC, D_IN)
    xcb = xc.astype(BF16)
    g0 = jnp.dot(xcb[:, :MXU_DIM], wg_ref[0, 0], preferred_element_type=F32)
    g1 = jnp.dot(xcb[:, MXU_DIM:], wg_ref[0, 1], preferred_element_type=F32)
    v_buf[SUBLANES:SUBLANES + ts, :] = cx_sc[:, :D_SC] * cx_sc[:, D_SC:]
    cv = jnp.zeros((ts, D_SC), F32)
    for k in range(SC_CONV):
        off = SUBLANES - (SC_CONV - 1) + k
        cv = cv + scw_ref[0, k:k + 1, :] * v_buf[pl.ds(off, ts), :]
    v_buf[0:SUBLANES, :] = v_buf[ts:ts + SUBLANES, :]

    b_sc = in_proj(2 * D_LRU, 2 * D_LRU + D_SC)
    r = jax.nn.sigmoid(jnp.concatenate([g0[:, :MXU_DIM], g1[:, :MXU_DIM]], axis=-1)
                       + bg_ref[0, 0:1, :])
    i = jax.nn.sigmoid(jnp.concatenate([g0[:, MXU_DIM:], g1[:, MXU_DIM:]], axis=-1)
                       + bg_ref[0, 1:2, :])
    log_a = r * ((-LRU_C) * jax.nn.softplus(-lam_ref[0]))
    a = jnp.exp(log_a)
    uin = jnp.sqrt(-jnp.tanh(log_a) * (1.0 + a * a)) * (i * xc)
    yn_sc = _group_rms(b_sc * cv, gmat_ref, gns_ref[0]).astype(BF16)
    out_sc = jnp.dot(yn_sc, wout_ref[0, D_LRU:, :], preferred_element_type=F32)

    gate_lru = jax.nn.gelu(in_proj(D_LRU, 2 * D_LRU))

    last = SUBLANES - 1
    hs_blocks = []
    for lb in range(D_LRU // LANES):
        lanes = slice(lb * LANES, (lb + 1) * LANES)
        h_prev = carry_ref[lb, last:last + 1, :]
        a0_ref[lb] = a[:, lanes]
        u0_ref[lb] = uin[:, lanes]
        a1, u1 = _scan_groups(a0_ref, u0_ref, lb, n1)
        a1_ref[lb] = a1
        u1_ref[lb] = u1
        a2, u2 = _scan_groups(a1_ref, u1_ref, lb, n2)
        a2, u2 = _scan8(a2, u2, _row_in_group(n2, LANES))
        h2 = u2 + a2 * h_prev
        carry_ref[lb] = h2
        _apply_prefix(a1_ref, u1_ref, lb, n2, _shift_rows_down(h2, h_prev))
        h1 = u1_ref[lb]
        _apply_prefix(a0_ref, u0_ref, lb, n1, _shift_rows_down(h1, h_prev))
        hs_blocks.append(u0_ref[lb])
    hs = jnp.concatenate(hs_blocks, axis=-1)

    yn_lru = _group_rms(gate_lru * hs, gmat_ref, gnl_ref[0]).astype(BF16)
    out = out_sc + jnp.dot(yn_lru, wout_ref[0, :D_LRU, :], preferred_element_type=F32)
    o_ref[0] = x + gate * out


def _mixer(x, mods, layer, g, w_in, lcw, lcb, wg, bg, lam, scw, gnl, gns, gmat, w_out):
    batch, seq, _ = x.shape
    ts = TS
    nlb = D_LRU // LANES
    per_layer = lambda shape: _layer_spec(shape, layer)
    return pl.pallas_call(
        _mixer_kernel,
        grid=(batch, seq // ts),
        in_specs=[
            pl.BlockSpec((1, ts, D_MODEL), lambda b, t: (b, t, 0)),
            pl.BlockSpec((1, 1, 3 * D_MODEL), lambda b, t: (layer * PAD_B + b, 0, 0)),
            per_layer((1, D_MODEL)),
            per_layer((D_MODEL, D_IN)),
            per_layer((LRU_CONV, D_LRU)),
            per_layer((1, D_LRU)),
            per_layer((2, MXU_DIM, 2 * MXU_DIM)),
            per_layer((2, D_LRU)),
            per_layer((1, D_LRU)),
            per_layer((SC_CONV, D_SC)),
            per_layer((1, D_LRU)),
            per_layer((1, D_SC)),
            pl.BlockSpec((MXU_DIM, MXU_DIM), lambda b, t: (0, 0)),
            per_layer((D_LRU + D_SC, D_MODEL)),
        ],
        out_specs=pl.BlockSpec((1, ts, D_MODEL), lambda b, t: (b, t, 0)),
        out_shape=jax.ShapeDtypeStruct(x.shape, F32),
        scratch_shapes=[
            pltpu.VMEM((ts + SUBLANES, D_LRU), F32),
            pltpu.VMEM((ts + SUBLANES, D_SC), F32),
            pltpu.VMEM((nlb, ts, LANES), F32),
            pltpu.VMEM((nlb, ts, LANES), F32),
            pltpu.VMEM((nlb, ts // SUBLANES, LANES), F32),
            pltpu.VMEM((nlb, ts // SUBLANES, LANES), F32),
            pltpu.VMEM((nlb, SUBLANES, LANES), F32),
        ],
        compiler_params=pltpu.CompilerParams(
            dimension_semantics=("arbitrary", "arbitrary"), vmem_limit_bytes=VMEM_LIMIT),
        name="mixer",
    )(x, mods, g, w_in, lcw, lcb, wg, bg, lam, scw, gnl, gns, gmat, w_out)


def _swiglu(h, w1, w3, w2):
    a = jnp.dot(h, w1, preferred_element_type=F32)
    b = jnp.dot(h, w3, preferred_element_type=F32)
    g = (a * jax.nn.sigmoid(a) * b).astype(BF16)
    return jnp.dot(g, w2, preferred_element_type=F32)


def _mxu_halves(width):
    split = pl.cdiv(width // 2, MXU_DIM) * MXU_DIM
    return slice(0, split), slice(split, width)


def _dense_ffn_kernel(x_ref, mod_ref, g_ref, w1_ref, w3_ref, w2_ref, o_ref):
    x = x_ref[0]
    shift, scale, gate = _split_mod(mod_ref[0])
    h = _mod_norm(x, g_ref[0], scale, shift).astype(BF16)
    f = jnp.zeros(x.shape, F32)
    for cols in _mxu_halves(w1_ref.shape[2]):
        f = f + _swiglu(h, w1_ref[0, :, cols], w3_ref[0, :, cols], w2_ref[0, cols, :])
    o_ref[0] = x + gate * f


def _dense_ffn(x, mods, layer, g, w1, w3, w2, tm=512):
    batch, seq, _ = x.shape
    d_ff = w1.shape[2]
    j = layer // 2
    return pl.pallas_call(
        _dense_ffn_kernel,
        grid=(batch, seq // tm),
        in_specs=[
            pl.BlockSpec((1, tm, D_MODEL), lambda b, t: (b, t, 0)),
            pl.BlockSpec((1, 1, 3 * D_MODEL), lambda b, t: (layer * PAD_B + b, 0, 0)),
            _layer_spec((1, D_MODEL), layer),
            _layer_spec((D_MODEL, d_ff), j),
            _layer_spec((D_MODEL, d_ff), j),
            _layer_spec((d_ff, D_MODEL), j),
        ],
        out_specs=pl.BlockSpec((1, tm, D_MODEL), lambda b, t: (b, t, 0)),
        out_shape=jax.ShapeDtypeStruct(x.shape, F32),
        compiler_params=pltpu.CompilerParams(
            dimension_semantics=("arbitrary", "arbitrary"), vmem_limit_bytes=VMEM_LIMIT),
        name="dense_ffn",
    )(x, mods, g, w1, w3, w2)


ROUTE_TILE = 1024
CNT_CHUNK = 128
COMBINE_CHUNKS = 4
SLOT_TILE = 512
GATHER_SLOTS = 128
GATHER_CHUNKS = 6
ROW_ALIGN = 16
WINDOW = CNT_CHUNK + ROW_ALIGN


def _top2(logits):
    lane = lax.broadcasted_iota(jnp.int32, logits.shape, 1)
    m1 = jnp.max(logits, axis=-1, keepdims=True)
    i1 = jnp.min(jnp.where(logits == m1, lane, LANES), axis=-1, keepdims=True)
    rest = jnp.where(lane == i1, -jnp.inf, logits)
    m2 = jnp.max(rest, axis=-1, keepdims=True)
    i2 = jnp.min(jnp.where(rest == m2, lane, LANES), axis=-1, keepdims=True)
    e2 = jnp.exp(m2 - m1)
    w_first = 1.0 / (1.0 + e2)
    w_second = e2 / (1.0 + e2)
    first = lane == i1
    second = lane == i2
    weights = jnp.where(first, w_first, 0.0) + jnp.where(second, w_second, 0.0)
    return weights, first | second


def _split_bf16(v):
    hi = v.astype(BF16)
    return hi, (v - hi.astype(F32)).astype(BF16)


def _router_kernel(x_ref, mod_ref, g_ref, wr_ref, br_ref,
                   h_ref, rank_ref, w_ref, rank_row_ref, cnt_ref, carry_ref):
    n = ROUTE_TILE
    n_chunks = n // CNT_CHUNK

    @pl.when(pl.program_id(0) == 0)
    def _():
        carry_ref[...] = jnp.zeros(carry_ref.shape, F32)

    shift, scale, _ = _split_mod(mod_ref[0])
    h = _mod_norm(x_ref[...], g_ref[0], scale, shift)
    h_hi, h_lo = _split_bf16(h)
    w_hi, w_lo = _split_bf16(wr_ref[0])
    both = jnp.dot(h_hi, jnp.concatenate([w_hi, w_lo], axis=1), preferred_element_type=F32)
    logits = (both[:, :LANES] + both[:, LANES:]
              + jnp.dot(h_lo, w_hi, preferred_element_type=F32)) + br_ref[0]
    weights, mask = _top2(logits)
    m = jnp.where(mask, 1.0, 0.0).astype(BF16)
    before = carry_ref[0:1, :]
    chunk = lax.broadcasted_iota(jnp.int32, (n_chunks, n), 0)
    tok = lax.broadcasted_iota(jnp.int32, (n_chunks, n), 1)
    in_earlier_chunk = jnp.where(tok < chunk * CNT_CHUNK, 1.0, 0.0).astype(BF16)
    upto_chunk_end = jnp.where(tok < (chunk + 1) * CNT_CHUNK, 1.0, 0.0).astype(BF16)
    chunk_start = jnp.dot(in_earlier_chunk, m, preferred_element_type=F32) + before
    cnt = jnp.dot(upto_chunk_end, m, preferred_element_type=F32) + before
    row = lax.broadcasted_iota(jnp.int32, (CNT_CHUNK, CNT_CHUNK), 0)
    col = lax.broadcasted_iota(jnp.int32, (CNT_CHUNK, CNT_CHUNK), 1)
    strictly_lower = jnp.where(row > col, 1.0, 0.0).astype(BF16)
    in_chunk = jnp.concatenate(
        [jnp.dot(strictly_lower, m[j * CNT_CHUNK:(j + 1) * CNT_CHUNK, :],
                 preferred_element_type=F32) + chunk_start[j:j + 1, :]
         for j in range(n_chunks)], axis=0)
    rank = jnp.where(mask, in_chunk, -1.0)
    carry_ref[...] = jnp.broadcast_to(cnt[n_chunks - 1:, :], carry_ref.shape)
    h_ref[...] = h_hi
    rank_ref[...] = rank
    w_ref[...] = weights
    rank_row_ref[...] = rank.T[:N_EXPERTS, :]
    cnt_ref[...] = cnt.astype(jnp.int32)


def _dispatch_kernel(tile_e_ref, ls0_ref, c_lo_ref, n_g_ref, h_ref, rank_ref, o_ref):
    i = pl.program_id(0)
    e = tile_e_ref[i]
    n_chunks = rank_ref.shape[1]
    span = GATHER_CHUNKS * CNT_CHUNK
    subs = SLOT_TILE // GATHER_SLOTS

    def gather(sub, k):
        q = i * subs + sub
        slot = (ls0_ref[q]
                + lax.broadcasted_iota(jnp.int32, (GATHER_SLOTS, CNT_CHUNK), 0)).astype(F32)
        wanted = c_lo_ref[q] + k * GATHER_CHUNKS
        start = jnp.minimum(wanted, n_chunks - GATHER_CHUNKS)
        p = jnp.concatenate(
            [jnp.where((start + j >= wanted) & (rank_ref[e, pl.ds(start + j, 1), :] == slot),
                       1.0, 0.0) for j in range(GATHER_CHUNKS)], axis=1).astype(BF16)
        tokens = pl.ds(pl.multiple_of(start * CNT_CHUNK, CNT_CHUNK), span)
        return jnp.dot(p, h_ref[tokens, :], preferred_element_type=F32).astype(BF16)

    for sub in range(subs):
        o_ref[sub * GATHER_SLOTS:(sub + 1) * GATHER_SLOTS, :] = gather(sub, 0)
    for sub in range(subs):
        def more(k, carry, sub=sub):
            o_ref[sub * GATHER_SLOTS:(sub + 1) * GATHER_SLOTS, :] += gather(sub, k)
            return carry

        lax.fori_loop(1, n_g_ref[i * subs + sub], more, 0)


def _expert_kernel(tile_e_ref, n_live_ref, x_ref, w1_ref, w3_ref, w2_ref, o_ref,
                   w13b_ref, w2b_ref):
    i = pl.program_id(0)
    live = i < n_live_ref[0]
    halves = (slice(0, w2b_ref.shape[0]),)
    pair_at = [2 * cols.start for cols in halves]

    @pl.when((i == 0) | (tile_e_ref[i] != tile_e_ref[jnp.maximum(i - 1, 0)]))
    def _():
        for cols, at in zip(halves, pair_at):
            wd = cols.stop - cols.start
            w13b_ref[:, at:at + wd] = w1_ref[0, 0, :, cols].astype(BF16)
            w13b_ref[:, at + wd:at + 2 * wd] = w3_ref[0, 0, :, cols].astype(BF16)
        w2b_ref[...] = w2_ref[0, 0].astype(BF16)

    @pl.when(live)
    def _():
        x = x_ref[...]
        f = jnp.zeros((x.shape[0], D_MODEL), F32)
        for cols, at in zip(halves, pair_at):
            wd = cols.stop - cols.start
            ab = jnp.dot(x, w13b_ref[:, at:at + 2 * wd], preferred_element_type=F32)
            a, b = ab[:, :wd], ab[:, wd:]
            g = (a * jax.nn.sigmoid(a) * b).astype(BF16)
            f = f + jnp.dot(g, w2b_ref[cols, :], preferred_element_type=F32)
        o_ref[...] = f.astype(BF16)

    @pl.when(jnp.logical_not(live))
    def _():
        o_ref[...] = jnp.zeros(o_ref.shape, BF16)


def _combine_kernel(wstart_ref, base_ref, x_ref, mod_ref, rank_ref, w_ref, fg_ref, *rest, final):
    n_win = COMBINE_CHUNKS * N_EXPERTS
    y_refs, o_ref = rest[:n_win], rest[n_win]
    i = pl.program_id(0)
    lane = lax.broadcasted_iota(jnp.int32, (CNT_CHUNK, WINDOW), 1)
    gate = mod_ref[0][:, 2 * D_MODEL:]
    for s in range(COMBINE_CHUNKS):
        rows = slice(s * CNT_CHUNK, (s + 1) * CNT_CHUNK)
        rank = rank_ref[rows, :]
        wts = w_ref[rows, :]
        acc = jnp.zeros((CNT_CHUNK, D_MODEL), F32)
        for e in range(N_EXPERTS):
            w = s * N_EXPERTS + e
            rk = rank[:, e:e + 1]
            slot = rk + base_ref[e].astype(F32)
            window_slot = (wstart_ref[i * n_win + w] + lane).astype(F32)
            hit = (rk >= 0.0) & (slot == window_slot)
            p = jnp.where(hit, wts[:, e:e + 1], 0.0).astype(BF16)
            acc = acc + jnp.dot(p, y_refs[w][...], preferred_element_type=F32)
        y = x_ref[rows, :] + gate * acc
        if final:
            ms = jnp.mean(y * y, axis=-1, keepdims=True)
            y = y * lax.rsqrt(ms + EPS) * fg_ref[...]
        o_ref[rows, :] = y


def _route_metadata(cnt_end, n_tok):
    i32 = jnp.int32
    cnt_end = cnt_end[:, :N_EXPERTS]
    cnt_start = jnp.concatenate([jnp.zeros((1, N_EXPERTS), i32), cnt_end[:-1]], axis=0)
    n_chunks = cnt_end.shape[0]
    total = cnt_end[-1]
    n_tiles = (total + SLOT_TILE - 1) // SLOT_TILE
    tile_end = jnp.cumsum(n_tiles)
    tile_start = tile_end - n_tiles
    base = (tile_start * SLOT_TILE).astype(i32)
    max_tiles = 2 * n_tok // SLOT_TILE + N_EXPERTS
    i = jnp.arange(max_tiles, dtype=i32)
    tile_e = jnp.minimum(jnp.sum(i[:, None] >= tile_end[None, :], axis=1), N_EXPERTS - 1)
    subs = SLOT_TILE // GATHER_SLOTS
    q = jnp.arange(max_tiles * subs, dtype=i32)
    q_e = tile_e[q // subs]
    ls0 = (q // subs - tile_start[q_e]) * SLOT_TILE + (q % subs) * GATHER_SLOTS
    live = (q // subs < tile_end[-1]) & (ls0 < total[q_e])
    ends = cnt_end[:, q_e].T
    starts = cnt_start[:, q_e].T
    c_lo = jnp.sum(ends <= ls0[:, None], axis=1)
    c_hi = jnp.sum(starts < (ls0 + GATHER_SLOTS)[:, None], axis=1) - 1
    n_g = jnp.where(live, (c_hi - c_lo + GATHER_CHUNKS) // GATHER_CHUNKS, 0)
    c_lo = jnp.minimum(c_lo, n_chunks - 1)
    n_slots = max_tiles * SLOT_TILE
    wstart = jnp.minimum((base[None, :] + cnt_start) // ROW_ALIGN * ROW_ALIGN, n_slots - WINDOW)
    as_i32 = lambda a: a.astype(i32)
    return (as_i32(tile_e), as_i32(tile_end[-1:]), as_i32(ls0), as_i32(c_lo), as_i32(n_g), base,
            as_i32(wstart).reshape(-1))


def _moe_ffn(x, mods, layer, g, wr, br, w1, w3, w2, final_gain, final):
    batch, seq, _ = x.shape
    n_tok = batch * seq
    d_ff = w1.shape[3]
    j = layer // 2
    xt = x.reshape(n_tok, D_MODEL)
    params = lambda: pltpu.CompilerParams(
        dimension_semantics=("arbitrary",), vmem_limit_bytes=VMEM_LIMIT)

    route_tiles_per_seq = seq // ROUTE_TILE
    h, rank, wts, rank_row, cnt_end = pl.pallas_call(
        _router_kernel,
        grid=(n_tok // ROUTE_TILE,),
        in_specs=[
            pl.BlockSpec((ROUTE_TILE, D_MODEL), lambda i: (i, 0)),
            pl.BlockSpec((1, 1, 3 * D_MODEL),
                         lambda i: (layer * PAD_B + i // route_tiles_per_seq, 0, 0)),
            _layer_spec((1, D_MODEL), layer),
            _layer_spec((D_MODEL, LANES), j),
            _layer_spec((1, LANES), j),
        ],
        out_specs=[
            pl.BlockSpec((ROUTE_TILE, D_MODEL), lambda i: (i, 0)),
            pl.BlockSpec((ROUTE_TILE, LANES), lambda i: (i, 0)),
            pl.BlockSpec((ROUTE_TILE, LANES), lambda i: (i, 0)),
            pl.BlockSpec((N_EXPERTS, ROUTE_TILE), lambda i: (0, i)),
            pl.BlockSpec((ROUTE_TILE // CNT_CHUNK, LANES), lambda i: (i, 0)),
        ],
        out_shape=[
            jax.ShapeDtypeStruct((n_tok, D_MODEL), BF16),
            jax.ShapeDtypeStruct((n_tok, LANES), F32),
            jax.ShapeDtypeStruct((n_tok, LANES), F32),
            jax.ShapeDtypeStruct((N_EXPERTS, n_tok), F32),
            jax.ShapeDtypeStruct((n_tok // CNT_CHUNK, LANES), jnp.int32),
        ],
        scratch_shapes=[pltpu.VMEM((SUBLANES, LANES), F32)],
        compiler_params=params(),
        name="moe_router",
    )(xt, mods, g, wr, br)

    tile_e, n_live, ls0, c_lo, n_g, base, wstart = _route_metadata(cnt_end, n_tok)
    max_tiles = tile_e.shape[0]
    n_slots = max_tiles * SLOT_TILE

    x_sorted = pl.pallas_call(
        _dispatch_kernel,
        grid_spec=pltpu.PrefetchScalarGridSpec(
            num_scalar_prefetch=4,
            grid=(max_tiles,),
            in_specs=[pl.BlockSpec(memory_space=pltpu.VMEM),
                      pl.BlockSpec(memory_space=pltpu.VMEM)],
            out_specs=pl.BlockSpec((SLOT_TILE, D_MODEL), lambda i, *_: (i, 0)),
        ),
        out_shape=jax.ShapeDtypeStruct((n_slots, D_MODEL), BF16),
        compiler_params=params(),
        name="moe_dispatch",
    )(tile_e, ls0, c_lo, n_g, h, rank_row.reshape(N_EXPERTS, n_tok // CNT_CHUNK, CNT_CHUNK))

    y_sorted = pl.pallas_call(
        _expert_kernel,
        grid_spec=pltpu.PrefetchScalarGridSpec(
            num_scalar_prefetch=2,
            grid=(max_tiles,),
            in_specs=[
                pl.BlockSpec((SLOT_TILE, D_MODEL), lambda i, te, nl: (i, 0)),
                pl.BlockSpec((1, 1, D_MODEL, d_ff), lambda i, te, nl: (j, te[i], 0, 0)),
                pl.BlockSpec((1, 1, D_MODEL, d_ff), lambda i, te, nl: (j, te[i], 0, 0)),
                pl.BlockSpec((1, 1, d_ff, D_MODEL), lambda i, te, nl: (j, te[i], 0, 0)),
            ],
            out_specs=pl.BlockSpec((SLOT_TILE, D_MODEL), lambda i, te, nl: (i, 0)),
            scratch_shapes=[pltpu.VMEM((D_MODEL, 2 * d_ff), BF16),
                            pltpu.VMEM((d_ff, D_MODEL), BF16)],
        ),
        out_shape=jax.ShapeDtypeStruct((n_slots, D_MODEL), BF16),
        compiler_params=params(),
        name="moe_experts",
    )(tile_e, n_live, x_sorted, w1, w3, w2)

    rows = COMBINE_CHUNKS * CNT_CHUNK
    steps_per_seq = seq // rows
    n_win = COMBINE_CHUNKS * N_EXPERTS
    window_spec = lambda w: pl.BlockSpec(
        (pl.Element(WINDOW), pl.Element(D_MODEL)),
        lambda i, ws, bs: (pl.multiple_of(ws[i * n_win + w], ROW_ALIGN), 0))
    out = pl.pallas_call(
        functools.partial(_combine_kernel, final=final),
        grid_spec=pltpu.PrefetchScalarGridSpec(
            num_scalar_prefetch=2,
            grid=(n_tok // rows,),
            in_specs=[
                pl.BlockSpec((rows, D_MODEL), lambda i, ws, bs: (i, 0)),
                pl.BlockSpec((1, 1, 3 * D_MODEL),
                             lambda i, ws, bs: (layer * PAD_B + i // steps_per_seq, 0, 0)),
                pl.BlockSpec((rows, LANES), lambda i, ws, bs: (i, 0)),
                pl.BlockSpec((rows, LANES), lambda i, ws, bs: (i, 0)),
                pl.BlockSpec((1, D_MODEL), lambda i, ws, bs: (0, 0)),
            ] + [window_spec(w) for w in range(n_win)],
            out_specs=pl.BlockSpec((rows, D_MODEL), lambda i, ws, bs: (i, 0)),
        ),
        out_shape=jax.ShapeDtypeStruct((n_tok, D_MODEL), F32),
        compiler_params=params(),
        name="moe_combine",
    )(wstart, base, xt, mods, rank, wts, final_gain, *([y_sorted] * n_win))
    return out.reshape(x.shape)


def _block_diag_halves(w):
    depth, heads, hd, _ = w.shape
    per_tile = MXU_DIM // hd
    w = w.reshape(depth, heads // per_tile, per_tile, hd, hd)
    eye = jnp.eye(per_tile, dtype=w.dtype)
    bd = jnp.einsum("dtiab,ij->dtiajb", w, eye)
    return bd.reshape(depth, heads // per_tile, MXU_DIM, MXU_DIM)


def kernel(x, c, mix_norm, mix_mod_w, mix_mod_b, w_in, lru_conv_w, lru_conv_b, lru_wa, lru_ba,
           lru_wi, lru_bi, lru_lambda, sc_conv_w, gn_lru, gn_sc, w_out, ffn_norm, ffn_mod_w,
           ffn_mod_b, dense_w1, dense_w3, dense_w2, router_w, router_b, exp_w1, exp_w3, exp_w2,
           final_norm):
    depth = w_in.shape[0]
    batch = x.shape[0]
    c_pad = jnp.pad(c, ((0, PAD_B - batch), (0, 0)))
    mix_mods = _modulation(c_pad, mix_mod_w, mix_mod_b).reshape(depth * PAD_B, 1, 3 * D_MODEL)
    ffn_mods = _modulation(c_pad, ffn_mod_w, ffn_mod_b).reshape(depth * PAD_B, 1, 3 * D_MODEL)

    row = lambda p: p.reshape(depth, 1, p.shape[-1])
    wg = jnp.concatenate([_block_diag_halves(lru_wa), _block_diag_halves(lru_wi)],
                         axis=-1).astype(BF16)
    bg = jnp.stack([lru_ba.reshape(depth, D_LRU), lru_bi.reshape(depth, D_LRU)], axis=1)
    head = jnp.arange(MXU_DIM) // HEAD_DIM
    gmat = jnp.where(head[:, None] == head[None, :], 1.0 / HEAD_DIM, 0.0).astype(BF16)
    w_in_b = w_in.astype(BF16)
    w_out_b = w_out.astype(BF16)
    dense_w1_b, dense_w3_b, dense_w2_b = (w.astype(BF16) for w in (dense_w1, dense_w3, dense_w2))
    n_moe = router_w.shape[0]
    wr = jnp.pad(router_w, ((0, 0), (0, 0), (0, LANES - N_EXPERTS)))
    br = jnp.pad(router_b, ((0, 0), (0, LANES - N_EXPERTS)),
                 constant_values=-jnp.inf).reshape(n_moe, 1, LANES)
    final_gain = final_norm.reshape(1, D_MODEL)
    mix_g, ffn_g = row(mix_norm), row(ffn_norm)
    lcb, lam, gnl, gns = row(lru_conv_b), row(lru_lambda), row(gn_lru), row(gn_sc)

    for l in range(depth):
        x = _mixer(x, mix_mods, l, mix_g, w_in_b, lru_conv_w, lcb, wg, bg, lam, sc_conv_w, gnl,
                   gns, gmat, w_out_b)
        if l % 2 == 0:
            x = _dense_ffn(x, ffn_mods, l, ffn_g, dense_w1_b, dense_w3_b, dense_w2_b)
        else:
            x = _moe_ffn(x, ffn_mods, l, ffn_g, wr, br, exp_w1, exp_w3, exp_w2, final_gain,
                         final=(l == depth - 1))
    if depth % 2 == 1:
        raise NotImplementedError("final norm is fused into the last MoE layer")
    return x
```

```python
import functools

import jax
import jax.numpy as jnp
from jax import lax
from jax.experimental import pallas as pl
from jax.experimental.pallas import tpu as pltpu

F32 = jnp.float32
BF16 = jnp.bfloat16

D_MODEL = 1024
D_LRU = 512
D_SC = 512
D_IN = 2 * D_LRU + 3 * D_SC
HEAD_DIM = 64
LRU_CONV = 4
SC_CONV = 3
LRU_C = 8.0
N_EXPERTS = 8
EPS = 1e-6

SUBLANES = 8
LANES = 128
MXU_DIM = 256
TS = SUBLANES ** 3
PAD_B = 8
VMEM_LIMIT = 56 * 1024 * 1024


def _mod_norm(x, g, scale, shift):
    ms = jnp.mean(x * x, axis=-1, keepdims=True)
    return x * lax.rsqrt(ms + EPS) * (g * (1.0 + scale)) + shift


def _split_mod(mod):
    return mod[:, :D_MODEL], mod[:, D_MODEL:2 * D_MODEL], mod[:, 2 * D_MODEL:]


def _layer_spec(shape, layer):
    zeros = (0,) * len(shape)
    return pl.BlockSpec((1, *shape), lambda *_: (layer, *zeros))


def _mod_kernel(c_ref, w_ref, b_ref, o_ref):
    c = c_ref[...]
    s = (c * jax.nn.sigmoid(c)).astype(BF16)
    o_ref[0] = jnp.dot(s, w_ref[0].astype(BF16), preferred_element_type=F32) + b_ref[0]


def _modulation(c_pad, w, b):
    depth = w.shape[0]
    n_col = 3 * D_MODEL // D_MODEL
    return pl.pallas_call(
        _mod_kernel,
        grid=(depth, n_col),
        in_specs=[
            pl.BlockSpec((PAD_B, D_MODEL), lambda l, j: (0, 0)),
            pl.BlockSpec((1, D_MODEL, D_MODEL), lambda l, j: (l, 0, j)),
            pl.BlockSpec((1, 1, D_MODEL), lambda l, j: (l, 0, j)),
        ],
        out_specs=pl.BlockSpec((1, PAD_B, D_MODEL), lambda l, j: (l, 0, j)),
        out_shape=jax.ShapeDtypeStruct((depth, PAD_B, 3 * D_MODEL), F32),
        compiler_params=pltpu.CompilerParams(
            dimension_semantics=("arbitrary", "arbitrary"), vmem_limit_bytes=VMEM_LIMIT),
        name="adaln_mod",
    )(c_pad, w, b.reshape(depth, 1, 3 * D_MODEL))


def _scan8(a, u, row_in_group):
    for d in (1, 2, 4):
        keep = row_in_group >= d
        a_sh = jnp.where(keep, pltpu.roll(a, d, axis=0), 1.0)
        u_sh = jnp.where(keep, pltpu.roll(u, d, axis=0), 0.0)
        u = a * u_sh + u
        a = a * a_sh
    return a, u


def _scan_groups(a_ref, u_ref, lb, n):
    acc_a = acc_u = None
    for s in range(SUBLANES):
        rows = pl.ds(s, n, stride=SUBLANES)
        a_s = a_ref[lb, rows, :]
        u_s = u_ref[lb, rows, :]
        if s == 0:
            acc_a, acc_u = a_s, u_s
        else:
            acc_u = a_s * acc_u + u_s
            acc_a = a_s * acc_a
            a_ref[lb, rows, :] = acc_a
            u_ref[lb, rows, :] = acc_u
    return acc_a, acc_u


def _apply_prefix(a_ref, u_ref, lb, n, prefix):
    for s in range(SUBLANES):
        rows = pl.ds(s, n, stride=SUBLANES)
        u_ref[lb, rows, :] = u_ref[lb, rows, :] + a_ref[lb, rows, :] * prefix


def _row_in_group(rows, cols):
    return lax.broadcasted_iota(jnp.int32, (rows, cols), 0) % SUBLANES


def _shift_rows_down(x, first_row):
    row = lax.broadcasted_iota(jnp.int32, x.shape, 0)
    return jnp.where(row == 0, first_row, pltpu.roll(x, 1, axis=0))


def _group_rms(y, gmat_ref, gain):
    y2 = (y * y).astype(BF16)
    ms = jnp.concatenate(
        [jnp.dot(y2[:, :MXU_DIM], gmat_ref[...], preferred_element_type=F32),
         jnp.dot(y2[:, MXU_DIM:], gmat_ref[...], preferred_element_type=F32)], axis=-1)
    return y * lax.rsqrt(ms + EPS) * gain


def _mixer_kernel(x_ref, mod_ref, g_ref, win_ref, lcw_ref, lcb_ref, wg_ref, bg_ref, lam_ref,
                  scw_ref, gnl_ref, gns_ref, gmat_ref, wout_ref, o_ref,
                  xl_buf, v_buf, a0_ref, u0_ref, a1_ref, u1_ref, carry_ref):
    ts = TS
    n1 = ts // SUBLANES
    n2 = n1 // SUBLANES

    @pl.when(pl.program_id(1) == 0)
    def _():
        xl_buf[0:SUBLANES, :] = jnp.zeros((SUBLANES, D_LRU), F32)
        v_buf[0:SUBLANES, :] = jnp.zeros((SUBLANES, D_SC), F32)
        carry_ref[...] = jnp.zeros(carry_ref.shape, F32)

    x = x_ref[0]
    shift, scale, gate = _split_mod(mod_ref[0])
    h = _mod_norm(x, g_ref[0], scale, shift).astype(BF16)
    def in_proj(lo, hi):
        return jnp.dot(h, win_ref[0, :, lo:hi], preferred_element_type=F32)

    x_lru = in_proj(0, D_LRU)
    xl_buf[SUBLANES:SUBLANES + ts, :] = x_lru
    xc = lcb_ref[0]
    for k in range(LRU_CONV):
        off = SUBLANES - (LRU_CONV - 1) + k
        xc = xc + lcw_ref[0, k:k + 1, :] * xl_buf[pl.ds(off, ts), :]
    xl_buf[0:SUBLANES, :] = xl_buf[ts:ts + SUBLANES, :]

    cx_sc = in_proj(2 * D_LRU + D_SC, D_IN)
    xcb = xc.astype(BF16)
    g0 = jnp.dot(xcb[:, :MXU_DIM], wg_ref[0, 0], preferred_element_type=F32)
    g1 = jnp.dot(xcb[:, MXU_DIM:], wg_ref[0, 1], preferred_element_type=F32)
    v_buf[SUBLANES:SUBLANES + ts, :] = cx_sc[:, :D_SC] * cx_sc[:, D_SC:]
    cv = jnp.zeros((ts, D_SC), F32)
    for k in range(SC_CONV):
        off = SUBLANES - (SC_CONV - 1) + k
        cv = cv + scw_ref[0, k:k + 1, :] * v_buf[pl.ds(off, ts), :]
    v_buf[0:SUBLANES, :] = v_buf[ts:ts + SUBLANES, :]

    b_sc = in_proj(2 * D_LRU, 2 * D_LRU + D_SC)
    r = jax.nn.sigmoid(jnp.concatenate([g0[:, :MXU_DIM], g1[:, :MXU_DIM]], axis=-1)
                       + bg_ref[0, 0:1, :])
    i = jax.nn.sigmoid(jnp.concatenate([g0[:, MXU_DIM:], g1[:, MXU_DIM:]], axis=-1)
                       + bg_ref[0, 1:2, :])
    log_a = r * ((-LRU_C) * jax.nn.softplus(-lam_ref[0]))
    a = jnp.exp(log_a)
    uin = jnp.sqrt(-jnp.tanh(log_a) * (1.0 + a * a)) * (i * xc)
    yn_sc = _group_rms(b_sc * cv, gmat_ref, gns_ref[0]).astype(BF16)
    out_sc = jnp.dot(yn_sc, wout_ref[0, D_LRU:, :], preferred_element_type=F32)

    gate_lru = jax.nn.gelu(in_proj(D_LRU, 2 * D_LRU))

    last = SUBLANES - 1
    hs_blocks = []
    for lb in range(D_LRU // LANES):
        lanes = slice(lb * LANES, (lb + 1) * LANES)
        h_prev = carry_ref[lb, last:last + 1, :]
        a0_ref[lb] = a[:, lanes]
        u0_ref[lb] = uin[:, lanes]
        a1, u1 = _scan_groups(a0_ref, u0_ref, lb, n1)
        a1_ref[lb] = a1
        u1_ref[lb] = u1
        a2, u2 = _scan_groups(a1_ref, u1_ref, lb, n2)
        a2, u2 = _scan8(a2, u2, _row_in_group(n2, LANES))
        h2 = u2 + a2 * h_prev
        carry_ref[lb] = h2
        _apply_prefix(a1_ref, u1_ref, lb, n2, _shift_rows_down(h2, h_prev))
        h1 = u1_ref[lb]
        _apply_prefix(a0_ref, u0_ref, lb, n1, _shift_rows_down(h1, h_prev))
        hs_blocks.append(u0_ref[lb])
    hs = jnp.concatenate(hs_blocks, axis=-1)

    yn_lru = _group_rms(gate_lru * hs, gmat_ref, gnl_ref[0]).astype(BF16)
    out = out_sc + jnp.dot(yn_lru, wout_ref[0, :D_LRU, :], preferred_element_type=F32)
    o_ref[0] = x + gate * out


def _mixer(x, mods, layer, g, w_in, lcw, lcb, wg, bg, lam, scw, gnl, gns, gmat, w_out):
    batch, seq, _ = x.shape
    ts = TS
    nlb = D_LRU // LANES
    per_layer = lambda shape: _layer_spec(shape, layer)
    return pl.pallas_call(
        _mixer_kernel,
        grid=(batch, seq // ts),
        in_specs=[
            pl.BlockSpec((1, ts, D_MODEL), lambda b, t: (b, t, 0)),
            pl.BlockSpec((1, 1, 3 * D_MODEL), lambda b, t: (layer * PAD_B + b, 0, 0)),
            per_layer((1, D_MODEL)),
            per_layer((D_MODEL, D_IN)),
            per_layer((LRU_CONV, D_LRU)),
            per_layer((1, D_LRU)),
            per_layer((2, MXU_DIM, 2 * MXU_DIM)),
            per_layer((2, D_LRU)),
            per_layer((1, D_LRU)),
            per_layer((SC_CONV, D_SC)),
            per_layer((1, D_LRU)),
            per_layer((1, D_SC)),
            pl.BlockSpec((MXU_DIM, MXU_DIM), lambda b, t: (0, 0)),
            per_layer((D_LRU + D_SC, D_MODEL)),
        ],
        out_specs=pl.BlockSpec((1, ts, D_MODEL), lambda b, t: (b, t, 0)),
        out_shape=jax.ShapeDtypeStruct(x.shape, F32),
        scratch_shapes=[
            pltpu.VMEM((ts + SUBLANES, D_LRU), F32),
            pltpu.VMEM((ts + SUBLANES, D_SC), F32),
            pltpu.VMEM((nlb, ts, LANES), F32),
            pltpu.VMEM((nlb, ts, LANES), F32),
            pltpu.VMEM((nlb, ts // SUBLANES, LANES), F32),
            pltpu.VMEM((nlb, ts // SUBLANES, LANES), F32),
            pltpu.VMEM((nlb, SUBLANES, LANES), F32),
        ],
        compiler_params=pltpu.CompilerParams(
            dimension_semantics=("arbitrary", "arbitrary"), vmem_limit_bytes=VMEM_LIMIT),
        name="mixer",
    )(x, mods, g, w_in, lcw, lcb, wg, bg, lam, scw, gnl, gns, gmat, w_out)


def _swiglu(h, w1, w3, w2):
    a = jnp.dot(h, w1, preferred_element_type=F32)
    b = jnp.dot(h, w3, preferred_element_type=F32)
    g = (a * jax.nn.sigmoid(a) * b).astype(BF16)
    return jnp.dot(g, w2, preferred_element_type=F32)


def _mxu_halves(width):
    split = pl.cdiv(width // 2, MXU_DIM) * MXU_DIM
    return slice(0, split), slice(split, width)


def _dense_ffn_kernel(x_ref, mod_ref, g_ref, w1_ref, w3_ref, w2_ref, o_ref):
    x = x_ref[0]
    shift, scale, gate = _split_mod(mod_ref[0])
    h = _mod_norm(x, g_ref[0], scale, shift).astype(BF16)
    f = jnp.zeros(x.shape, F32)
    for cols in _mxu_halves(w1_ref.shape[2]):
        f = f + _swiglu(h, w1_ref[0, :, cols], w3_ref[0, :, cols], w2_ref[0, cols, :])
    o_ref[0] = x + gate * f


def _dense_ffn(x, mods, layer, g, w1, w3, w2, tm=512):
    batch, seq, _ = x.shape
    d_ff = w1.shape[2]
    j = layer // 2
    return pl.pallas_call(
        _dense_ffn_kernel,
        grid=(batch, seq // tm),
        in_specs=[
            pl.BlockSpec((1, tm, D_MODEL), lambda b, t: (b, t, 0)),
            pl.BlockSpec((1, 1, 3 * D_MODEL), lambda b, t: (layer * PAD_B + b, 0, 0)),
            _layer_spec((1, D_MODEL), layer),
            _layer_spec((D_MODEL, d_ff), j),
            _layer_spec((D_MODEL, d_ff), j),
            _layer_spec((d_ff, D_MODEL), j),
        ],
        out_specs=pl.BlockSpec((1, tm, D_MODEL), lambda b, t: (b, t, 0)),
        out_shape=jax.ShapeDtypeStruct(x.shape, F32),
        compiler_params=pltpu.CompilerParams(
            dimension_semantics=("arbitrary", "arbitrary"), vmem_limit_bytes=VMEM_LIMIT),
        name="dense_ffn",
    )(x, mods, g, w1, w3, w2)


ROUTE_TILE = 1024
CNT_CHUNK = 128
COMBINE_CHUNKS = 4
SLOT_TILE = 512
GATHER_SLOTS = 128
GATHER_CHUNKS = 6
ROW_ALIGN = 16
WINDOW = CNT_CHUNK + ROW_ALIGN


def _top2(logits):
    lane = lax.broadcasted_iota(jnp.int32, logits.shape, 1)
    m1 = jnp.max(logits, axis=-1, keepdims=True)
    i1 = jnp.min(jnp.where(logits == m1, lane, LANES), axis=-1, keepdims=True)
    rest = jnp.where(lane == i1, -jnp.inf, logits)
    m2 = jnp.max(rest, axis=-1, keepdims=True)
    i2 = jnp.min(jnp.where(rest == m2, lane, LANES), axis=-1, keepdims=True)
    e2 = jnp.exp(m2 - m1)
    w_first = 1.0 / (1.0 + e2)
    w_second = e2 / (1.0 + e2)
    first = lane == i1
    second = lane == i2
    weights = jnp.where(first, w_first, 0.0) + jnp.where(second, w_second, 0.0)
    return weights, first | second


def _split_bf16(v):
    hi = v.astype(BF16)
    return hi, (v - hi.astype(F32)).astype(BF16)


def _router_kernel(x_ref, mod_ref, g_ref, wr_ref, br_ref,
                   h_ref, rank_ref, w_ref, rank_row_ref, cnt_ref, carry_ref):
    n = ROUTE_TILE
    n_chunks = n // CNT_CHUNK

    @pl.when(pl.program_id(0) == 0)
    def _():
        carry_ref[...] = jnp.zeros(carry_ref.shape, F32)

    shift, scale, _ = _split_mod(mod_ref[0])
    h = _mod_norm(x_ref[...], g_ref[0], scale, shift)
    h_hi, h_lo = _split_bf16(h)
    w_hi, w_lo = _split_bf16(wr_ref[0])
    both = jnp.dot(h_hi, jnp.concatenate([w_hi, w_lo], axis=1), preferred_element_type=F32)
    logits = (both[:, :LANES] + both[:, LANES:]
              + jnp.dot(h_lo, w_hi, preferred_element_type=F32)) + br_ref[0]
    weights, mask = _top2(logits)
    m = jnp.where(mask, 1.0, 0.0).astype(BF16)
    before = carry_ref[0:1, :]
    chunk = lax.broadcasted_iota(jnp.int32, (n_chunks, n), 0)
    tok = lax.broadcasted_iota(jnp.int32, (n_chunks, n), 1)
    in_earlier_chunk = jnp.where(tok < chunk * CNT_CHUNK, 1.0, 0.0).astype(BF16)
    upto_chunk_end = jnp.where(tok < (chunk + 1) * CNT_CHUNK, 1.0, 0.0).astype(BF16)
    chunk_start = jnp.dot(in_earlier_chunk, m, preferred_element_type=F32) + before
    cnt = jnp.dot(upto_chunk_end, m, preferred_element_type=F32) + before
    row = lax.broadcasted_iota(jnp.int32, (CNT_CHUNK, CNT_CHUNK), 0)
    col = lax.broadcasted_iota(jnp.int32, (CNT_CHUNK, CNT_CHUNK), 1)
    strictly_lower = jnp.where(row > col, 1.0, 0.0).astype(BF16)
    in_chunk = jnp.concatenate(
        [jnp.dot(strictly_lower, m[j * CNT_CHUNK:(j + 1) * CNT_CHUNK, :],
                 preferred_element_type=F32) + chunk_start[j:j + 1, :]
         for j in range(n_chunks)], axis=0)
    rank = jnp.where(mask, in_chunk, -1.0)
    carry_ref[...] = jnp.broadcast_to(cnt[n_chunks - 1:, :], carry_ref.shape)
    h_ref[...] = h_hi
    rank_ref[...] = rank
    w_ref[...] = weights
    rank_row_ref[...] = rank.T[:N_EXPERTS, :]
    cnt_ref[...] = cnt.astype(jnp.int32)


def _dispatch_kernel(tile_e_ref, ls0_ref, c_lo_ref, n_g_ref, h_ref, rank_ref, o_ref):
    i = pl.program_id(0)
    e = tile_e_ref[i]
    n_chunks = rank_ref.shape[1]
    span = GATHER_CHUNKS * CNT_CHUNK
    subs = SLOT_TILE // GATHER_SLOTS

    def gather(sub, k):
        q = i * subs + sub
        slot = (ls0_ref[q]
                + lax.broadcasted_iota(jnp.int32, (GATHER_SLOTS, CNT_CHUNK), 0)).astype(F32)
        wanted = c_lo_ref[q] + k * GATHER_CHUNKS
        start = jnp.minimum(wanted, n_chunks - GATHER_CHUNKS)
        p = jnp.concatenate(
            [jnp.where((start + j >= wanted) & (rank_ref[e, pl.ds(start + j, 1), :] == slot),
                       1.0, 0.0) for j in range(GATHER_CHUNKS)], axis=1).astype(BF16)
        tokens = pl.ds(pl.multiple_of(start * CNT_CHUNK, CNT_CHUNK), span)
        return jnp.dot(p, h_ref[tokens, :], preferred_element_type=F32).astype(BF16)

    for sub in range(subs):
        o_ref[sub * GATHER_SLOTS:(sub + 1) * GATHER_SLOTS, :] = gather(sub, 0)
    for sub in range(subs):
        def more(k, carry, sub=sub):
            o_ref[sub * GATHER_SLOTS:(sub + 1) * GATHER_SLOTS, :] += gather(sub, k)
            return carry

        lax.fori_loop(1, n_g_ref[i * subs + sub], more, 0)


def _expert_kernel(tile_e_ref, n_live_ref, x_ref, w1_ref, w3_ref, w2_ref, o_ref,
                   w13b_ref, w2b_ref):
    k = pl.program_id(0)
    n_tiles = pl.num_programs(0) - 1
    d_ff = w2b_ref.shape[0]
    tile = k - 1
    live = (tile >= 0) & (tile < n_live_ref[0])
    expert_of = lambda t: tile_e_ref[jnp.clip(t, 0, n_tiles - 1)]

    @pl.when((tile == 0) | ((tile > 0) & (expert_of(tile) != expert_of(tile - 1))))
    def _():
        w13b_ref[:, :d_ff] = w1_ref[0, 0].astype(BF16)
        w13b_ref[:, d_ff:] = w3_ref[0, 0].astype(BF16)

    @pl.when(live)
    def _():
        ab = jnp.dot(x_ref[...], w13b_ref[...], preferred_element_type=F32)
        a, b = ab[:, :d_ff], ab[:, d_ff:]
        g = (a * jax.nn.sigmoid(a) * b).astype(BF16)
        o_ref[...] = jnp.dot(g, w2b_ref[...], preferred_element_type=F32).astype(BF16)

    @pl.when((tile >= 0) & jnp.logical_not(live))
    def _():
        o_ref[...] = jnp.zeros(o_ref.shape, BF16)

    @pl.when((k == 0) | ((k < n_tiles) & (expert_of(k) != expert_of(tile))))
    def _():
        w2b_ref[...] = w2_ref[0, 0].astype(BF16)


def _combine_kernel(wstart_ref, base_ref, x_ref, mod_ref, rank_ref, w_ref, fg_ref, *rest, final):
    n_win = COMBINE_CHUNKS * N_EXPERTS
    y_refs, o_ref = rest[:n_win], rest[n_win]
    i = pl.program_id(0)
    lane = lax.broadcasted_iota(jnp.int32, (CNT_CHUNK, WINDOW), 1)
    gate = mod_ref[0][:, 2 * D_MODEL:]
    for s in range(COMBINE_CHUNKS):
        rows = slice(s * CNT_CHUNK, (s + 1) * CNT_CHUNK)
        rank = rank_ref[rows, :]
        wts = w_ref[rows, :]
        acc = jnp.zeros((CNT_CHUNK, D_MODEL), F32)
        for e in range(N_EXPERTS):
            w = s * N_EXPERTS + e
            rk = rank[:, e:e + 1]
            slot = rk + base_ref[e].astype(F32)
            window_slot = (wstart_ref[i * n_win + w] + lane).astype(F32)
            hit = (rk >= 0.0) & (slot == window_slot)
            p = jnp.where(hit, wts[:, e:e + 1], 0.0).astype(BF16)
            acc = acc + jnp.dot(p, y_refs[w][...], preferred_element_type=F32)
        y = x_ref[rows, :] + gate * acc
        if final:
            ms = jnp.mean(y * y, axis=-1, keepdims=True)
            y = y * lax.rsqrt(ms + EPS) * fg_ref[...]
        o_ref[rows, :] = y


def _route_metadata(cnt_end, n_tok):
    i32 = jnp.int32
    cnt_end = cnt_end[:, :N_EXPERTS]
    cnt_start = jnp.concatenate([jnp.zeros((1, N_EXPERTS), i32), cnt_end[:-1]], axis=0)
    n_chunks = cnt_end.shape[0]
    total = cnt_end[-1]
    n_tiles = (total + SLOT_TILE - 1) // SLOT_TILE
    tile_end = jnp.cumsum(n_tiles)
    tile_start = tile_end - n_tiles
    base = (tile_start * SLOT_TILE).astype(i32)
    max_tiles = 2 * n_tok // SLOT_TILE + N_EXPERTS
    i = jnp.arange(max_tiles, dtype=i32)
    tile_e = jnp.minimum(jnp.sum(i[:, None] >= tile_end[None, :], axis=1), N_EXPERTS - 1)
    subs = SLOT_TILE // GATHER_SLOTS
    q = jnp.arange(max_tiles * subs, dtype=i32)
    q_e = tile_e[q // subs]
    ls0 = (q // subs - tile_start[q_e]) * SLOT_TILE + (q % subs) * GATHER_SLOTS
    live = (q // subs < tile_end[-1]) & (ls0 < total[q_e])
    ends = cnt_end[:, q_e].T
    starts = cnt_start[:, q_e].T
    c_lo = jnp.sum(ends <= ls0[:, None], axis=1)
    c_hi = jnp.sum(starts < (ls0 + GATHER_SLOTS)[:, None], axis=1) - 1
    n_g = jnp.where(live, (c_hi - c_lo + GATHER_CHUNKS) // GATHER_CHUNKS, 0)
    c_lo = jnp.minimum(c_lo, n_chunks - 1)
    n_slots = max_tiles * SLOT_TILE
    wstart = jnp.minimum((base[None, :] + cnt_start) // ROW_ALIGN * ROW_ALIGN, n_slots - WINDOW)
    as_i32 = lambda a: a.astype(i32)
    return (as_i32(tile_e), as_i32(tile_end[-1:]), as_i32(ls0), as_i32(c_lo), as_i32(n_g), base,
            as_i32(wstart).reshape(-1))


def _moe_ffn(x, mods, layer, g, wr, br, w1, w3, w2, final_gain, final):
    batch, seq, _ = x.shape
    n_tok = batch * seq
    d_ff = w1.shape[3]
    j = layer // 2
    xt = x.reshape(n_tok, D_MODEL)
    params = lambda: pltpu.CompilerParams(
        dimension_semantics=("arbitrary",), vmem_limit_bytes=VMEM_LIMIT)

    route_tiles_per_seq = seq // ROUTE_TILE
    h, rank, wts, rank_row, cnt_end = pl.pallas_call(
        _router_kernel,
        grid=(n_tok // ROUTE_TILE,),
        in_specs=[
            pl.BlockSpec((ROUTE_TILE, D_MODEL), lambda i: (i, 0)),
            pl.BlockSpec((1, 1, 3 * D_MODEL),
                         lambda i: (layer * PAD_B + i // route_tiles_per_seq, 0, 0)),
            _layer_spec((1, D_MODEL), layer),
            _layer_spec((D_MODEL, LANES), j),
            _layer_spec((1, LANES), j),
        ],
        out_specs=[
            pl.BlockSpec((ROUTE_TILE, D_MODEL), lambda i: (i, 0)),
            pl.BlockSpec((ROUTE_TILE, LANES), lambda i: (i, 0)),
            pl.BlockSpec((ROUTE_TILE, LANES), lambda i: (i, 0)),
            pl.BlockSpec((N_EXPERTS, ROUTE_TILE), lambda i: (0, i)),
            pl.BlockSpec((ROUTE_TILE // CNT_CHUNK, LANES), lambda i: (i, 0)),
        ],
        out_shape=[
            jax.ShapeDtypeStruct((n_tok, D_MODEL), BF16),
            jax.ShapeDtypeStruct((n_tok, LANES), F32),
            jax.ShapeDtypeStruct((n_tok, LANES), F32),
            jax.ShapeDtypeStruct((N_EXPERTS, n_tok), F32),
            jax.ShapeDtypeStruct((n_tok // CNT_CHUNK, LANES), jnp.int32),
        ],
        scratch_shapes=[pltpu.VMEM((SUBLANES, LANES), F32)],
        compiler_params=params(),
        name="moe_router",
    )(xt, mods, g, wr, br)

    tile_e, n_live, ls0, c_lo, n_g, base, wstart = _route_metadata(cnt_end, n_tok)
    max_tiles = tile_e.shape[0]
    n_slots = max_tiles * SLOT_TILE

    x_sorted = pl.pallas_call(
        _dispatch_kernel,
        grid_spec=pltpu.PrefetchScalarGridSpec(
            num_scalar_prefetch=4,
            grid=(max_tiles,),
            in_specs=[pl.BlockSpec(memory_space=pltpu.VMEM),
                      pl.BlockSpec(memory_space=pltpu.VMEM)],
            out_specs=pl.BlockSpec((SLOT_TILE, D_MODEL), lambda i, *_: (i, 0)),
        ),
        out_shape=jax.ShapeDtypeStruct((n_slots, D_MODEL), BF16),
        compiler_params=params(),
        name="moe_dispatch",
    )(tile_e, ls0, c_lo, n_g, h, rank_row.reshape(N_EXPERTS, n_tok // CNT_CHUNK, CNT_CHUNK))

    this_expert = lambda k, te, nl: (j, te[jnp.maximum(k - 1, 0)], 0, 0)
    next_expert = lambda k, te, nl: (j, te[jnp.minimum(k, max_tiles - 1)], 0, 0)
    y_sorted = pl.pallas_call(
        _expert_kernel,
        grid_spec=pltpu.PrefetchScalarGridSpec(
            num_scalar_prefetch=2,
            grid=(max_tiles + 1,),
            in_specs=[
                pl.BlockSpec((SLOT_TILE, D_MODEL), lambda k, te, nl: (jnp.maximum(k - 1, 0), 0)),
                pl.BlockSpec((1, 1, D_MODEL, d_ff), this_expert),
                pl.BlockSpec((1, 1, D_MODEL, d_ff), this_expert),
                pl.BlockSpec((1, 1, d_ff, D_MODEL), next_expert),
            ],
            out_specs=pl.BlockSpec((SLOT_TILE, D_MODEL),
                                   lambda k, te, nl: (jnp.maximum(k - 1, 0), 0)),
            scratch_shapes=[pltpu.VMEM((D_MODEL, 2 * d_ff), BF16),
                            pltpu.VMEM((d_ff, D_MODEL), BF16)],
        ),
        out_shape=jax.ShapeDtypeStruct((n_slots, D_MODEL), BF16),
        compiler_params=params(),
        name="moe_experts",
    )(tile_e, n_live, x_sorted, w1, w3, w2)

    rows = COMBINE_CHUNKS * CNT_CHUNK
    steps_per_seq = seq // rows
    n_win = COMBINE_CHUNKS * N_EXPERTS
    window_spec = lambda w: pl.BlockSpec(
        (pl.Element(WINDOW), pl.Element(D_MODEL)),
        lambda i, ws, bs: (pl.multiple_of(ws[i * n_win + w], ROW_ALIGN), 0))
    out = pl.pallas_call(
        functools.partial(_combine_kernel, final=final),
        grid_spec=pltpu.PrefetchScalarGridSpec(
            num_scalar_prefetch=2,
            grid=(n_tok // rows,),
            in_specs=[
                pl.BlockSpec((rows, D_MODEL), lambda i, ws, bs: (i, 0)),
                pl.BlockSpec((1, 1, 3 * D_MODEL),
                             lambda i, ws, bs: (layer * PAD_B + i // steps_per_seq, 0, 0)),
                pl.BlockSpec((rows, LANES), lambda i, ws, bs: (i, 0)),
                pl.BlockSpec((rows, LANES), lambda i, ws, bs: (i, 0)),
                pl.BlockSpec((1, D_MODEL), lambda i, ws, bs: (0, 0)),
            ] + [window_spec(w) for w in range(n_win)],
            out_specs=pl.BlockSpec((rows, D_MODEL), lambda i, ws, bs: (i, 0)),
        ),
        out_shape=jax.ShapeDtypeStruct((n_tok, D_MODEL), F32),
        compiler_params=params(),
        name="moe_combine",
    )(wstart, base, xt, mods, rank, wts, final_gain, *([y_sorted] * n_win))
    return out.reshape(x.shape)


def _block_diag_halves(w):
    depth, heads, hd, _ = w.shape
    per_tile = MXU_DIM // hd
    w = w.reshape(depth, heads // per_tile, per_tile, hd, hd)
    eye = jnp.eye(per_tile, dtype=w.dtype)
    bd = jnp.einsum("dtiab,ij->dtiajb", w, eye)
    return bd.reshape(depth, heads // per_tile, MXU_DIM, MXU_DIM)


def kernel(x, c, mix_norm, mix_mod_w, mix_mod_b, w_in, lru_conv_w, lru_conv_b, lru_wa, lru_ba,
           lru_wi, lru_bi, lru_lambda, sc_conv_w, gn_lru, gn_sc, w_out, ffn_norm, ffn_mod_w,
           ffn_mod_b, dense_w1, dense_w3, dense_w2, router_w, router_b, exp_w1, exp_w3, exp_w2,
           final_norm):
    depth = w_in.shape[0]
    batch = x.shape[0]
    c_pad = jnp.pad(c, ((0, PAD_B - batch), (0, 0)))
    mix_mods = _modulation(c_pad, mix_mod_w, mix_mod_b).reshape(depth * PAD_B, 1, 3 * D_MODEL)
    ffn_mods = _modulation(c_pad, ffn_mod_w, ffn_mod_b).reshape(depth * PAD_B, 1, 3 * D_MODEL)

    row = lambda p: p.reshape(depth, 1, p.shape[-1])
    wg = jnp.concatenate([_block_diag_halves(lru_wa), _block_diag_halves(lru_wi)],
                         axis=-1).astype(BF16)
    bg = jnp.stack([lru_ba.reshape(depth, D_LRU), lru_bi.reshape(depth, D_LRU)], axis=1)
    head = jnp.arange(MXU_DIM) // HEAD_DIM
    gmat = jnp.where(head[:, None] == head[None, :], 1.0 / HEAD_DIM, 0.0).astype(BF16)
    w_in_b = w_in.astype(BF16)
    w_out_b = w_out.astype(BF16)
    dense_w1_b, dense_w3_b, dense_w2_b = (w.astype(BF16) for w in (dense_w1, dense_w3, dense_w2))
    n_moe = router_w.shape[0]
    wr = jnp.pad(router_w, ((0, 0), (0, 0), (0, LANES - N_EXPERTS)))
    br = jnp.pad(router_b, ((0, 0), (0, LANES - N_EXPERTS)),
                 constant_values=-jnp.inf).reshape(n_moe, 1, LANES)
    final_gain = final_norm.reshape(1, D_MODEL)
    mix_g, ffn_g = row(mix_norm), row(ffn_norm)
    lcb, lam, gnl, gns = row(lru_conv_b), row(lru_lambda), row(gn_lru), row(gn_sc)

    for l in range(depth):
        x = _mixer(x, mix_mods, l, mix_g, w_in_b, lru_conv_w, lcb, wg, bg, lam, sc_conv_w, gnl,
                   gns, gmat, w_out_b)
        if l % 2 == 0:
            x = _dense_ffn(x, ffn_mods, l, ffn_g, dense_w1_b, dense_w3_b, dense_w2_b)
        else:
            x = _moe_ffn(x, ffn_mods, l, ffn_g, wr, br, exp_w1, exp_w3, exp_w2, final_gain,
                         final=(l == depth - 1))
    if depth % 2 == 1:
        raise NotImplementedError("final norm is fused into the last MoE layer")
    return x
```

```python
import functools

import jax
import jax.numpy as jnp
from jax import lax
from jax.experimental import pallas as pl
from jax.experimental.pallas import tpu as pltpu

F32 = jnp.float32
BF16 = jnp.bfloat16

D_MODEL = 1024
D_LRU = 512
D_SC = 512
D_IN = 2 * D_LRU + 3 * D_SC
HEAD_DIM = 64
LRU_CONV = 4
SC_CONV = 3
LRU_C = 8.0
N_EXPERTS = 8
EPS = 1e-6

SUBLANES = 8
LANES = 128
MXU_DIM = 256
TS = SUBLANES ** 3
PAD_B = 8
VMEM_LIMIT = 56 * 1024 * 1024


def _mod_norm(x, g, scale, shift):
    ms = jnp.mean(x * x, axis=-1, keepdims=True)
    return x * lax.rsqrt(ms + EPS) * (g * (1.0 + scale)) + shift


def _split_mod(mod):
    return mod[:, :D_MODEL], mod[:, D_MODEL:2 * D_MODEL], mod[:, 2 * D_MODEL:]


def _layer_spec(shape, layer):
    zeros = (0,) * len(shape)
    return pl.BlockSpec((1, *shape), lambda *_: (layer, *zeros), pipeline_mode=pl.Buffered(1))


def _mod_kernel(c_ref, w_ref, b_ref, o_ref):
    c = c_ref[...]
    s = (c * jax.nn.sigmoid(c)).astype(BF16)
    o_ref[0] = jnp.dot(s, w_ref[0].astype(BF16), preferred_element_type=F32) + b_ref[0]


def _modulation(c_pad, w, b):
    depth = w.shape[0]
    n_col = 3 * D_MODEL // D_MODEL
    return pl.pallas_call(
        _mod_kernel,
        grid=(depth, n_col),
        in_specs=[
            pl.BlockSpec((PAD_B, D_MODEL), lambda l, j: (0, 0)),
            pl.BlockSpec((1, D_MODEL, D_MODEL), lambda l, j: (l, 0, j)),
            pl.BlockSpec((1, 1, D_MODEL), lambda l, j: (l, 0, j)),
        ],
        out_specs=pl.BlockSpec((1, PAD_B, D_MODEL), lambda l, j: (l, 0, j)),
        out_shape=jax.ShapeDtypeStruct((depth, PAD_B, 3 * D_MODEL), F32),
        compiler_params=pltpu.CompilerParams(
            dimension_semantics=("arbitrary", "arbitrary"), vmem_limit_bytes=VMEM_LIMIT),
        name="adaln_mod",
    )(c_pad, w, b.reshape(depth, 1, 3 * D_MODEL))


def _scan8(a, u, row_in_group):
    for d in (1, 2, 4):
        keep = row_in_group >= d
        a_sh = jnp.where(keep, pltpu.roll(a, d, axis=0), 1.0)
        u_sh = jnp.where(keep, pltpu.roll(u, d, axis=0), 0.0)
        u = a * u_sh + u
        a = a * a_sh
    return a, u


def _scan_groups(a_ref, u_ref, lb, n):
    acc_a = acc_u = None
    for s in range(SUBLANES):
        rows = pl.ds(s, n, stride=SUBLANES)
        a_s = a_ref[lb, rows, :]
        u_s = u_ref[lb, rows, :]
        if s == 0:
            acc_a, acc_u = a_s, u_s
        else:
            acc_u = a_s * acc_u + u_s
            acc_a = a_s * acc_a
            a_ref[lb, rows, :] = acc_a
            u_ref[lb, rows, :] = acc_u
    return acc_a, acc_u


def _apply_prefix(a_ref, u_ref, lb, n, prefix):
    for s in range(SUBLANES):
        rows = pl.ds(s, n, stride=SUBLANES)
        u_ref[lb, rows, :] = u_ref[lb, rows, :] + a_ref[lb, rows, :] * prefix


def _row_in_group(rows, cols):
    return lax.broadcasted_iota(jnp.int32, (rows, cols), 0) % SUBLANES


def _shift_rows_down(x, first_row):
    row = lax.broadcasted_iota(jnp.int32, x.shape, 0)
    return jnp.where(row == 0, first_row, pltpu.roll(x, 1, axis=0))


def _group_rms(y, gmat_ref, gain):
    y2 = (y * y).astype(BF16)
    ms = jnp.concatenate(
        [jnp.dot(y2[:, :MXU_DIM], gmat_ref[...], preferred_element_type=F32),
         jnp.dot(y2[:, MXU_DIM:], gmat_ref[...], preferred_element_type=F32)], axis=-1)
    return y * lax.rsqrt(ms + EPS) * gain


def _mixer_kernel(x_ref, mod_ref, g_ref, win_ref, lcw_ref, lcb_ref, wg_ref, bg_ref, lam_ref,
                  scw_ref, gnl_ref, gns_ref, gmat_ref, wout_ref, o_ref,
                  xl_buf, v_buf, a0_ref, u0_ref, a1_ref, u1_ref, carry_ref):
    @pl.when(pl.program_id(1) == 0)
    def _():
        xl_buf[0:SUBLANES, :] = jnp.zeros((SUBLANES, D_LRU), F32)
        v_buf[0:SUBLANES, :] = jnp.zeros((SUBLANES, D_SC), F32)
        carry_ref[...] = jnp.zeros(carry_ref.shape, F32)

    o_ref[0] = _mixer_tile(
        x_ref[0], mod_ref[0], g_ref, win_ref, lcw_ref, lcb_ref, wg_ref, bg_ref, lam_ref, scw_ref,
        gnl_ref, gns_ref, gmat_ref, wout_ref, xl_buf, v_buf, a0_ref, u0_ref, a1_ref, u1_ref,
        carry_ref)


def _mixer_tile(x, mod, g_ref, win_ref, lcw_ref, lcb_ref, wg_ref, bg_ref, lam_ref, scw_ref,
                gnl_ref, gns_ref, gmat_ref, wout_ref, xl_buf, v_buf, a0_ref, u0_ref, a1_ref,
                u1_ref, carry_ref):
    ts = TS
    n1 = ts // SUBLANES
    n2 = n1 // SUBLANES
    shift, scale, gate = _split_mod(mod)
    h = _mod_norm(x, g_ref[0], scale, shift).astype(BF16)
    def in_proj(lo, hi):
        return jnp.dot(h, win_ref[0, :, lo:hi], preferred_element_type=F32)

    x_lru = in_proj(0, D_LRU)
    xl_buf[SUBLANES:SUBLANES + ts, :] = x_lru
    xc = lcb_ref[0]
    for k in range(LRU_CONV):
        off = SUBLANES - (LRU_CONV - 1) + k
        xc = xc + lcw_ref[0, k:k + 1, :] * xl_buf[pl.ds(off, ts), :]
    xl_buf[0:SUBLANES, :] = xl_buf[ts:ts + SUBLANES, :]

    cx_sc = in_proj(2 * D_LRU + D_SC, D_IN)
    xcb = xc.astype(BF16)
    g0 = jnp.dot(xcb[:, :MXU_DIM], wg_ref[0, 0], preferred_element_type=F32)
    g1 = jnp.dot(xcb[:, MXU_DIM:], wg_ref[0, 1], preferred_element_type=F32)
    v_buf[SUBLANES:SUBLANES + ts, :] = cx_sc[:, :D_SC] * cx_sc[:, D_SC:]
    cv = jnp.zeros((ts, D_SC), F32)
    for k in range(SC_CONV):
        off = SUBLANES - (SC_CONV - 1) + k
        cv = cv + scw_ref[0, k:k + 1, :] * v_buf[pl.ds(off, ts), :]
    v_buf[0:SUBLANES, :] = v_buf[ts:ts + SUBLANES, :]

    b_sc = in_proj(2 * D_LRU, 2 * D_LRU + D_SC)
    r = jax.nn.sigmoid(jnp.concatenate([g0[:, :MXU_DIM], g1[:, :MXU_DIM]], axis=-1)
                       + bg_ref[0, 0:1, :])
    i = jax.nn.sigmoid(jnp.concatenate([g0[:, MXU_DIM:], g1[:, MXU_DIM:]], axis=-1)
                       + bg_ref[0, 1:2, :])
    log_a = r * ((-LRU_C) * jax.nn.softplus(-lam_ref[0]))
    a = jnp.exp(log_a)
    uin = jnp.sqrt(-jnp.tanh(log_a) * (1.0 + a * a)) * (i * xc)
    yn_sc = _group_rms(b_sc * cv, gmat_ref, gns_ref[0]).astype(BF16)
    out_sc = jnp.dot(yn_sc, wout_ref[0, D_LRU:, :], preferred_element_type=F32)

    gate_lru = jax.nn.gelu(in_proj(D_LRU, 2 * D_LRU))

    last = SUBLANES - 1
    hs_blocks = []
    for lb in range(D_LRU // LANES):
        lanes = slice(lb * LANES, (lb + 1) * LANES)
        h_prev = carry_ref[lb, last:last + 1, :]
        a0_ref[lb] = a[:, lanes]
        u0_ref[lb] = uin[:, lanes]
        a1, u1 = _scan_groups(a0_ref, u0_ref, lb, n1)
        a1_ref[lb] = a1
        u1_ref[lb] = u1
        a2, u2 = _scan_groups(a1_ref, u1_ref, lb, n2)
        a2, u2 = _scan8(a2, u2, _row_in_group(n2, LANES))
        h2 = u2 + a2 * h_prev
        carry_ref[lb] = h2
        _apply_prefix(a1_ref, u1_ref, lb, n2, _shift_rows_down(h2, h_prev))
        h1 = u1_ref[lb]
        _apply_prefix(a0_ref, u0_ref, lb, n1, _shift_rows_down(h1, h_prev))
        hs_blocks.append(u0_ref[lb])
    hs = jnp.concatenate(hs_blocks, axis=-1)

    yn_lru = _group_rms(gate_lru * hs, gmat_ref, gnl_ref[0]).astype(BF16)
    out = out_sc + jnp.dot(yn_lru, wout_ref[0, :D_LRU, :], preferred_element_type=F32)
    return x + gate * out


def _mixer(x, mods, layer, g, w_in, lcw, lcb, wg, bg, lam, scw, gnl, gns, gmat, w_out):
    batch, seq, _ = x.shape
    ts = TS
    rows = ts
    nlb = D_LRU // LANES
    per_layer = lambda shape: _layer_spec(shape, layer)
    return pl.pallas_call(
        _mixer_kernel,
        grid=(batch, seq // rows),
        in_specs=[
            pl.BlockSpec((1, rows, D_MODEL), lambda b, t: (b, t, 0)),
            pl.BlockSpec((1, 1, 3 * D_MODEL), lambda b, t: (layer * PAD_B + b, 0, 0)),
            per_layer((1, D_MODEL)),
            per_layer((D_MODEL, D_IN)),
            per_layer((LRU_CONV, D_LRU)),
            per_layer((1, D_LRU)),
            per_layer((2, MXU_DIM, 2 * MXU_DIM)),
            per_layer((2, D_LRU)),
            per_layer((1, D_LRU)),
            per_layer((SC_CONV, D_SC)),
            per_layer((1, D_LRU)),
            per_layer((1, D_SC)),
            pl.BlockSpec((MXU_DIM, MXU_DIM), lambda b, t: (0, 0)),
            per_layer((D_LRU + D_SC, D_MODEL)),
        ],
        out_specs=pl.BlockSpec((1, rows, D_MODEL), lambda b, t: (b, t, 0)),
        out_shape=jax.ShapeDtypeStruct(x.shape, F32),
        scratch_shapes=[
            pltpu.VMEM((ts + SUBLANES, D_LRU), F32),
            pltpu.VMEM((ts + SUBLANES, D_SC), F32),
            pltpu.VMEM((nlb, ts, LANES), F32),
            pltpu.VMEM((nlb, ts, LANES), F32),
            pltpu.VMEM((nlb, ts // SUBLANES, LANES), F32),
            pltpu.VMEM((nlb, ts // SUBLANES, LANES), F32),
            pltpu.VMEM((nlb, SUBLANES, LANES), F32),
        ],
        compiler_params=pltpu.CompilerParams(
            dimension_semantics=("arbitrary", "arbitrary"), vmem_limit_bytes=VMEM_LIMIT),
        name="mixer",
    )(x, mods, g, w_in, lcw, lcb, wg, bg, lam, scw, gnl, gns, gmat, w_out)


def _swiglu(h, w1, w3, w2):
    a = jnp.dot(h, w1, preferred_element_type=F32)
    b = jnp.dot(h, w3, preferred_element_type=F32)
    g = (a * jax.nn.sigmoid(a) * b).astype(BF16)
    return jnp.dot(g, w2, preferred_element_type=F32)


def _mxu_halves(width):
    split = pl.cdiv(width // 2, MXU_DIM) * MXU_DIM
    return slice(0, split), slice(split, width)


def _dense_ffn_kernel(x_ref, mod_ref, g_ref, w1_ref, w3_ref, w2_ref, o_ref):
    x = x_ref[0]
    shift, scale, gate = _split_mod(mod_ref[0])
    h = _mod_norm(x, g_ref[0], scale, shift).astype(BF16)
    f = jnp.zeros(x.shape, F32)
    for cols in _mxu_halves(w1_ref.shape[2]):
        f = f + _swiglu(h, w1_ref[0, :, cols], w3_ref[0, :, cols], w2_ref[0, cols, :])
    o_ref[0] = x + gate * f


def _dense_ffn(x, mods, layer, g, w1, w3, w2, tm=1024):
    batch, seq, _ = x.shape
    d_ff = w1.shape[2]
    j = layer // 2
    return pl.pallas_call(
        _dense_ffn_kernel,
        grid=(batch, seq // tm),
        in_specs=[
            pl.BlockSpec((1, tm, D_MODEL), lambda b, t: (b, t, 0)),
            pl.BlockSpec((1, 1, 3 * D_MODEL), lambda b, t: (layer * PAD_B + b, 0, 0)),
            _layer_spec((1, D_MODEL), layer),
            _layer_spec((D_MODEL, d_ff), j),
            _layer_spec((D_MODEL, d_ff), j),
            _layer_spec((d_ff, D_MODEL), j),
        ],
        out_specs=pl.BlockSpec((1, tm, D_MODEL), lambda b, t: (b, t, 0)),
        out_shape=jax.ShapeDtypeStruct(x.shape, F32),
        compiler_params=pltpu.CompilerParams(
            dimension_semantics=("arbitrary", "arbitrary"), vmem_limit_bytes=VMEM_LIMIT),
        name="dense_ffn",
    )(x, mods, g, w1, w3, w2)


ROUTE_TILE = 1024
CNT_CHUNK = 128
COMBINE_CHUNKS = 4
SLOT_TILE = 512
GATHER_SLOTS = 128
GATHER_CHUNKS = 6
ROW_ALIGN = 16
WINDOW = CNT_CHUNK + ROW_ALIGN


def _top2(logits):
    lane = lax.broadcasted_iota(jnp.int32, logits.shape, 1)
    m1 = jnp.max(logits, axis=-1, keepdims=True)
    i1 = jnp.min(jnp.where(logits == m1, lane, LANES), axis=-1, keepdims=True)
    rest = jnp.where(lane == i1, -jnp.inf, logits)
    m2 = jnp.max(rest, axis=-1, keepdims=True)
    i2 = jnp.min(jnp.where(rest == m2, lane, LANES), axis=-1, keepdims=True)
    e2 = jnp.exp(m2 - m1)
    w_first = 1.0 / (1.0 + e2)
    w_second = e2 / (1.0 + e2)
    first = lane == i1
    second = lane == i2
    weights = jnp.where(first, w_first, 0.0) + jnp.where(second, w_second, 0.0)
    return weights, first | second


def _split_bf16(v):
    hi = v.astype(BF16)
    return hi, (v - hi.astype(F32)).astype(BF16)


def _router_kernel(x_ref, mod_ref, g_ref, wr_ref, br_ref,
                   h_ref, rank_ref, w_ref, rank_row_ref, cnt_ref, carry_ref):
    n = ROUTE_TILE
    n_chunks = n // CNT_CHUNK

    @pl.when(pl.program_id(0) == 0)
    def _():
        carry_ref[...] = jnp.zeros(carry_ref.shape, F32)

    shift, scale, _ = _split_mod(mod_ref[0])
    h = _mod_norm(x_ref[...], g_ref[0], scale, shift)
    h_hi, h_lo = _split_bf16(h)
    w_hi, w_lo = _split_bf16(wr_ref[0])
    both = jnp.dot(h_hi, jnp.concatenate([w_hi, w_lo], axis=1), preferred_element_type=F32)
    logits = (both[:, :LANES] + both[:, LANES:]
              + jnp.dot(h_lo, w_hi, preferred_element_type=F32)) + br_ref[0]
    weights, mask = _top2(logits)
    m = jnp.where(mask, 1.0, 0.0).astype(BF16)
    before = carry_ref[0:1, :]
    chunk = lax.broadcasted_iota(jnp.int32, (n_chunks, n), 0)
    tok = lax.broadcasted_iota(jnp.int32, (n_chunks, n), 1)
    in_earlier_chunk = jnp.where(tok < chunk * CNT_CHUNK, 1.0, 0.0).astype(BF16)
    upto_chunk_end = jnp.where(tok < (chunk + 1) * CNT_CHUNK, 1.0, 0.0).astype(BF16)
    chunk_start = jnp.dot(in_earlier_chunk, m, preferred_element_type=F32) + before
    cnt = jnp.dot(upto_chunk_end, m, preferred_element_type=F32) + before
    row = lax.broadcasted_iota(jnp.int32, (CNT_CHUNK, CNT_CHUNK), 0)
    col = lax.broadcasted_iota(jnp.int32, (CNT_CHUNK, CNT_CHUNK), 1)
    strictly_lower = jnp.where(row > col, 1.0, 0.0).astype(BF16)
    in_chunk = jnp.concatenate(
        [jnp.dot(strictly_lower, m[j * CNT_CHUNK:(j + 1) * CNT_CHUNK, :],
                 preferred_element_type=F32) + chunk_start[j:j + 1, :]
         for j in range(n_chunks)], axis=0)
    rank = jnp.where(mask, in_chunk, -1.0)
    carry_ref[...] = jnp.broadcast_to(cnt[n_chunks - 1:, :], carry_ref.shape)
    h_ref[...] = h_hi
    rank_ref[...] = rank
    w_ref[...] = weights
    rank_row_ref[...] = rank.T[:N_EXPERTS, :]
    cnt_ref[...] = cnt.astype(jnp.int32)


def _dispatch_kernel(tile_e_ref, ls0_ref, c_lo_ref, n_g_ref, h_ref, rank_ref, o_ref):
    i = pl.program_id(0)
    e = tile_e_ref[i]
    n_chunks = rank_ref.shape[1]
    span = GATHER_CHUNKS * CNT_CHUNK
    subs = SLOT_TILE // GATHER_SLOTS

    def gather(sub, k):
        q = i * subs + sub
        slot = (ls0_ref[q]
                + lax.broadcasted_iota(jnp.int32, (GATHER_SLOTS, CNT_CHUNK), 0)).astype(F32)
        wanted = c_lo_ref[q] + k * GATHER_CHUNKS
        start = jnp.minimum(wanted, n_chunks - GATHER_CHUNKS)
        p = jnp.concatenate(
            [jnp.where((start + j >= wanted) & (rank_ref[e, pl.ds(start + j, 1), :] == slot),
                       1.0, 0.0) for j in range(GATHER_CHUNKS)], axis=1).astype(BF16)
        tokens = pl.ds(pl.multiple_of(start * CNT_CHUNK, CNT_CHUNK), span)
        return jnp.dot(p, h_ref[tokens, :], preferred_element_type=F32).astype(BF16)

    for sub in range(subs):
        o_ref[sub * GATHER_SLOTS:(sub + 1) * GATHER_SLOTS, :] = gather(sub, 0)
    for sub in range(subs):
        def more(k, carry, sub=sub):
            o_ref[sub * GATHER_SLOTS:(sub + 1) * GATHER_SLOTS, :] += gather(sub, k)
            return carry

        lax.fori_loop(1, n_g_ref[i * subs + sub], more, 0)


def _expert_kernel(tile_e_ref, n_live_ref, x_ref, w1_ref, w3_ref, w2_ref, o_ref,
                   w13b_ref, w2b_ref):
    k = pl.program_id(0)
    n_tiles = pl.num_programs(0) - 1
    d_ff = w2b_ref.shape[0]
    tile = k - 1
    live = (tile >= 0) & (tile < n_live_ref[0])
    expert_of = lambda t: tile_e_ref[jnp.clip(t, 0, n_tiles - 1)]

    @pl.when((tile == 0) | ((tile > 0) & (expert_of(tile) != expert_of(tile - 1))))
    def _():
        w13b_ref[:, :d_ff] = w1_ref[0, 0].astype(BF16)
        w13b_ref[:, d_ff:] = w3_ref[0, 0].astype(BF16)

    @pl.when(live)
    def _():
        ab = jnp.dot(x_ref[...], w13b_ref[...], preferred_element_type=F32)
        a, b = ab[:, :d_ff], ab[:, d_ff:]
        g = (a * jax.nn.sigmoid(a) * b).astype(BF16)
        o_ref[...] = jnp.dot(g, w2b_ref[...], preferred_element_type=F32).astype(BF16)

    @pl.when((tile >= 0) & jnp.logical_not(live))
    def _():
        o_ref[...] = jnp.zeros(o_ref.shape, BF16)

    @pl.when((k == 0) | ((k < n_tiles) & (expert_of(k) != expert_of(tile))))
    def _():
        w2b_ref[...] = w2_ref[0, 0].astype(BF16)


def _combine_kernel(wstart_ref, base_ref, x_ref, mod_ref, rank_ref, w_ref, fg_ref, *rest, final):
    n_win = COMBINE_CHUNKS * N_EXPERTS
    y_refs, o_ref = rest[:n_win], rest[n_win]
    i = pl.program_id(0)
    lane = lax.broadcasted_iota(jnp.int32, (CNT_CHUNK, WINDOW), 1)
    gate = mod_ref[0][:, 2 * D_MODEL:]
    for s in range(COMBINE_CHUNKS):
        rows = slice(s * CNT_CHUNK, (s + 1) * CNT_CHUNK)
        rank = rank_ref[rows, :]
        wts = w_ref[rows, :]
        acc = jnp.zeros((CNT_CHUNK, D_MODEL), F32)
        for e in range(N_EXPERTS):
            w = s * N_EXPERTS + e
            rk = rank[:, e:e + 1]
            slot = rk + base_ref[e].astype(F32)
            window_slot = (wstart_ref[i * n_win + w] + lane).astype(F32)
            hit = (rk >= 0.0) & (slot == window_slot)
            p = jnp.where(hit, wts[:, e:e + 1], 0.0).astype(BF16)
            acc = acc + jnp.dot(p, y_refs[w][...], preferred_element_type=F32)
        y = x_ref[rows, :] + gate * acc
        if final:
            ms = jnp.mean(y * y, axis=-1, keepdims=True)
            y = y * lax.rsqrt(ms + EPS) * fg_ref[...]
        o_ref[rows, :] = y


def _route_metadata(cnt_end, n_tok):
    i32 = jnp.int32
    cnt_end = cnt_end[:, :N_EXPERTS]
    cnt_start = jnp.concatenate([jnp.zeros((1, N_EXPERTS), i32), cnt_end[:-1]], axis=0)
    n_chunks = cnt_end.shape[0]
    total = cnt_end[-1]
    n_tiles = (total + SLOT_TILE - 1) // SLOT_TILE
    tile_end = jnp.cumsum(n_tiles)
    tile_start = tile_end - n_tiles
    base = (tile_start * SLOT_TILE).astype(i32)
    max_tiles = 2 * n_tok // SLOT_TILE + N_EXPERTS
    i = jnp.arange(max_tiles, dtype=i32)
    tile_e = jnp.minimum(jnp.sum(i[:, None] >= tile_end[None, :], axis=1), N_EXPERTS - 1)
    subs = SLOT_TILE // GATHER_SLOTS
    q = jnp.arange(max_tiles * subs, dtype=i32)
    q_e = tile_e[q // subs]
    ls0 = (q // subs - tile_start[q_e]) * SLOT_TILE + (q % subs) * GATHER_SLOTS
    live = (q // subs < tile_end[-1]) & (ls0 < total[q_e])
    ends = cnt_end[:, q_e].T
    starts = cnt_start[:, q_e].T
    c_lo = jnp.sum(ends <= ls0[:, None], axis=1)
    c_hi = jnp.sum(starts < (ls0 + GATHER_SLOTS)[:, None], axis=1) - 1
    n_g = jnp.where(live, (c_hi - c_lo + GATHER_CHUNKS) // GATHER_CHUNKS, 0)
    c_lo = jnp.minimum(c_lo, n_chunks - 1)
    n_slots = max_tiles * SLOT_TILE
    wstart = jnp.minimum((base[None, :] + cnt_start) // ROW_ALIGN * ROW_ALIGN, n_slots - WINDOW)
    as_i32 = lambda a: a.astype(i32)
    return (as_i32(tile_e), as_i32(tile_end[-1:]), as_i32(ls0), as_i32(c_lo), as_i32(n_g), base,
            as_i32(wstart).reshape(-1))


def _moe_ffn(x, mods, layer, g, wr, br, w1, w3, w2, final_gain, final):
    batch, seq, _ = x.shape
    n_tok = batch * seq
    d_ff = w1.shape[3]
    j = layer // 2
    xt = x.reshape(n_tok, D_MODEL)
    params = lambda: pltpu.CompilerParams(
        dimension_semantics=("arbitrary",), vmem_limit_bytes=VMEM_LIMIT)

    route_tiles_per_seq = seq // ROUTE_TILE
    h, rank, wts, rank_row, cnt_end = pl.pallas_call(
        _router_kernel,
        grid=(n_tok // ROUTE_TILE,),
        in_specs=[
            pl.BlockSpec((ROUTE_TILE, D_MODEL), lambda i: (i, 0)),
            pl.BlockSpec((1, 1, 3 * D_MODEL),
                         lambda i: (layer * PAD_B + i // route_tiles_per_seq, 0, 0)),
            _layer_spec((1, D_MODEL), layer),
            _layer_spec((D_MODEL, LANES), j),
            _layer_spec((1, LANES), j),
        ],
        out_specs=[
            pl.BlockSpec((ROUTE_TILE, D_MODEL), lambda i: (i, 0)),
            pl.BlockSpec((ROUTE_TILE, LANES), lambda i: (i, 0)),
            pl.BlockSpec((ROUTE_TILE, LANES), lambda i: (i, 0)),
            pl.BlockSpec((N_EXPERTS, ROUTE_TILE), lambda i: (0, i)),
            pl.BlockSpec((ROUTE_TILE // CNT_CHUNK, LANES), lambda i: (i, 0)),
        ],
        out_shape=[
            jax.ShapeDtypeStruct((n_tok, D_MODEL), BF16),
            jax.ShapeDtypeStruct((n_tok, LANES), F32),
            jax.ShapeDtypeStruct((n_tok, LANES), F32),
            jax.ShapeDtypeStruct((N_EXPERTS, n_tok), F32),
            jax.ShapeDtypeStruct((n_tok // CNT_CHUNK, LANES), jnp.int32),
        ],
        scratch_shapes=[pltpu.VMEM((SUBLANES, LANES), F32)],
        compiler_params=params(),
        name="moe_router",
    )(xt, mods, g, wr, br)

    tile_e, n_live, ls0, c_lo, n_g, base, wstart = _route_metadata(cnt_end, n_tok)
    max_tiles = tile_e.shape[0]
    n_slots = max_tiles * SLOT_TILE

    x_sorted = pl.pallas_call(
        _dispatch_kernel,
        grid_spec=pltpu.PrefetchScalarGridSpec(
            num_scalar_prefetch=4,
            grid=(max_tiles,),
            in_specs=[pl.BlockSpec(memory_space=pltpu.VMEM),
                      pl.BlockSpec(memory_space=pltpu.VMEM)],
            out_specs=pl.BlockSpec((SLOT_TILE, D_MODEL), lambda i, *_: (i, 0)),
        ),
        out_shape=jax.ShapeDtypeStruct((n_slots, D_MODEL), BF16),
        compiler_params=params(),
        name="moe_dispatch",
    )(tile_e, ls0, c_lo, n_g, h, rank_row.reshape(N_EXPERTS, n_tok // CNT_CHUNK, CNT_CHUNK))

    this_expert = lambda k, te, nl: (j, te[jnp.maximum(k - 1, 0)], 0, 0)
    next_expert = lambda k, te, nl: (j, te[jnp.minimum(k, max_tiles - 1)], 0, 0)
    y_sorted = pl.pallas_call(
        _expert_kernel,
        grid_spec=pltpu.PrefetchScalarGridSpec(
            num_scalar_prefetch=2,
            grid=(max_tiles + 1,),
            in_specs=[
                pl.BlockSpec((SLOT_TILE, D_MODEL), lambda k, te, nl: (jnp.maximum(k - 1, 0), 0)),
                pl.BlockSpec((1, 1, D_MODEL, d_ff), this_expert),
                pl.BlockSpec((1, 1, D_MODEL, d_ff), this_expert),
                pl.BlockSpec((1, 1, d_ff, D_MODEL), next_expert),
            ],
            out_specs=pl.BlockSpec((SLOT_TILE, D_MODEL),
                                   lambda k, te, nl: (jnp.maximum(k - 1, 0), 0)),
            scratch_shapes=[pltpu.VMEM((D_MODEL, 2 * d_ff), BF16),
                            pltpu.VMEM((d_ff, D_MODEL), BF16)],
        ),
        out_shape=jax.ShapeDtypeStruct((n_slots, D_MODEL), BF16),
        compiler_params=params(),
        name="moe_experts",
    )(tile_e, n_live, x_sorted, w1, w3, w2)

    rows = COMBINE_CHUNKS * CNT_CHUNK
    steps_per_seq = seq // rows
    n_win = COMBINE_CHUNKS * N_EXPERTS
    window_spec = lambda w: pl.BlockSpec(
        (pl.Element(WINDOW), pl.Element(D_MODEL)),
        lambda i, ws, bs: (pl.multiple_of(ws[i * n_win + w], ROW_ALIGN), 0))
    out = pl.pallas_call(
        functools.partial(_combine_kernel, final=final),
        grid_spec=pltpu.PrefetchScalarGridSpec(
            num_scalar_prefetch=2,
            grid=(n_tok // rows,),
            in_specs=[
                pl.BlockSpec((rows, D_MODEL), lambda i, ws, bs: (i, 0)),
                pl.BlockSpec((1, 1, 3 * D_MODEL),
                             lambda i, ws, bs: (layer * PAD_B + i // steps_per_seq, 0, 0)),
                pl.BlockSpec((rows, LANES), lambda i, ws, bs: (i, 0)),
                pl.BlockSpec((rows, LANES), lambda i, ws, bs: (i, 0)),
                pl.BlockSpec((1, D_MODEL), lambda i, ws, bs: (0, 0)),
            ] + [window_spec(w) for w in range(n_win)],
            out_specs=pl.BlockSpec((rows, D_MODEL), lambda i, ws, bs: (i, 0)),
        ),
        out_shape=jax.ShapeDtypeStruct((n_tok, D_MODEL), F32),
        compiler_params=params(),
        name="moe_combine",
    )(wstart, base, xt, mods, rank, wts, final_gain, *([y_sorted] * n_win))
    return out.reshape(x.shape)


def _block_diag_halves(w):
    depth, heads, hd, _ = w.shape
    per_tile = MXU_DIM // hd
    w = w.reshape(depth, heads // per_tile, per_tile, hd, hd)
    eye = jnp.eye(per_tile, dtype=w.dtype)
    bd = jnp.einsum("dtiab,ij->dtiajb", w, eye)
    return bd.reshape(depth, heads // per_tile, MXU_DIM, MXU_DIM)


def kernel(x, c, mix_norm, mix_mod_w, mix_mod_b, w_in, lru_conv_w, lru_conv_b, lru_wa, lru_ba,
           lru_wi, lru_bi, lru_lambda, sc_conv_w, gn_lru, gn_sc, w_out, ffn_norm, ffn_mod_w,
           ffn_mod_b, dense_w1, dense_w3, dense_w2, router_w, router_b, exp_w1, exp_w3, exp_w2,
           final_norm):
    depth = w_in.shape[0]
    batch = x.shape[0]
    c_pad = jnp.pad(c, ((0, PAD_B - batch), (0, 0)))
    mix_mods = _modulation(c_pad, mix_mod_w, mix_mod_b).reshape(depth * PAD_B, 1, 3 * D_MODEL)
    ffn_mods = _modulation(c_pad, ffn_mod_w, ffn_mod_b).reshape(depth * PAD_B, 1, 3 * D_MODEL)

    row = lambda p: p.reshape(depth, 1, p.shape[-1])
    wg = jnp.concatenate([_block_diag_halves(lru_wa), _block_diag_halves(lru_wi)],
                         axis=-1).astype(BF16)
    bg = jnp.stack([lru_ba.reshape(depth, D_LRU), lru_bi.reshape(depth, D_LRU)], axis=1)
    head = jnp.arange(MXU_DIM) // HEAD_DIM
    gmat = jnp.where(head[:, None] == head[None, :], 1.0 / HEAD_DIM, 0.0).astype(BF16)
    w_in_b = w_in.astype(BF16)
    w_out_b = w_out.astype(BF16)
    dense_w1_b, dense_w3_b, dense_w2_b = (w.astype(BF16) for w in (dense_w1, dense_w3, dense_w2))
    n_moe = router_w.shape[0]
    wr = jnp.pad(router_w, ((0, 0), (0, 0), (0, LANES - N_EXPERTS)))
    br = jnp.pad(router_b, ((0, 0), (0, LANES - N_EXPERTS)),
                 constant_values=-jnp.inf).reshape(n_moe, 1, LANES)
    final_gain = final_norm.reshape(1, D_MODEL)
    mix_g, ffn_g = row(mix_norm), row(ffn_norm)
    lcb, lam, gnl, gns = row(lru_conv_b), row(lru_lambda), row(gn_lru), row(gn_sc)

    for l in range(depth):
        x = _mixer(x, mix_mods, l, mix_g, w_in_b, lru_conv_w, lcb, wg, bg, lam, sc_conv_w, gnl,
                   gns, gmat, w_out_b)
        if l % 2 == 0:
            x = _dense_ffn(x, ffn_mods, l, ffn_g, dense_w1_b, dense_w3_b, dense_w2_b)
        else:
            x = _moe_ffn(x, ffn_mods, l, ffn_g, wr, br, exp_w1, exp_w3, exp_w2, final_gain,
                         final=(l == depth - 1))
    if depth % 2 == 1:
        raise NotImplementedError("final norm is fused into the last MoE layer")
    return x
```

```python
import functools

import jax
import jax.numpy as jnp
from jax import lax
from jax.experimental import pallas as pl
from jax.experimental.pallas import tpu as pltpu

F32 = jnp.float32
BF16 = jnp.bfloat16

D_MODEL = 1024
D_LRU = 512
D_SC = 512
D_IN = 2 * D_LRU + 3 * D_SC
HEAD_DIM = 64
LRU_CONV = 4
SC_CONV = 3
LRU_C = 8.0
N_EXPERTS = 8
EPS = 1e-6

SUBLANES = 8
LANES = 128
MXU_DIM = 256
TS = SUBLANES ** 3
GROUPS = TS // SUBLANES
PAD_B = 8
VMEM_LIMIT = 56 * 1024 * 1024


def _mod_norm(x, g, scale, shift):
    ms = jnp.mean(x * x, axis=-1, keepdims=True)
    return x * lax.rsqrt(ms + EPS) * (g * (1.0 + scale)) + shift


def _split_mod(mod):
    return mod[:, :D_MODEL], mod[:, D_MODEL:2 * D_MODEL], mod[:, 2 * D_MODEL:]


def _layer_spec(shape, layer):
    zeros = (0,) * len(shape)
    return pl.BlockSpec((1, *shape), lambda *_: (layer, *zeros), pipeline_mode=pl.Buffered(1))


def _mod_kernel(c_ref, w_ref, b_ref, o_ref):
    c = c_ref[...]
    s = (c * jax.nn.sigmoid(c)).astype(BF16)
    o_ref[0] = jnp.dot(s, w_ref[0].astype(BF16), preferred_element_type=F32) + b_ref[0]


def _modulation(c_pad, w, b):
    depth = w.shape[0]
    n_col = 3 * D_MODEL // D_MODEL
    return pl.pallas_call(
        _mod_kernel,
        grid=(depth, n_col),
        in_specs=[
            pl.BlockSpec((PAD_B, D_MODEL), lambda l, j: (0, 0)),
            pl.BlockSpec((1, D_MODEL, D_MODEL), lambda l, j: (l, 0, j)),
            pl.BlockSpec((1, 1, D_MODEL), lambda l, j: (l, 0, j)),
        ],
        out_specs=pl.BlockSpec((1, PAD_B, D_MODEL), lambda l, j: (l, 0, j)),
        out_shape=jax.ShapeDtypeStruct((depth, PAD_B, 3 * D_MODEL), F32),
        compiler_params=pltpu.CompilerParams(
            dimension_semantics=("arbitrary", "arbitrary"), vmem_limit_bytes=VMEM_LIMIT),
        name="adaln_mod",
    )(c_pad, w, b.reshape(depth, 1, 3 * D_MODEL))


def _scan8(a, u, row_in_group):
    for d in (1, 2, 4):
        keep = row_in_group >= d
        a_sh = jnp.where(keep, pltpu.roll(a, d, axis=0), 1.0)
        u_sh = jnp.where(keep, pltpu.roll(u, d, axis=0), 0.0)
        u = a * u_sh + u
        a = a * a_sh
    return a, u


def _row_in_group(rows, cols):
    return lax.broadcasted_iota(jnp.int32, (rows, cols), 0) % SUBLANES


def _shift_rows_down(x, first_row):
    row = lax.broadcasted_iota(jnp.int32, x.shape, 0)
    return jnp.where(row == 0, first_row, pltpu.roll(x, 1, axis=0))


def _group_rms(y, gmat_ref, gain):
    y2 = (y * y).astype(BF16)
    ms = jnp.concatenate(
        [jnp.dot(y2[:, :MXU_DIM], gmat_ref[...], preferred_element_type=F32),
         jnp.dot(y2[:, MXU_DIM:], gmat_ref[...], preferred_element_type=F32)], axis=-1)
    return y * lax.rsqrt(ms + EPS) * gain


def _mixer_kernel(x_ref, mod_ref, g_ref, win_ref, lcw_ref, lcb_ref, wg_ref, bg_ref, lam_ref,
                  scw_ref, gnl_ref, gns_ref, gmat_ref, wout_ref, o_ref,
                  xl_buf, v_buf, a0_ref, u0_ref, a1_ref, u1_ref, b64_ref, carry_ref):
    @pl.when(pl.program_id(1) == 0)
    def _():
        xl_buf[...] = jnp.zeros(xl_buf.shape, F32)
        v_buf[...] = jnp.zeros(v_buf.shape, F32)
        carry_ref[...] = jnp.zeros(carry_ref.shape, F32)

    x = jnp.concatenate([x_ref[0, 0, :, s * D_MODEL:(s + 1) * D_MODEL] for s in range(SUBLANES)],
                        axis=0)
    out = _mixer_tile(
        x, mod_ref[0], g_ref, win_ref, lcw_ref, lcb_ref, wg_ref, bg_ref, lam_ref, scw_ref,
        gnl_ref, gns_ref, gmat_ref, wout_ref, xl_buf, v_buf, a0_ref, u0_ref, a1_ref, u1_ref,
        b64_ref, carry_ref)
    for s in range(SUBLANES):
        o_ref[0, 0, :, s * D_MODEL:(s + 1) * D_MODEL] = out[s * GROUPS:(s + 1) * GROUPS, :]


def _slabs(y):
    return [y[s * GROUPS:(s + 1) * GROUPS, :] for s in range(SUBLANES)]


def _causal_conv(y, w_ref, hist_ref):
    taps = w_ref.shape[1]
    slabs = _slabs(y)
    early = {j - SUBLANES: _shift_rows_down(slabs[j], hist_ref[j:j + 1, :])
             for j in range(SUBLANES - taps + 1, SUBLANES)}
    at = lambda j: slabs[j] if j >= 0 else early[j]
    out = []
    for s in range(SUBLANES):
        acc = w_ref[0, 0:1, :] * at(s - (taps - 1))
        for k in range(1, taps):
            acc = acc + w_ref[0, k:k + 1, :] * at(s - (taps - 1) + k)
        out.append(acc)
    for j in range(SUBLANES - taps + 1, SUBLANES):
        hist_ref[j:j + 1, :] = slabs[j][GROUPS - 1:, :]
    return jnp.concatenate(out, axis=0)


def _mixer_tile(x, mod, g_ref, win_ref, lcw_ref, lcb_ref, wg_ref, bg_ref, lam_ref, scw_ref,
                gnl_ref, gns_ref, gmat_ref, wout_ref, xl_buf, v_buf, a0_ref, u0_ref, a1_ref,
                u1_ref, b64_ref, carry_ref):
    n2 = GROUPS // SUBLANES
    shift, scale, gate = _split_mod(mod)
    h = _mod_norm(x, g_ref[0], scale, shift).astype(BF16)
    def in_proj(lo, hi):
        return jnp.dot(h, win_ref[0, :, lo:hi], preferred_element_type=F32)

    xc = _causal_conv(in_proj(0, D_LRU), lcw_ref, xl_buf) + lcb_ref[0]

    cx_sc = in_proj(2 * D_LRU + D_SC, D_IN)
    xcb = xc.astype(BF16)
    g0 = jnp.dot(xcb[:, :MXU_DIM], wg_ref[0, 0], preferred_element_type=F32)
    g1 = jnp.dot(xcb[:, MXU_DIM:], wg_ref[0, 1], preferred_element_type=F32)
    cv = _causal_conv(cx_sc[:, :D_SC] * cx_sc[:, D_SC:], scw_ref, v_buf)

    b_sc = in_proj(2 * D_LRU, 2 * D_LRU + D_SC)
    r = jax.nn.sigmoid(jnp.concatenate([g0[:, :MXU_DIM], g1[:, :MXU_DIM]], axis=-1)
                       + bg_ref[0, 0:1, :])
    i = jax.nn.sigmoid(jnp.concatenate([g0[:, MXU_DIM:], g1[:, MXU_DIM:]], axis=-1)
                       + bg_ref[0, 1:2, :])
    log_a = r * ((-LRU_C) * jax.nn.softplus(-lam_ref[0]))
    a = jnp.exp(log_a)
    uin = jnp.sqrt(-jnp.tanh(log_a) * (1.0 + a * a)) * (i * xc)
    yn_sc = _group_rms(b_sc * cv, gmat_ref, gns_ref[0]).astype(BF16)
    out_sc = jnp.dot(yn_sc, wout_ref[0, D_LRU:, :], preferred_element_type=F32)

    gate_lru = jax.nn.gelu(in_proj(D_LRU, 2 * D_LRU))

    last = SUBLANES - 1
    hs_blocks = []
    for lb in range(D_LRU // LANES):
        lanes = slice(lb * LANES, (lb + 1) * LANES)
        h_prev = carry_ref[lb, last:last + 1, :]
        acc_a = acc_u = None
        for s, (a_s, u_s) in enumerate(zip(_slabs(a[:, lanes]), _slabs(uin[:, lanes]))):
            if s == 0:
                acc_a, acc_u = a_s, u_s
            else:
                acc_u = a_s * acc_u + u_s
                acc_a = a_s * acc_a
            a0_ref[lb, s * GROUPS:(s + 1) * GROUPS, :] = acc_a
            u0_ref[lb, s * GROUPS:(s + 1) * GROUPS, :] = acc_u
        a1, u1 = _scan8(acc_a, acc_u, _row_in_group(GROUPS, LANES))
        a1_ref[lb] = a1
        u1_ref[lb] = u1
        a2, u2 = _scan8(a1_ref[lb, pl.ds(last, n2, stride=SUBLANES), :],
                        u1_ref[lb, pl.ds(last, n2, stride=SUBLANES), :],
                        _row_in_group(n2, LANES))
        h2 = u2 + a2 * h_prev
        carry_ref[lb] = h2
        b64_ref[lb] = _shift_rows_down(h2, h_prev)
        before64 = jnp.concatenate(
            [jnp.broadcast_to(b64_ref[lb, j:j + 1, :], (SUBLANES, LANES)) for j in range(n2)],
            axis=0)
        h1 = u1 + a1 * before64
        before8 = _shift_rows_down(h1, h_prev)
        hs_blocks.append(jnp.concatenate(
            [u0_ref[lb, s * GROUPS:(s + 1) * GROUPS, :]
             + a0_ref[lb, s * GROUPS:(s + 1) * GROUPS, :] * before8 for s in range(SUBLANES)],
            axis=0))
    hs = jnp.concatenate(hs_blocks, axis=-1)

    yn_lru = _group_rms(gate_lru * hs, gmat_ref, gnl_ref[0]).astype(BF16)
    out = out_sc + jnp.dot(yn_lru, wout_ref[0, :D_LRU, :], preferred_element_type=F32)
    return x + gate * out


def _mixer(x, mods, layer, g, w_in, lcw, lcb, wg, bg, lam, scw, gnl, gns, gmat, w_out):
    batch, seq, _ = x.shape
    ts = TS
    nlb = D_LRU // LANES
    per_layer = lambda shape: _layer_spec(shape, layer)
    tiled = (batch, seq // ts, GROUPS, SUBLANES * D_MODEL)
    tile_spec = pl.BlockSpec((1, 1, GROUPS, SUBLANES * D_MODEL), lambda b, t: (b, t, 0, 0))
    return pl.pallas_call(
        _mixer_kernel,
        grid=(batch, seq // ts),
        in_specs=[
            tile_spec,
            pl.BlockSpec((1, 1, 3 * D_MODEL), lambda b, t: (layer * PAD_B + b, 0, 0)),
            per_layer((1, D_MODEL)),
            per_layer((D_MODEL, D_IN)),
            per_layer((LRU_CONV, D_LRU)),
            per_layer((1, D_LRU)),
            per_layer((2, MXU_DIM, 2 * MXU_DIM)),
            per_layer((2, D_LRU)),
            per_layer((1, D_LRU)),
            per_layer((SC_CONV, D_SC)),
            per_layer((1, D_LRU)),
            per_layer((1, D_SC)),
            pl.BlockSpec((MXU_DIM, MXU_DIM), lambda b, t: (0, 0)),
            per_layer((D_LRU + D_SC, D_MODEL)),
        ],
        out_specs=tile_spec,
        out_shape=jax.ShapeDtypeStruct(tiled, F32),
        scratch_shapes=[
            pltpu.VMEM((SUBLANES, D_LRU), F32),
            pltpu.VMEM((SUBLANES, D_SC), F32),
            pltpu.VMEM((nlb, ts, LANES), F32),
            pltpu.VMEM((nlb, ts, LANES), F32),
            pltpu.VMEM((nlb, GROUPS, LANES), F32),
            pltpu.VMEM((nlb, GROUPS, LANES), F32),
            pltpu.VMEM((nlb, SUBLANES, LANES), F32),
            pltpu.VMEM((nlb, SUBLANES, LANES), F32),
        ],
        compiler_params=pltpu.CompilerParams(
            dimension_semantics=("arbitrary", "arbitrary"), vmem_limit_bytes=VMEM_LIMIT),
        name="mixer",
    )(x.reshape(tiled), mods, g, w_in, lcw, lcb, wg, bg, lam, scw, gnl, gns, gmat,
      w_out).reshape(x.shape)


def _swiglu(h, w1, w3, w2):
    a = jnp.dot(h, w1, preferred_element_type=F32)
    b = jnp.dot(h, w3, preferred_element_type=F32)
    g = (a * jax.nn.sigmoid(a) * b).astype(BF16)
    return jnp.dot(g, w2, preferred_element_type=F32)


def _mxu_halves(width):
    split = pl.cdiv(width // 2, MXU_DIM) * MXU_DIM
    return slice(0, split), slice(split, width)


def _dense_ffn_kernel(x_ref, mod_ref, g_ref, w1_ref, w3_ref, w2_ref, o_ref):
    x = x_ref[0]
    shift, scale, gate = _split_mod(mod_ref[0])
    h = _mod_norm(x, g_ref[0], scale, shift).astype(BF16)
    f = jnp.zeros(x.shape, F32)
    for cols in _mxu_halves(w1_ref.shape[2]):
        f = f + _swiglu(h, w1_ref[0, :, cols], w3_ref[0, :, cols], w2_ref[0, cols, :])
    o_ref[0] = x + gate * f


def _dense_ffn(x, mods, layer, g, w1, w3, w2, tm=1024):
    batch, seq, _ = x.shape
    d_ff = w1.shape[2]
    j = layer // 2
    return pl.pallas_call(
        _dense_ffn_kernel,
        grid=(batch, seq // tm),
        in_specs=[
            pl.BlockSpec((1, tm, D_MODEL), lambda b, t: (b, t, 0)),
            pl.BlockSpec((1, 1, 3 * D_MODEL), lambda b, t: (layer * PAD_B + b, 0, 0)),
            _layer_spec((1, D_MODEL), layer),
            _layer_spec((D_MODEL, d_ff), j),
            _layer_spec((D_MODEL, d_ff), j),
            _layer_spec((d_ff, D_MODEL), j),
        ],
        out_specs=pl.BlockSpec((1, tm, D_MODEL), lambda b, t: (b, t, 0)),
        out_shape=jax.ShapeDtypeStruct(x.shape, F32),
        compiler_params=pltpu.CompilerParams(
            dimension_semantics=("arbitrary", "arbitrary"), vmem_limit_bytes=VMEM_LIMIT),
        name="dense_ffn",
    )(x, mods, g, w1, w3, w2)


ROUTE_TILE = 1024
CNT_CHUNK = 128
COMBINE_CHUNKS = 4
SLOT_TILE = 512
GATHER_SLOTS = 128
GATHER_CHUNKS = 6
ROW_ALIGN = 16
WINDOW = CNT_CHUNK + ROW_ALIGN


def _top2(logits):
    lane = lax.broadcasted_iota(jnp.int32, logits.shape, 1)
    m1 = jnp.max(logits, axis=-1, keepdims=True)
    i1 = jnp.min(jnp.where(logits == m1, lane, LANES), axis=-1, keepdims=True)
    rest = jnp.where(lane == i1, -jnp.inf, logits)
    m2 = jnp.max(rest, axis=-1, keepdims=True)
    i2 = jnp.min(jnp.where(rest == m2, lane, LANES), axis=-1, keepdims=True)
    e2 = jnp.exp(m2 - m1)
    w_first = 1.0 / (1.0 + e2)
    w_second = e2 / (1.0 + e2)
    first = lane == i1
    second = lane == i2
    weights = jnp.where(first, w_first, 0.0) + jnp.where(second, w_second, 0.0)
    return weights, first | second


def _split_bf16(v):
    hi = v.astype(BF16)
    return hi, (v - hi.astype(F32)).astype(BF16)


def _router_kernel(x_ref, mod_ref, g_ref, wr_ref, br_ref,
                   h_ref, rank_ref, w_ref, rank_row_ref, cnt_ref, carry_ref):
    n = ROUTE_TILE
    n_chunks = n // CNT_CHUNK

    @pl.when(pl.program_id(0) == 0)
    def _():
        carry_ref[...] = jnp.zeros(carry_ref.shape, F32)

    shift, scale, _ = _split_mod(mod_ref[0])
    h = _mod_norm(x_ref[...], g_ref[0], scale, shift)
    h_hi, h_lo = _split_bf16(h)
    w_hi, w_lo = _split_bf16(wr_ref[0])
    both = jnp.dot(h_hi, jnp.concatenate([w_hi, w_lo], axis=1), preferred_element_type=F32)
    logits = (both[:, :LANES] + both[:, LANES:]
              + jnp.dot(h_lo, w_hi, preferred_element_type=F32)) + br_ref[0]
    weights, mask = _top2(logits)
    m = jnp.where(mask, 1.0, 0.0).astype(BF16)
    before = carry_ref[0:1, :]
    chunk = lax.broadcasted_iota(jnp.int32, (n_chunks, n), 0)
    tok = lax.broadcasted_iota(jnp.int32, (n_chunks, n), 1)
    in_earlier_chunk = jnp.where(tok < chunk * CNT_CHUNK, 1.0, 0.0).astype(BF16)
    upto_chunk_end = jnp.where(tok < (chunk + 1) * CNT_CHUNK, 1.0, 0.0).astype(BF16)
    chunk_start = jnp.dot(in_earlier_chunk, m, preferred_element_type=F32) + before
    cnt = jnp.dot(upto_chunk_end, m, preferred_element_type=F32) + before
    row = lax.broadcasted_iota(jnp.int32, (CNT_CHUNK, CNT_CHUNK), 0)
    col = lax.broadcasted_iota(jnp.int32, (CNT_CHUNK, CNT_CHUNK), 1)
    strictly_lower = jnp.where(row > col, 1.0, 0.0).astype(BF16)
    in_chunk = jnp.concatenate(
        [jnp.dot(strictly_lower, m[j * CNT_CHUNK:(j + 1) * CNT_CHUNK, :],
                 preferred_element_type=F32) + chunk_start[j:j + 1, :]
         for j in range(n_chunks)], axis=0)
    rank = jnp.where(mask, in_chunk, -1.0)
    carry_ref[...] = jnp.broadcast_to(cnt[n_chunks - 1:, :], carry_ref.shape)
    h_ref[...] = h_hi
    rank_ref[...] = rank
    w_ref[...] = weights
    rank_row_ref[...] = rank.T[:N_EXPERTS, :]
    cnt_ref[...] = cnt.astype(jnp.int32)


def _dispatch_kernel(tile_e_ref, ls0_ref, c_lo_ref, n_g_ref, h_ref, rank_ref, o_ref):
    i = pl.program_id(0)
    e = tile_e_ref[i]
    n_chunks = rank_ref.shape[1]
    span = GATHER_CHUNKS * CNT_CHUNK
    subs = SLOT_TILE // GATHER_SLOTS

    def gather(sub, k):
        q = i * subs + sub
        slot = (ls0_ref[q]
                + lax.broadcasted_iota(jnp.int32, (GATHER_SLOTS, CNT_CHUNK), 0)).astype(F32)
        wanted = c_lo_ref[q] + k * GATHER_CHUNKS
        start = jnp.minimum(wanted, n_chunks - GATHER_CHUNKS)
        p = jnp.concatenate(
            [jnp.where((start + j >= wanted) & (rank_ref[e, pl.ds(start + j, 1), :] == slot),
                       1.0, 0.0) for j in range(GATHER_CHUNKS)], axis=1).astype(BF16)
        tokens = pl.ds(pl.multiple_of(start * CNT_CHUNK, CNT_CHUNK), span)
        return jnp.dot(p, h_ref[tokens, :], preferred_element_type=F32).astype(BF16)

    for sub in range(subs):
        o_ref[sub * GATHER_SLOTS:(sub + 1) * GATHER_SLOTS, :] = gather(sub, 0)
    for sub in range(subs):
        def more(k, carry, sub=sub):
            o_ref[sub * GATHER_SLOTS:(sub + 1) * GATHER_SLOTS, :] += gather(sub, k)
            return carry

        lax.fori_loop(1, n_g_ref[i * subs + sub], more, 0)


def _expert_kernel(tile_e_ref, n_live_ref, x_ref, w1_ref, w3_ref, w2_ref, o_ref,
                   w13b_ref, w2b_ref):
    k = pl.program_id(0)
    n_tiles = pl.num_programs(0) - 1
    d_ff = w2b_ref.shape[0]
    tile = k - 1
    live = (tile >= 0) & (tile < n_live_ref[0])
    expert_of = lambda t: tile_e_ref[jnp.clip(t, 0, n_tiles - 1)]

    @pl.when((tile == 0) | ((tile > 0) & (expert_of(tile) != expert_of(tile - 1))))
    def _():
        w13b_ref[:, :d_ff] = w1_ref[0, 0].astype(BF16)
        w13b_ref[:, d_ff:] = w3_ref[0, 0].astype(BF16)

    @pl.when(live)
    def _():
        ab = jnp.dot(x_ref[...], w13b_ref[...], preferred_element_type=F32)
        a, b = ab[:, :d_ff], ab[:, d_ff:]
        g = (a * jax.nn.sigmoid(a) * b).astype(BF16)
        o_ref[...] = jnp.dot(g, w2b_ref[...], preferred_element_type=F32).astype(BF16)

    @pl.when((tile >= 0) & jnp.logical_not(live))
    def _():
        o_ref[...] = jnp.zeros(o_ref.shape, BF16)

    @pl.when((k == 0) | ((k < n_tiles) & (expert_of(k) != expert_of(tile))))
    def _():
        w2b_ref[...] = w2_ref[0, 0].astype(BF16)


def _combine_kernel(wstart_ref, base_ref, x_ref, mod_ref, rank_ref, w_ref, fg_ref, *rest, final):
    n_win = COMBINE_CHUNKS * N_EXPERTS
    y_refs, o_ref = rest[:n_win], rest[n_win]
    i = pl.program_id(0)
    lane = lax.broadcasted_iota(jnp.int32, (CNT_CHUNK, WINDOW), 1)
    gate = mod_ref[0][:, 2 * D_MODEL:]
    for s in range(COMBINE_CHUNKS):
        rows = slice(s * CNT_CHUNK, (s + 1) * CNT_CHUNK)
        rank = rank_ref[rows, :]
        wts = w_ref[rows, :]
        acc = jnp.zeros((CNT_CHUNK, D_MODEL), F32)
        for e in range(N_EXPERTS):
            w = s * N_EXPERTS + e
            rk = rank[:, e:e + 1]
            slot = rk + base_ref[e].astype(F32)
            window_slot = (wstart_ref[i * n_win + w] + lane).astype(F32)
            hit = (rk >= 0.0) & (slot == window_slot)
            p = jnp.where(hit, wts[:, e:e + 1], 0.0).astype(BF16)
            acc = acc + jnp.dot(p, y_refs[w][...], preferred_element_type=F32)
        y = x_ref[rows, :] + gate * acc
        if final:
            ms = jnp.mean(y * y, axis=-1, keepdims=True)
            y = y * lax.rsqrt(ms + EPS) * fg_ref[...]
        o_ref[rows, :] = y


def _route_metadata(cnt_end, n_tok):
    i32 = jnp.int32
    cnt_end = cnt_end[:, :N_EXPERTS]
    cnt_start = jnp.concatenate([jnp.zeros((1, N_EXPERTS), i32), cnt_end[:-1]], axis=0)
    n_chunks = cnt_end.shape[0]
    total = cnt_end[-1]
    n_tiles = (total + SLOT_TILE - 1) // SLOT_TILE
    tile_end = jnp.cumsum(n_tiles)
    tile_start = tile_end - n_tiles
    base = (tile_start * SLOT_TILE).astype(i32)
    max_tiles = 2 * n_tok // SLOT_TILE + N_EXPERTS
    i = jnp.arange(max_tiles, dtype=i32)
    tile_e = jnp.minimum(jnp.sum(i[:, None] >= tile_end[None, :], axis=1), N_EXPERTS - 1)
    subs = SLOT_TILE // GATHER_SLOTS
    q = jnp.arange(max_tiles * subs, dtype=i32)
    q_e = tile_e[q // subs]
    ls0 = (q // subs - tile_start[q_e]) * SLOT_TILE + (q % subs) * GATHER_SLOTS
    live = (q // subs < tile_end[-1]) & (ls0 < total[q_e])
    ends = cnt_end[:, q_e].T
    starts = cnt_start[:, q_e].T
    c_lo = jnp.sum(ends <= ls0[:, None], axis=1)
    c_hi = jnp.sum(starts < (ls0 + GATHER_SLOTS)[:, None], axis=1) - 1
    n_g = jnp.where(live, (c_hi - c_lo + GATHER_CHUNKS) // GATHER_CHUNKS, 0)
    c_lo = jnp.minimum(c_lo, n_chunks - 1)
    n_slots = max_tiles * SLOT_TILE
    wstart = jnp.minimum((base[None, :] + cnt_start) // ROW_ALIGN * ROW_ALIGN, n_slots - WINDOW)
    as_i32 = lambda a: a.astype(i32)
    return (as_i32(tile_e), as_i32(tile_end[-1:]), as_i32(ls0), as_i32(c_lo), as_i32(n_g), base,
            as_i32(wstart).reshape(-1))


def _moe_ffn(x, mods, layer, g, wr, br, w1, w3, w2, final_gain, final):
    batch, seq, _ = x.shape
    n_tok = batch * seq
    d_ff = w1.shape[3]
    j = layer // 2
    xt = x.reshape(n_tok, D_MODEL)
    params = lambda: pltpu.CompilerParams(
        dimension_semantics=("arbitrary",), vmem_limit_bytes=VMEM_LIMIT)

    route_tiles_per_seq = seq // ROUTE_TILE
    h, rank, wts, rank_row, cnt_end = pl.pallas_call(
        _router_kernel,
        grid=(n_tok // ROUTE_TILE,),
        in_specs=[
            pl.BlockSpec((ROUTE_TILE, D_MODEL), lambda i: (i, 0)),
            pl.BlockSpec((1, 1, 3 * D_MODEL),
                         lambda i: (layer * PAD_B + i // route_tiles_per_seq, 0, 0)),
            _layer_spec((1, D_MODEL), layer),
            _layer_spec((D_MODEL, LANES), j),
            _layer_spec((1, LANES), j),
        ],
        out_specs=[
            pl.BlockSpec((ROUTE_TILE, D_MODEL), lambda i: (i, 0)),
            pl.BlockSpec((ROUTE_TILE, LANES), lambda i: (i, 0)),
            pl.BlockSpec((ROUTE_TILE, LANES), lambda i: (i, 0)),
            pl.BlockSpec((N_EXPERTS, ROUTE_TILE), lambda i: (0, i)),
            pl.BlockSpec((ROUTE_TILE // CNT_CHUNK, LANES), lambda i: (i, 0)),
        ],
        out_shape=[
            jax.ShapeDtypeStruct((n_tok, D_MODEL), BF16),
            jax.ShapeDtypeStruct((n_tok, LANES), F32),
            jax.ShapeDtypeStruct((n_tok, LANES), F32),
            jax.ShapeDtypeStruct((N_EXPERTS, n_tok), F32),
            jax.ShapeDtypeStruct((n_tok // CNT_CHUNK, LANES), jnp.int32),
        ],
        scratch_shapes=[pltpu.VMEM((SUBLANES, LANES), F32)],
        compiler_params=params(),
        name="moe_router",
    )(xt, mods, g, wr, br)

    tile_e, n_live, ls0, c_lo, n_g, base, wstart = _route_metadata(cnt_end, n_tok)
    max_tiles = tile_e.shape[0]
    n_slots = max_tiles * SLOT_TILE

    x_sorted = pl.pallas_call(
        _dispatch_kernel,
        grid_spec=pltpu.PrefetchScalarGridSpec(
            num_scalar_prefetch=4,
            grid=(max_tiles,),
            in_specs=[pl.BlockSpec(memory_space=pltpu.VMEM),
                      pl.BlockSpec(memory_space=pltpu.VMEM)],
            out_specs=pl.BlockSpec((SLOT_TILE, D_MODEL), lambda i, *_: (i, 0)),
        ),
        out_shape=jax.ShapeDtypeStruct((n_slots, D_MODEL), BF16),
        compiler_params=params(),
        name="moe_dispatch",
    )(tile_e, ls0, c_lo, n_g, h, rank_row.reshape(N_EXPERTS, n_tok // CNT_CHUNK, CNT_CHUNK))

    this_expert = lambda k, te, nl: (j, te[jnp.maximum(k - 1, 0)], 0, 0)
    next_expert = lambda k, te, nl: (j, te[jnp.minimum(k, max_tiles - 1)], 0, 0)
    y_sorted = pl.pallas_call(
        _expert_kernel,
        grid_spec=pltpu.PrefetchScalarGridSpec(
            num_scalar_prefetch=2,
            grid=(max_tiles + 1,),
            in_specs=[
                pl.BlockSpec((SLOT_TILE, D_MODEL), lambda k, te, nl: (jnp.maximum(k - 1, 0), 0)),
                pl.BlockSpec((1, 1, D_MODEL, d_ff), this_expert),
                pl.BlockSpec((1, 1, D_MODEL, d_ff), this_expert),
                pl.BlockSpec((1, 1, d_ff, D_MODEL), next_expert),
            ],
            out_specs=pl.BlockSpec((SLOT_TILE, D_MODEL),
                                   lambda k, te, nl: (jnp.maximum(k - 1, 0), 0)),
            scratch_shapes=[pltpu.VMEM((D_MODEL, 2 * d_ff), BF16),
                            pltpu.VMEM((d_ff, D_MODEL), BF16)],
        ),
        out_shape=jax.ShapeDtypeStruct((n_slots, D_MODEL), BF16),
        compiler_params=params(),
        name="moe_experts",
    )(tile_e, n_live, x_sorted, w1, w3, w2)

    rows = COMBINE_CHUNKS * CNT_CHUNK
    steps_per_seq = seq // rows
    n_win = COMBINE_CHUNKS * N_EXPERTS
    window_spec = lambda w: pl.BlockSpec(
        (pl.Element(WINDOW), pl.Element(D_MODEL)),
        lambda i, ws, bs: (pl.multiple_of(ws[i * n_win + w], ROW_ALIGN), 0))
    out = pl.pallas_call(
        functools.partial(_combine_kernel, final=final),
        grid_spec=pltpu.PrefetchScalarGridSpec(
            num_scalar_prefetch=2,
            grid=(n_tok // rows,),
            in_specs=[
                pl.BlockSpec((rows, D_MODEL), lambda i, ws, bs: (i, 0)),
                pl.BlockSpec((1, 1, 3 * D_MODEL),
                             lambda i, ws, bs: (layer * PAD_B + i // steps_per_seq, 0, 0)),
                pl.BlockSpec((rows, LANES), lambda i, ws, bs: (i, 0)),
                pl.BlockSpec((rows, LANES), lambda i, ws, bs: (i, 0)),
                pl.BlockSpec((1, D_MODEL), lambda i, ws, bs: (0, 0)),
            ] + [window_spec(w) for w in range(n_win)],
            out_specs=pl.BlockSpec((rows, D_MODEL), lambda i, ws, bs: (i, 0)),
        ),
        out_shape=jax.ShapeDtypeStruct((n_tok, D_MODEL), F32),
        compiler_params=params(),
        name="moe_combine",
    )(wstart, base, xt, mods, rank, wts, final_gain, *([y_sorted] * n_win))
    return out.reshape(x.shape)


def _block_diag_halves(w):
    depth, heads, hd, _ = w.shape
    per_tile = MXU_DIM // hd
    w = w.reshape(depth, heads // per_tile, per_tile, hd, hd)
    eye = jnp.eye(per_tile, dtype=w.dtype)
    bd = jnp.einsum("dtiab,ij->dtiajb", w, eye)
    return bd.reshape(depth, heads // per_tile, MXU_DIM, MXU_DIM)


def kernel(x, c, mix_norm, mix_mod_w, mix_mod_b, w_in, lru_conv_w, lru_conv_b, lru_wa, lru_ba,
           lru_wi, lru_bi, lru_lambda, sc_conv_w, gn_lru, gn_sc, w_out, ffn_norm, ffn_mod_w,
           ffn_mod_b, dense_w1, dense_w3, dense_w2, router_w, router_b, exp_w1, exp_w3, exp_w2,
           final_norm):
    depth = w_in.shape[0]
    batch = x.shape[0]
    c_pad = jnp.pad(c, ((0, PAD_B - batch), (0, 0)))
    mix_mods = _modulation(c_pad, mix_mod_w, mix_mod_b).reshape(depth * PAD_B, 1, 3 * D_MODEL)
    ffn_mods = _modulation(c_pad, ffn_mod_w, ffn_mod_b).reshape(depth * PAD_B, 1, 3 * D_MODEL)

    row = lambda p: p.reshape(depth, 1, p.shape[-1])
    wg = jnp.concatenate([_block_diag_halves(lru_wa), _block_diag_halves(lru_wi)],
                         axis=-1).astype(BF16)
    bg = jnp.stack([lru_ba.reshape(depth, D_LRU), lru_bi.reshape(depth, D_LRU)], axis=1)
    head = jnp.arange(MXU_DIM) // HEAD_DIM
    gmat = jnp.where(head[:, None] == head[None, :], 1.0 / HEAD_DIM, 0.0).astype(BF16)
    w_in_b = w_in.astype(BF16)
    w_out_b = w_out.astype(BF16)
    dense_w1_b, dense_w3_b, dense_w2_b = (w.astype(BF16) for w in (dense_w1, dense_w3, dense_w2))
    n_moe = router_w.shape[0]
    wr = jnp.pad(router_w, ((0, 0), (0, 0), (0, LANES - N_EXPERTS)))
    br = jnp.pad(router_b, ((0, 0), (0, LANES - N_EXPERTS)),
                 constant_values=-jnp.inf).reshape(n_moe, 1, LANES)
    final_gain = final_norm.reshape(1, D_MODEL)
    mix_g, ffn_g = row(mix_norm), row(ffn_norm)
    lcb, lam, gnl, gns = row(lru_conv_b), row(lru_lambda), row(gn_lru), row(gn_sc)

    for l in range(depth):
        x = _mixer(x, mix_mods, l, mix_g, w_in_b, lru_conv_w, lcb, wg, bg, lam, sc_conv_w, gnl,
                   gns, gmat, w_out_b)
        if l % 2 == 0:
            x = _dense_ffn(x, ffn_mods, l, ffn_g, dense_w1_b, dense_w3_b, dense_w2_b)
        else:
            x = _moe_ffn(x, ffn_mods, l, ffn_g, wr, br, exp_w1, exp_w3, exp_w2, final_gain,
                         final=(l == depth - 1))
    if depth % 2 == 1:
        raise NotImplementedError("final norm is fused into the last MoE layer")
    return x
```

```python
import functools

import jax
import jax.numpy as jnp
from jax import lax
from jax.experimental import pallas as pl
from jax.experimental.pallas import tpu as pltpu

F32 = jnp.float32
BF16 = jnp.bfloat16

D_MODEL = 1024
D_LRU = 512
D_SC = 512
D_IN = 2 * D_LRU + 3 * D_SC
HEAD_DIM = 64
LRU_CONV = 4
SC_CONV = 3
LRU_C = 8.0
N_EXPERTS = 8
EPS = 1e-6

SUBLANES = 8
LANES = 128
MXU_DIM = 256
TS = SUBLANES ** 3
GROUPS = TS // SUBLANES
PAD_B = 8
VMEM_LIMIT = 56 * 1024 * 1024


def _mod_norm(x, g, scale, shift):
    ms = jnp.mean(x * x, axis=-1, keepdims=True)
    return x * lax.rsqrt(ms + EPS) * (g * (1.0 + scale)) + shift


def _split_mod(mod):
    return mod[:, :D_MODEL], mod[:, D_MODEL:2 * D_MODEL], mod[:, 2 * D_MODEL:]


def _layer_spec(shape, layer):
    zeros = (0,) * len(shape)
    return pl.BlockSpec((1, *shape), lambda *_: (layer, *zeros), pipeline_mode=pl.Buffered(1))


def _mod_kernel(c_ref, w_ref, b_ref, o_ref):
    c = c_ref[...]
    s = (c * jax.nn.sigmoid(c)).astype(BF16)
    o_ref[0] = jnp.dot(s, w_ref[0].astype(BF16), preferred_element_type=F32) + b_ref[0]


def _modulation(c_pad, w, b):
    depth = w.shape[0]
    n_col = 3 * D_MODEL // D_MODEL
    return pl.pallas_call(
        _mod_kernel,
        grid=(depth, n_col),
        in_specs=[
            pl.BlockSpec((PAD_B, D_MODEL), lambda l, j: (0, 0)),
            pl.BlockSpec((1, D_MODEL, D_MODEL), lambda l, j: (l, 0, j)),
            pl.BlockSpec((1, 1, D_MODEL), lambda l, j: (l, 0, j)),
        ],
        out_specs=pl.BlockSpec((1, PAD_B, D_MODEL), lambda l, j: (l, 0, j)),
        out_shape=jax.ShapeDtypeStruct((depth, PAD_B, 3 * D_MODEL), F32),
        compiler_params=pltpu.CompilerParams(
            dimension_semantics=("arbitrary", "arbitrary"), vmem_limit_bytes=VMEM_LIMIT),
        name="adaln_mod",
    )(c_pad, w, b.reshape(depth, 1, 3 * D_MODEL))


def _scan8(a, u, row_in_group):
    for d in (1, 2, 4):
        keep = row_in_group >= d
        a_sh = jnp.where(keep, pltpu.roll(a, d, axis=0), 1.0)
        u_sh = jnp.where(keep, pltpu.roll(u, d, axis=0), 0.0)
        u = a * u_sh + u
        a = a * a_sh
    return a, u


def _row_in_group(rows, cols):
    return lax.broadcasted_iota(jnp.int32, (rows, cols), 0) % SUBLANES


def _shift_rows_down(x, first_row):
    row = lax.broadcasted_iota(jnp.int32, x.shape, 0)
    return jnp.where(row == 0, first_row, pltpu.roll(x, 1, axis=0))


def _group_rms(y, gmat_ref, gain):
    y2 = (y * y).astype(BF16)
    ms = jnp.concatenate(
        [jnp.dot(y2[:, :MXU_DIM], gmat_ref[...], preferred_element_type=F32),
         jnp.dot(y2[:, MXU_DIM:], gmat_ref[...], preferred_element_type=F32)], axis=-1)
    return y * lax.rsqrt(ms + EPS) * gain


def _mixer_kernel(*refs, natural_in):
    n_x = D_MODEL // LANES if natural_in else 1
    x_refs, refs = refs[:n_x], refs[n_x:]
    (mod_ref, g_ref, win_ref, lcw_ref, lcb_ref, wg_ref, bg_ref, lam_ref, scw_ref, gnl_ref,
     gns_ref, gmat_ref, wout_ref, o_ref, xl_buf, v_buf, a0_ref, u0_ref, a1_ref, u1_ref, b64_ref,
     carry_ref) = refs

    @pl.when(pl.program_id(1) == 0)
    def _():
        xl_buf[...] = jnp.zeros(xl_buf.shape, F32)
        v_buf[...] = jnp.zeros(v_buf.shape, F32)
        carry_ref[...] = jnp.zeros(carry_ref.shape, F32)

    if natural_in:
        x = jnp.concatenate(
            [jnp.concatenate([r[0, pl.ds(s, GROUPS, stride=SUBLANES), :] for r in x_refs], axis=1)
             for s in range(SUBLANES)], axis=0)
    else:
        x = x_refs[0][0]
    o_ref[0] = _mixer_tile(
        x, mod_ref[0], g_ref, win_ref, lcw_ref, lcb_ref, wg_ref, bg_ref, lam_ref, scw_ref,
        gnl_ref, gns_ref, gmat_ref, wout_ref, xl_buf, v_buf, a0_ref, u0_ref, a1_ref, u1_ref,
        b64_ref, carry_ref)


def _slabs(y):
    return [y[s * GROUPS:(s + 1) * GROUPS, :] for s in range(SUBLANES)]


def _causal_conv(y, w_ref, hist_ref):
    taps = w_ref.shape[1]
    slabs = _slabs(y)
    early = {j - SUBLANES: _shift_rows_down(slabs[j], hist_ref[j:j + 1, :])
             for j in range(SUBLANES - taps + 1, SUBLANES)}
    at = lambda j: slabs[j] if j >= 0 else early[j]
    out = []
    for s in range(SUBLANES):
        acc = w_ref[0, 0:1, :] * at(s - (taps - 1))
        for k in range(1, taps):
            acc = acc + w_ref[0, k:k + 1, :] * at(s - (taps - 1) + k)
        out.append(acc)
    for j in range(SUBLANES - taps + 1, SUBLANES):
        hist_ref[j:j + 1, :] = slabs[j][GROUPS - 1:, :]
    return jnp.concatenate(out, axis=0)


def _mixer_tile(x, mod, g_ref, win_ref, lcw_ref, lcb_ref, wg_ref, bg_ref, lam_ref, scw_ref,
                gnl_ref, gns_ref, gmat_ref, wout_ref, xl_buf, v_buf, a0_ref, u0_ref, a1_ref,
                u1_ref, b64_ref, carry_ref):
    n2 = GROUPS // SUBLANES
    shift, scale, gate = _split_mod(mod)
    h = _mod_norm(x, g_ref[0], scale, shift).astype(BF16)
    def in_proj(lo, hi):
        return jnp.dot(h, win_ref[0, :, lo:hi], preferred_element_type=F32)

    xc = _causal_conv(in_proj(0, D_LRU), lcw_ref, xl_buf) + lcb_ref[0]

    cx_sc = in_proj(2 * D_LRU + D_SC, D_IN)
    xcb = xc.astype(BF16)
    g0 = jnp.dot(xcb[:, :MXU_DIM], wg_ref[0, 0], preferred_element_type=F32)
    g1 = jnp.dot(xcb[:, MXU_DIM:], wg_ref[0, 1], preferred_element_type=F32)
    cv = _causal_conv(cx_sc[:, :D_SC] * cx_sc[:, D_SC:], scw_ref, v_buf)

    b_sc = in_proj(2 * D_LRU, 2 * D_LRU + D_SC)
    r = jax.nn.sigmoid(jnp.concatenate([g0[:, :MXU_DIM], g1[:, :MXU_DIM]], axis=-1)
                       + bg_ref[0, 0:1, :])
    i = jax.nn.sigmoid(jnp.concatenate([g0[:, MXU_DIM:], g1[:, MXU_DIM:]], axis=-1)
                       + bg_ref[0, 1:2, :])
    log_a = r * ((-LRU_C) * jax.nn.softplus(-lam_ref[0]))
    a = jnp.exp(log_a)
    uin = jnp.sqrt(-jnp.tanh(log_a) * (1.0 + a * a)) * (i * xc)
    yn_sc = _group_rms(b_sc * cv, gmat_ref, gns_ref[0]).astype(BF16)
    out_sc = jnp.dot(yn_sc, wout_ref[0, D_LRU:, :], preferred_element_type=F32)

    gate_lru = jax.nn.gelu(in_proj(D_LRU, 2 * D_LRU))

    last = SUBLANES - 1
    hs_blocks = []
    for lb in range(D_LRU // LANES):
        lanes = slice(lb * LANES, (lb + 1) * LANES)
        h_prev = carry_ref[lb, last:last + 1, :]
        acc_a = acc_u = None
        for s, (a_s, u_s) in enumerate(zip(_slabs(a[:, lanes]), _slabs(uin[:, lanes]))):
            if s == 0:
                acc_a, acc_u = a_s, u_s
            else:
                acc_u = a_s * acc_u + u_s
                acc_a = a_s * acc_a
            a0_ref[lb, s * GROUPS:(s + 1) * GROUPS, :] = acc_a
            u0_ref[lb, s * GROUPS:(s + 1) * GROUPS, :] = acc_u
        a1, u1 = _scan8(acc_a, acc_u, _row_in_group(GROUPS, LANES))
        a1_ref[lb] = a1
        u1_ref[lb] = u1
        a2, u2 = _scan8(a1_ref[lb, pl.ds(last, n2, stride=SUBLANES), :],
                        u1_ref[lb, pl.ds(last, n2, stride=SUBLANES), :],
                        _row_in_group(n2, LANES))
        h2 = u2 + a2 * h_prev
        carry_ref[lb] = h2
        b64_ref[lb] = _shift_rows_down(h2, h_prev)
        before64 = jnp.concatenate(
            [jnp.broadcast_to(b64_ref[lb, j:j + 1, :], (SUBLANES, LANES)) for j in range(n2)],
            axis=0)
        h1 = u1 + a1 * before64
        before8 = _shift_rows_down(h1, h_prev)
        hs_blocks.append(jnp.concatenate(
            [u0_ref[lb, s * GROUPS:(s + 1) * GROUPS, :]
             + a0_ref[lb, s * GROUPS:(s + 1) * GROUPS, :] * before8 for s in range(SUBLANES)],
            axis=0))
    hs = jnp.concatenate(hs_blocks, axis=-1)

    yn_lru = _group_rms(gate_lru * hs, gmat_ref, gnl_ref[0]).astype(BF16)
    out = out_sc + jnp.dot(yn_lru, wout_ref[0, :D_LRU, :], preferred_element_type=F32)
    return x + gate * out


def _mixer(x, mods, layer, g, w_in, lcw, lcb, wg, bg, lam, scw, gnl, gns, gmat, w_out, natural_in):
    batch, seq, _ = x.shape
    ts = TS
    nlb = D_LRU // LANES
    per_layer = lambda shape: _layer_spec(shape, layer)
    tile_spec = pl.BlockSpec((1, ts, D_MODEL), lambda b, t: (b, t, 0))
    if natural_in:
        x_specs = [pl.BlockSpec((1, ts, LANES), lambda b, t, l=l: (b, t, l))
                   for l in range(D_MODEL // LANES)]
    else:
        x_specs = [tile_spec]
    return pl.pallas_call(
        functools.partial(_mixer_kernel, natural_in=natural_in),
        grid=(batch, seq // ts),
        in_specs=x_specs + [
            pl.BlockSpec((1, 1, 3 * D_MODEL), lambda b, t: (layer * PAD_B + b, 0, 0)),
            per_layer((1, D_MODEL)),
            per_layer((D_MODEL, D_IN)),
            per_layer((LRU_CONV, D_LRU)),
            per_layer((1, D_LRU)),
            per_layer((2, MXU_DIM, 2 * MXU_DIM)),
            per_layer((2, D_LRU)),
            per_layer((1, D_LRU)),
            per_layer((SC_CONV, D_SC)),
            per_layer((1, D_LRU)),
            per_layer((1, D_SC)),
            pl.BlockSpec((MXU_DIM, MXU_DIM), lambda b, t: (0, 0)),
            per_layer((D_LRU + D_SC, D_MODEL)),
        ],
        out_specs=tile_spec,
        out_shape=jax.ShapeDtypeStruct(x.shape, F32),
        scratch_shapes=[
            pltpu.VMEM((SUBLANES, D_LRU), F32),
            pltpu.VMEM((SUBLANES, D_SC), F32),
            pltpu.VMEM((nlb, ts, LANES), F32),
            pltpu.VMEM((nlb, ts, LANES), F32),
            pltpu.VMEM((nlb, GROUPS, LANES), F32),
            pltpu.VMEM((nlb, GROUPS, LANES), F32),
            pltpu.VMEM((nlb, SUBLANES, LANES), F32),
            pltpu.VMEM((nlb, SUBLANES, LANES), F32),
        ],
        compiler_params=pltpu.CompilerParams(
            dimension_semantics=("arbitrary", "arbitrary"), vmem_limit_bytes=VMEM_LIMIT),
        name="mixer",
    )(*([x] * len(x_specs)), mods, g, w_in, lcw, lcb, wg, bg, lam, scw, gnl, gns, gmat, w_out)


def _swiglu(h, w1, w3, w2):
    a = jnp.dot(h, w1, preferred_element_type=F32)
    b = jnp.dot(h, w3, preferred_element_type=F32)
    g = (a * jax.nn.sigmoid(a) * b).astype(BF16)
    return jnp.dot(g, w2, preferred_element_type=F32)


def _mxu_halves(width):
    split = pl.cdiv(width // 2, MXU_DIM) * MXU_DIM
    return slice(0, split), slice(split, width)


def _dense_ffn_kernel(x_ref, mod_ref, g_ref, w1_ref, w3_ref, w2_ref, o_ref):
    x = x_ref[0]
    shift, scale, gate = _split_mod(mod_ref[0])
    h = _mod_norm(x, g_ref[0], scale, shift).astype(BF16)
    f = jnp.zeros(x.shape, F32)
    for cols in _mxu_halves(w1_ref.shape[2]):
        f = f + _swiglu(h, w1_ref[0, :, cols], w3_ref[0, :, cols], w2_ref[0, cols, :])
    o_ref[0] = x + gate * f


def _dense_ffn(x, mods, layer, g, w1, w3, w2, tm=1024):
    batch, seq, _ = x.shape
    d_ff = w1.shape[2]
    j = layer // 2
    return pl.pallas_call(
        _dense_ffn_kernel,
        grid=(batch, seq // tm),
        in_specs=[
            pl.BlockSpec((1, tm, D_MODEL), lambda b, t: (b, t, 0)),
            pl.BlockSpec((1, 1, 3 * D_MODEL), lambda b, t: (layer * PAD_B + b, 0, 0)),
            _layer_spec((1, D_MODEL), layer),
            _layer_spec((D_MODEL, d_ff), j),
            _layer_spec((D_MODEL, d_ff), j),
            _layer_spec((d_ff, D_MODEL), j),
        ],
        out_specs=pl.BlockSpec((1, tm, D_MODEL), lambda b, t: (b, t, 0)),
        out_shape=jax.ShapeDtypeStruct(x.shape, F32),
        compiler_params=pltpu.CompilerParams(
            dimension_semantics=("arbitrary", "arbitrary"), vmem_limit_bytes=VMEM_LIMIT),
        name="dense_ffn",
    )(x, mods, g, w1, w3, w2)


ROUTE_TILE = 1024
CNT_CHUNK = 128
COMBINE_CHUNKS = 4
SLOT_TILE = 512
GATHER_SLOTS = 128
GATHER_CHUNKS = 6
ROW_ALIGN = 16
WINDOW = CNT_CHUNK + ROW_ALIGN


def _top2(logits):
    lane = lax.broadcasted_iota(jnp.int32, logits.shape, 1)
    m1 = jnp.max(logits, axis=-1, keepdims=True)
    i1 = jnp.min(jnp.where(logits == m1, lane, LANES), axis=-1, keepdims=True)
    rest = jnp.where(lane == i1, -jnp.inf, logits)
    m2 = jnp.max(rest, axis=-1, keepdims=True)
    i2 = jnp.min(jnp.where(rest == m2, lane, LANES), axis=-1, keepdims=True)
    e2 = jnp.exp(m2 - m1)
    w_first = 1.0 / (1.0 + e2)
    w_second = e2 / (1.0 + e2)
    first = lane == i1
    second = lane == i2
    weights = jnp.where(first, w_first, 0.0) + jnp.where(second, w_second, 0.0)
    return weights, first | second


def _split_bf16(v):
    hi = v.astype(BF16)
    return hi, (v - hi.astype(F32)).astype(BF16)


def _router_kernel(x_ref, mod_ref, g_ref, wr_ref, br_ref,
                   h_ref, rank_ref, w_ref, rank_row_ref, cnt_ref, carry_ref):
    n = ROUTE_TILE
    n_chunks = n // CNT_CHUNK

    @pl.when(pl.program_id(0) == 0)
    def _():
        carry_ref[...] = jnp.zeros(carry_ref.shape, F32)

    shift, scale, _ = _split_mod(mod_ref[0])
    h = _mod_norm(x_ref[...], g_ref[0], scale, shift)
    h_hi, h_lo = _split_bf16(h)
    w_hi, w_lo = _split_bf16(wr_ref[0])
    both = jnp.dot(h_hi, jnp.concatenate([w_hi, w_lo], axis=1), preferred_element_type=F32)
    logits = (both[:, :LANES] + both[:, LANES:]
              + jnp.dot(h_lo, w_hi, preferred_element_type=F32)) + br_ref[0]
    weights, mask = _top2(logits)
    m = jnp.where(mask, 1.0, 0.0).astype(BF16)
    before = carry_ref[0:1, :]
    chunk = lax.broadcasted_iota(jnp.int32, (n_chunks, n), 0)
    tok = lax.broadcasted_iota(jnp.int32, (n_chunks, n), 1)
    in_earlier_chunk = jnp.where(tok < chunk * CNT_CHUNK, 1.0, 0.0).astype(BF16)
    upto_chunk_end = jnp.where(tok < (chunk + 1) * CNT_CHUNK, 1.0, 0.0).astype(BF16)
    chunk_start = jnp.dot(in_earlier_chunk, m, preferred_element_type=F32) + before
    cnt = jnp.dot(upto_chunk_end, m, preferred_element_type=F32) + before
    row = lax.broadcasted_iota(jnp.int32, (CNT_CHUNK, CNT_CHUNK), 0)
    col = lax.broadcasted_iota(jnp.int32, (CNT_CHUNK, CNT_CHUNK), 1)
    strictly_lower = jnp.where(row > col, 1.0, 0.0).astype(BF16)
    in_chunk = jnp.concatenate(
        [jnp.dot(strictly_lower, m[j * CNT_CHUNK:(j + 1) * CNT_CHUNK, :],
                 preferred_element_type=F32) + chunk_start[j:j + 1, :]
         for j in range(n_chunks)], axis=0)
    rank = jnp.where(mask, in_chunk, -1.0)
    carry_ref[...] = jnp.broadcast_to(cnt[n_chunks - 1:, :], carry_ref.shape)
    h_ref[...] = h_hi
    rank_ref[...] = rank
    w_ref[...] = weights
    rank_row_ref[...] = rank.T[:N_EXPERTS, :]
    cnt_ref[...] = cnt.astype(jnp.int32)


def _dispatch_kernel(tile_e_ref, ls0_ref, c_lo_ref, n_g_ref, h_ref, rank_ref, o_ref):
    i = pl.program_id(0)
    e = tile_e_ref[i]
    n_chunks = rank_ref.shape[1]
    span = GATHER_CHUNKS * CNT_CHUNK
    subs = SLOT_TILE // GATHER_SLOTS

    def gather(sub, k):
        q = i * subs + sub
        slot = (ls0_ref[q]
                + lax.broadcasted_iota(jnp.int32, (GATHER_SLOTS, CNT_CHUNK), 0)).astype(F32)
        wanted = c_lo_ref[q] + k * GATHER_CHUNKS
        start = jnp.minimum(wanted, n_chunks - GATHER_CHUNKS)
        p = jnp.concatenate(
            [jnp.where((start + j >= wanted) & (rank_ref[e, pl.ds(start + j, 1), :] == slot),
                       1.0, 0.0) for j in range(GATHER_CHUNKS)], axis=1).astype(BF16)
        tokens = pl.ds(pl.multiple_of(start * CNT_CHUNK, CNT_CHUNK), span)
        return jnp.dot(p, h_ref[tokens, :], preferred_element_type=F32).astype(BF16)

    for sub in range(subs):
        o_ref[sub * GATHER_SLOTS:(sub + 1) * GATHER_SLOTS, :] = gather(sub, 0)
    for sub in range(subs):
        def more(k, carry, sub=sub):
            o_ref[sub * GATHER_SLOTS:(sub + 1) * GATHER_SLOTS, :] += gather(sub, k)
            return carry

        lax.fori_loop(1, n_g_ref[i * subs + sub], more, 0)


def _expert_kernel(tile_e_ref, n_live_ref, x_ref, w1_ref, w3_ref, w2_ref, o_ref,
                   w13b_ref, w2b_ref):
    k = pl.program_id(0)
    n_tiles = pl.num_programs(0) - 1
    d_ff = w2b_ref.shape[0]
    tile = k - 1
    live = (tile >= 0) & (tile < n_live_ref[0])
    expert_of = lambda t: tile_e_ref[jnp.clip(t, 0, n_tiles - 1)]

    @pl.when((tile == 0) | ((tile > 0) & (expert_of(tile) != expert_of(tile - 1))))
    def _():
        w13b_ref[:, :d_ff] = w1_ref[0, 0].astype(BF16)
        w13b_ref[:, d_ff:] = w3_ref[0, 0].astype(BF16)

    @pl.when(live)
    def _():
        ab = jnp.dot(x_ref[...], w13b_ref[...], preferred_element_type=F32)
        a, b = ab[:, :d_ff], ab[:, d_ff:]
        g = (a * jax.nn.sigmoid(a) * b).astype(BF16)
        o_ref[...] = jnp.dot(g, w2b_ref[...], preferred_element_type=F32).astype(BF16)

    @pl.when((tile >= 0) & jnp.logical_not(live))
    def _():
        o_ref[...] = jnp.zeros(o_ref.shape, BF16)

    @pl.when((k == 0) | ((k < n_tiles) & (expert_of(k) != expert_of(tile))))
    def _():
        w2b_ref[...] = w2_ref[0, 0].astype(BF16)


def _combine_kernel(wstart_ref, base_ref, x_ref, mod_ref, rank_ref, w_ref, fg_ref, *rest, final):
    n_win = COMBINE_CHUNKS * N_EXPERTS
    y_refs, o_ref = rest[:n_win], rest[n_win]
    i = pl.program_id(0)
    lane = lax.broadcasted_iota(jnp.int32, (CNT_CHUNK, WINDOW), 1)
    gate = mod_ref[0][:, 2 * D_MODEL:]
    chunks = []
    for s in range(COMBINE_CHUNKS):
        rows = slice(s * CNT_CHUNK, (s + 1) * CNT_CHUNK)
        rank = rank_ref[rows, :]
        wts = w_ref[rows, :]
        acc = jnp.zeros((CNT_CHUNK, D_MODEL), F32)
        for e in range(N_EXPERTS):
            w = s * N_EXPERTS + e
            rk = rank[:, e:e + 1]
            slot = rk + base_ref[e].astype(F32)
            window_slot = (wstart_ref[i * n_win + w] + lane).astype(F32)
            hit = (rk >= 0.0) & (slot == window_slot)
            p = jnp.where(hit, wts[:, e:e + 1], 0.0).astype(BF16)
            acc = acc + jnp.dot(p, y_refs[w][...], preferred_element_type=F32)
        y = x_ref[rows, :] + gate * acc
        if final:
            ms = jnp.mean(y * y, axis=-1, keepdims=True)
            y = y * lax.rsqrt(ms + EPS) * fg_ref[...]
        chunks.append(y)
    y = jnp.concatenate(chunks, axis=0)
    if final:
        y = jnp.swapaxes(y.reshape(SUBLANES, GROUPS, D_MODEL), 0, 1).reshape(TS, D_MODEL)
    o_ref[...] = y


def _route_metadata(cnt_end, n_tok):
    i32 = jnp.int32
    cnt_end = cnt_end[:, :N_EXPERTS]
    cnt_start = jnp.concatenate([jnp.zeros((1, N_EXPERTS), i32), cnt_end[:-1]], axis=0)
    n_chunks = cnt_end.shape[0]
    total = cnt_end[-1]
    n_tiles = (total + SLOT_TILE - 1) // SLOT_TILE
    tile_end = jnp.cumsum(n_tiles)
    tile_start = tile_end - n_tiles
    base = (tile_start * SLOT_TILE).astype(i32)
    max_tiles = 2 * n_tok // SLOT_TILE + N_EXPERTS
    i = jnp.arange(max_tiles, dtype=i32)
    tile_e = jnp.minimum(jnp.sum(i[:, None] >= tile_end[None, :], axis=1), N_EXPERTS - 1)
    subs = SLOT_TILE // GATHER_SLOTS
    q = jnp.arange(max_tiles * subs, dtype=i32)
    q_e = tile_e[q // subs]
    ls0 = (q // subs - tile_start[q_e]) * SLOT_TILE + (q % subs) * GATHER_SLOTS
    live = (q // subs < tile_end[-1]) & (ls0 < total[q_e])
    ends = cnt_end[:, q_e].T
    starts = cnt_start[:, q_e].T
    c_lo = jnp.sum(ends <= ls0[:, None], axis=1)
    c_hi = jnp.sum(starts < (ls0 + GATHER_SLOTS)[:, None], axis=1) - 1
    n_g = jnp.where(live, (c_hi - c_lo + GATHER_CHUNKS) // GATHER_CHUNKS, 0)
    c_lo = jnp.minimum(c_lo, n_chunks - 1)
    n_slots = max_tiles * SLOT_TILE
    wstart = jnp.minimum((base[None, :] + cnt_start) // ROW_ALIGN * ROW_ALIGN, n_slots - WINDOW)
    as_i32 = lambda a: a.astype(i32)
    return (as_i32(tile_e), as_i32(tile_end[-1:]), as_i32(ls0), as_i32(c_lo), as_i32(n_g), base,
            as_i32(wstart).reshape(-1))


def _moe_ffn(x, mods, layer, g, wr, br, w1, w3, w2, final_gain, final):
    batch, seq, _ = x.shape
    n_tok = batch * seq
    d_ff = w1.shape[3]
    j = layer // 2
    xt = x.reshape(n_tok, D_MODEL)
    params = lambda: pltpu.CompilerParams(
        dimension_semantics=("arbitrary",), vmem_limit_bytes=VMEM_LIMIT)

    route_tiles_per_seq = seq // ROUTE_TILE
    h, rank, wts, rank_row, cnt_end = pl.pallas_call(
        _router_kernel,
        grid=(n_tok // ROUTE_TILE,),
        in_specs=[
            pl.BlockSpec((ROUTE_TILE, D_MODEL), lambda i: (i, 0)),
            pl.BlockSpec((1, 1, 3 * D_MODEL),
                         lambda i: (layer * PAD_B + i // route_tiles_per_seq, 0, 0)),
            _layer_spec((1, D_MODEL), layer),
            _layer_spec((D_MODEL, LANES), j),
            _layer_spec((1, LANES), j),
        ],
        out_specs=[
            pl.BlockSpec((ROUTE_TILE, D_MODEL), lambda i: (i, 0)),
            pl.BlockSpec((ROUTE_TILE, LANES), lambda i: (i, 0)),
            pl.BlockSpec((ROUTE_TILE, LANES), lambda i: (i, 0)),
            pl.BlockSpec((N_EXPERTS, ROUTE_TILE), lambda i: (0, i)),
            pl.BlockSpec((ROUTE_TILE // CNT_CHUNK, LANES), lambda i: (i, 0)),
        ],
        out_shape=[
            jax.ShapeDtypeStruct((n_tok, D_MODEL), BF16),
            jax.ShapeDtypeStruct((n_tok, LANES), F32),
            jax.ShapeDtypeStruct((n_tok, LANES), F32),
            jax.ShapeDtypeStruct((N_EXPERTS, n_tok), F32),
            jax.ShapeDtypeStruct((n_tok // CNT_CHUNK, LANES), jnp.int32),
        ],
        scratch_shapes=[pltpu.VMEM((SUBLANES, LANES), F32)],
        compiler_params=params(),
        name="moe_router",
    )(xt, mods, g, wr, br)

    tile_e, n_live, ls0, c_lo, n_g, base, wstart = _route_metadata(cnt_end, n_tok)
    max_tiles = tile_e.shape[0]
    n_slots = max_tiles * SLOT_TILE

    x_sorted = pl.pallas_call(
        _dispatch_kernel,
        grid_spec=pltpu.PrefetchScalarGridSpec(
            num_scalar_prefetch=4,
            grid=(max_tiles,),
            in_specs=[pl.BlockSpec(memory_space=pltpu.VMEM),
                      pl.BlockSpec(memory_space=pltpu.VMEM)],
            out_specs=pl.BlockSpec((SLOT_TILE, D_MODEL), lambda i, *_: (i, 0)),
        ),
        out_shape=jax.ShapeDtypeStruct((n_slots, D_MODEL), BF16),
        compiler_params=params(),
        name="moe_dispatch",
    )(tile_e, ls0, c_lo, n_g, h, rank_row.reshape(N_EXPERTS, n_tok // CNT_CHUNK, CNT_CHUNK))

    this_expert = lambda k, te, nl: (j, te[jnp.maximum(k - 1, 0)], 0, 0)
    next_expert = lambda k, te, nl: (j, te[jnp.minimum(k, max_tiles - 1)], 0, 0)
    y_sorted = pl.pallas_call(
        _expert_kernel,
        grid_spec=pltpu.PrefetchScalarGridSpec(
            num_scalar_prefetch=2,
            grid=(max_tiles + 1,),
            in_specs=[
                pl.BlockSpec((SLOT_TILE, D_MODEL), lambda k, te, nl: (jnp.maximum(k - 1, 0), 0)),
                pl.BlockSpec((1, 1, D_MODEL, d_ff), this_expert),
                pl.BlockSpec((1, 1, D_MODEL, d_ff), this_expert),
                pl.BlockSpec((1, 1, d_ff, D_MODEL), next_expert),
            ],
            out_specs=pl.BlockSpec((SLOT_TILE, D_MODEL),
                                   lambda k, te, nl: (jnp.maximum(k - 1, 0), 0)),
            scratch_shapes=[pltpu.VMEM((D_MODEL, 2 * d_ff), BF16),
                            pltpu.VMEM((d_ff, D_MODEL), BF16)],
        ),
        out_shape=jax.ShapeDtypeStruct((n_slots, D_MODEL), BF16),
        compiler_params=params(),
        name="moe_experts",
    )(tile_e, n_live, x_sorted, w1, w3, w2)

    rows = COMBINE_CHUNKS * CNT_CHUNK
    assert rows == TS
    steps_per_seq = seq // rows
    n_win = COMBINE_CHUNKS * N_EXPERTS
    window_spec = lambda w: pl.BlockSpec(
        (pl.Element(WINDOW), pl.Element(D_MODEL)),
        lambda i, ws, bs: (pl.multiple_of(ws[i * n_win + w], ROW_ALIGN), 0))
    out = pl.pallas_call(
        functools.partial(_combine_kernel, final=final),
        grid_spec=pltpu.PrefetchScalarGridSpec(
            num_scalar_prefetch=2,
            grid=(n_tok // rows,),
            in_specs=[
                pl.BlockSpec((rows, D_MODEL), lambda i, ws, bs: (i, 0)),
                pl.BlockSpec((1, 1, 3 * D_MODEL),
                             lambda i, ws, bs: (layer * PAD_B + i // steps_per_seq, 0, 0)),
                pl.BlockSpec((rows, LANES), lambda i, ws, bs: (i, 0)),
                pl.BlockSpec((rows, LANES), lambda i, ws, bs: (i, 0)),
                pl.BlockSpec((1, D_MODEL), lambda i, ws, bs: (0, 0)),
            ] + [window_spec(w) for w in range(n_win)],
            out_specs=pl.BlockSpec((rows, D_MODEL), lambda i, ws, bs: (i, 0)),
        ),
        out_shape=jax.ShapeDtypeStruct((n_tok, D_MODEL), F32),
        compiler_params=params(),
        name="moe_combine",
    )(wstart, base, xt, mods, rank, wts, final_gain, *([y_sorted] * n_win))
    return out.reshape(x.shape)


def _block_diag_halves(w):
    depth, heads, hd, _ = w.shape
    per_tile = MXU_DIM // hd
    w = w.reshape(depth, heads // per_tile, per_tile, hd, hd)
    eye = jnp.eye(per_tile, dtype=w.dtype)
    bd = jnp.einsum("dtiab,ij->dtiajb", w, eye)
    return bd.reshape(depth, heads // per_tile, MXU_DIM, MXU_DIM)


def kernel(x, c, mix_norm, mix_mod_w, mix_mod_b, w_in, lru_conv_w, lru_conv_b, lru_wa, lru_ba,
           lru_wi, lru_bi, lru_lambda, sc_conv_w, gn_lru, gn_sc, w_out, ffn_norm, ffn_mod_w,
           ffn_mod_b, dense_w1, dense_w3, dense_w2, router_w, router_b, exp_w1, exp_w3, exp_w2,
           final_norm):
    depth = w_in.shape[0]
    batch = x.shape[0]
    c_pad = jnp.pad(c, ((0, PAD_B - batch), (0, 0)))
    mix_mods = _modulation(c_pad, mix_mod_w, mix_mod_b).reshape(depth * PAD_B, 1, 3 * D_MODEL)
    ffn_mods = _modulation(c_pad, ffn_mod_w, ffn_mod_b).reshape(depth * PAD_B, 1, 3 * D_MODEL)

    row = lambda p: p.reshape(depth, 1, p.shape[-1])
    wg = jnp.concatenate([_block_diag_halves(lru_wa), _block_diag_halves(lru_wi)],
                         axis=-1).astype(BF16)
    bg = jnp.stack([lru_ba.reshape(depth, D_LRU), lru_bi.reshape(depth, D_LRU)], axis=1)
    head = jnp.arange(MXU_DIM) // HEAD_DIM
    gmat = jnp.where(head[:, None] == head[None, :], 1.0 / HEAD_DIM, 0.0).astype(BF16)
    w_in_b = w_in.astype(BF16)
    w_out_b = w_out.astype(BF16)
    dense_w1_b, dense_w3_b, dense_w2_b = (w.astype(BF16) for w in (dense_w1, dense_w3, dense_w2))
    n_moe = router_w.shape[0]
    wr = jnp.pad(router_w, ((0, 0), (0, 0), (0, LANES - N_EXPERTS)))
    br = jnp.pad(router_b, ((0, 0), (0, LANES - N_EXPERTS)),
                 constant_values=-jnp.inf).reshape(n_moe, 1, LANES)
    final_gain = final_norm.reshape(1, D_MODEL)
    mix_g, ffn_g = row(mix_norm), row(ffn_norm)
    lcb, lam, gnl, gns = row(lru_conv_b), row(lru_lambda), row(gn_lru), row(gn_sc)

    for l in range(depth):
        x = _mixer(x, mix_mods, l, mix_g, w_in_b, lru_conv_w, lcb, wg, bg, lam, sc_conv_w, gnl,
                   gns, gmat, w_out_b, natural_in=(l == 0))
        if l % 2 == 0:
            x = _dense_ffn(x, ffn_mods, l, ffn_g, dense_w1_b, dense_w3_b, dense_w2_b)
        else:
            x = _moe_ffn(x, ffn_mods, l, ffn_g, wr, br, exp_w1, exp_w3, exp_w2, final_gain,
                         final=(l == depth - 1))
    if depth % 2 == 1:
        raise NotImplementedError("final norm is fused into the last MoE layer")
    return x
```

```python
import functools

import jax
import jax.numpy as jnp
from jax import lax
from jax.experimental import pallas as pl
from jax.experimental.pallas import tpu as pltpu

F32 = jnp.float32
BF16 = jnp.bfloat16

D_MODEL = 1024
D_LRU = 512
D_SC = 512
D_IN = 2 * D_LRU + 3 * D_SC
HEAD_DIM = 64
LRU_CONV = 4
SC_CONV = 3
LRU_C = 8.0
N_EXPERTS = 8
EPS = 1e-6

SUBLANES = 8
LANES = 128
MXU_DIM = 256
TS = SUBLANES ** 3
GROUPS = TS // SUBLANES
MIXER_TILES = 2
PAD_B = 8
VMEM_LIMIT = 56 * 1024 * 1024


def _mod_norm(x, g, scale, shift):
    ms = jnp.mean(x * x, axis=-1, keepdims=True)
    return x * lax.rsqrt(ms + EPS) * (g * (1.0 + scale)) + shift


def _split_mod(mod):
    return mod[:, :D_MODEL], mod[:, D_MODEL:2 * D_MODEL], mod[:, 2 * D_MODEL:]


def _layer_spec(shape, layer):
    zeros = (0,) * len(shape)
    return pl.BlockSpec((1, *shape), lambda *_: (layer, *zeros), pipeline_mode=pl.Buffered(1))


def _mod_kernel(c_ref, w_ref, b_ref, o_ref):
    c = c_ref[...]
    s = (c * jax.nn.sigmoid(c)).astype(BF16)
    o_ref[0] = jnp.dot(s, w_ref[0].astype(BF16), preferred_element_type=F32) + b_ref[0]


def _modulation(c_pad, w, b):
    depth = w.shape[0]
    n_parts = 3
    return pl.pallas_call(
        _mod_kernel,
        grid=(depth, n_parts),
        in_specs=[
            pl.BlockSpec((PAD_B, D_MODEL), lambda l, j: (0, 0)),
            pl.BlockSpec((1, D_MODEL, D_MODEL), lambda l, j: (l, 0, j)),
            pl.BlockSpec((1, 1, D_MODEL), lambda l, j: (l, 0, j)),
        ],
        out_specs=pl.BlockSpec((1, PAD_B, D_MODEL), lambda l, j: (l, 0, j)),
        out_shape=jax.ShapeDtypeStruct((depth, PAD_B, 3 * D_MODEL), F32),
        compiler_params=pltpu.CompilerParams(
            dimension_semantics=("arbitrary", "arbitrary"), vmem_limit_bytes=VMEM_LIMIT),
        name="adaln_mod",
    )(c_pad, w, b.reshape(depth, 1, 3 * D_MODEL))


def _scan8(a, u, row_in_group):
    for d in (1, 2, 4):
        keep = row_in_group >= d
        a_sh = jnp.where(keep, pltpu.roll(a, d, axis=0), 1.0)
        u_sh = jnp.where(keep, pltpu.roll(u, d, axis=0), 0.0)
        u = a * u_sh + u
        a = a * a_sh
    return a, u


def _row_in_group(rows, cols):
    return lax.broadcasted_iota(jnp.int32, (rows, cols), 0) % SUBLANES


def _shift_rows_down(x, first_row):
    row = lax.broadcasted_iota(jnp.int32, x.shape, 0)
    return jnp.where(row == 0, first_row, pltpu.roll(x, 1, axis=0))


def _group_rms(y, gmat_ref, gain):
    y2 = (y * y).astype(BF16)
    ms = jnp.concatenate(
        [jnp.dot(y2[:, :MXU_DIM], gmat_ref[...], preferred_element_type=F32),
         jnp.dot(y2[:, MXU_DIM:], gmat_ref[...], preferred_element_type=F32)], axis=-1)
    return y * lax.rsqrt(ms + EPS) * gain


def _mixer_kernel(*refs, natural_in):
    n_x = D_MODEL // LANES if natural_in else 1
    x_refs, refs = refs[:n_x], refs[n_x:]
    (mod_ref, g_ref, win_ref, lcw_ref, lcb_ref, wg_ref, bg_ref, lam_ref, scw_ref, gnl_ref,
     gns_ref, gmat_ref, wout_ref, o_ref, xl_buf, v_buf, a0_ref, u0_ref, a1_ref, u1_ref, b64_ref,
     carry_ref) = refs

    @pl.when(pl.program_id(1) == 0)
    def _():
        xl_buf[...] = jnp.zeros(xl_buf.shape, F32)
        v_buf[...] = jnp.zeros(v_buf.shape, F32)
        carry_ref[...] = jnp.zeros(carry_ref.shape, F32)

    for sub in range(MIXER_TILES):
        r0 = sub * TS
        if natural_in:
            x = jnp.concatenate(
                [jnp.concatenate([r[0, pl.ds(r0 + s, GROUPS, stride=SUBLANES), :] for r in x_refs],
                                 axis=1) for s in range(SUBLANES)], axis=0)
        else:
            x = x_refs[0][0, r0:r0 + TS, :]
        o_ref[0, r0:r0 + TS, :] = _mixer_tile(
            x, mod_ref[0], g_ref, win_ref, lcw_ref, lcb_ref, wg_ref, bg_ref, lam_ref, scw_ref,
            gnl_ref, gns_ref, gmat_ref, wout_ref, xl_buf, v_buf, a0_ref, u0_ref, a1_ref, u1_ref,
            b64_ref, carry_ref)


def _slabs(y):
    return [y[s * GROUPS:(s + 1) * GROUPS, :] for s in range(SUBLANES)]


def _causal_conv(y, w_ref, hist_ref):
    taps = w_ref.shape[1]
    slabs = _slabs(y)
    early = {j - SUBLANES: _shift_rows_down(slabs[j], hist_ref[j:j + 1, :])
             for j in range(SUBLANES - taps + 1, SUBLANES)}
    at = lambda j: slabs[j] if j >= 0 else early[j]
    out = []
    for s in range(SUBLANES):
        acc = w_ref[0, 0:1, :] * at(s - (taps - 1))
        for k in range(1, taps):
            acc = acc + w_ref[0, k:k + 1, :] * at(s - (taps - 1) + k)
        out.append(acc)
    for j in range(SUBLANES - taps + 1, SUBLANES):
        hist_ref[j:j + 1, :] = slabs[j][GROUPS - 1:, :]
    return jnp.concatenate(out, axis=0)


def _mixer_tile(x, mod, g_ref, win_ref, lcw_ref, lcb_ref, wg_ref, bg_ref, lam_ref, scw_ref,
                gnl_ref, gns_ref, gmat_ref, wout_ref, xl_buf, v_buf, a0_ref, u0_ref, a1_ref,
                u1_ref, b64_ref, carry_ref):
    n2 = GROUPS // SUBLANES
    shift, scale, gate = _split_mod(mod)
    h = _mod_norm(x, g_ref[0], scale, shift).astype(BF16)
    def in_proj(lo, hi):
        return jnp.dot(h, win_ref[0, :, lo:hi], preferred_element_type=F32)

    xc = _causal_conv(in_proj(0, D_LRU), lcw_ref, xl_buf) + lcb_ref[0]

    cx_sc = in_proj(2 * D_LRU + D_SC, D_IN)
    xcb = xc.astype(BF16)
    g0 = jnp.dot(xcb[:, :MXU_DIM], wg_ref[0, 0], preferred_element_type=F32)
    g1 = jnp.dot(xcb[:, MXU_DIM:], wg_ref[0, 1], preferred_element_type=F32)
    cv = _causal_conv(cx_sc[:, :D_SC] * cx_sc[:, D_SC:], scw_ref, v_buf)

    b_sc = in_proj(2 * D_LRU, 2 * D_LRU + D_SC)
    r = jax.nn.sigmoid(jnp.concatenate([g0[:, :MXU_DIM], g1[:, :MXU_DIM]], axis=-1)
                       + bg_ref[0, 0:1, :])
    i = jax.nn.sigmoid(jnp.concatenate([g0[:, MXU_DIM:], g1[:, MXU_DIM:]], axis=-1)
                       + bg_ref[0, 1:2, :])
    log_a = r * ((-LRU_C) * jax.nn.softplus(-lam_ref[0]))
    a = jnp.exp(log_a)
    uin = jnp.sqrt(-jnp.tanh(log_a) * (1.0 + a * a)) * (i * xc)
    yn_sc = _group_rms(b_sc * cv, gmat_ref, gns_ref[0]).astype(BF16)
    out_sc = jnp.dot(yn_sc, wout_ref[0, D_LRU:, :], preferred_element_type=F32)

    gate_lru = jax.nn.gelu(in_proj(D_LRU, 2 * D_LRU))

    last = SUBLANES - 1
    hs_blocks = []
    for lb in range(D_LRU // LANES):
        lanes = slice(lb * LANES, (lb + 1) * LANES)
        h_prev = carry_ref[lb, last:last + 1, :]
        acc_a = acc_u = None
        for s, (a_s, u_s) in enumerate(zip(_slabs(a[:, lanes]), _slabs(uin[:, lanes]))):
            if s == 0:
                acc_a, acc_u = a_s, u_s
            else:
                acc_u = a_s * acc_u + u_s
                acc_a = a_s * acc_a
            a0_ref[lb, s * GROUPS:(s + 1) * GROUPS, :] = acc_a
            u0_ref[lb, s * GROUPS:(s + 1) * GROUPS, :] = acc_u
        a1, u1 = _scan8(acc_a, acc_u, _row_in_group(GROUPS, LANES))
        a1_ref[lb] = a1
        u1_ref[lb] = u1
        a2, u2 = _scan8(a1_ref[lb, pl.ds(last, n2, stride=SUBLANES), :],
                        u1_ref[lb, pl.ds(last, n2, stride=SUBLANES), :],
                        _row_in_group(n2, LANES))
        h2 = u2 + a2 * h_prev
        carry_ref[lb] = h2
        b64_ref[lb] = _shift_rows_down(h2, h_prev)
        before64 = jnp.concatenate(
            [jnp.broadcast_to(b64_ref[lb, j:j + 1, :], (SUBLANES, LANES)) for j in range(n2)],
            axis=0)
        h1 = u1 + a1 * before64
        before8 = _shift_rows_down(h1, h_prev)
        hs_blocks.append(jnp.concatenate(
            [u0_ref[lb, s * GROUPS:(s + 1) * GROUPS, :]
             + a0_ref[lb, s * GROUPS:(s + 1) * GROUPS, :] * before8 for s in range(SUBLANES)],
            axis=0))
    hs = jnp.concatenate(hs_blocks, axis=-1)

    yn_lru = _group_rms(gate_lru * hs, gmat_ref, gnl_ref[0]).astype(BF16)
    out = out_sc + jnp.dot(yn_lru, wout_ref[0, :D_LRU, :], preferred_element_type=F32)
    return x + gate * out


def _mixer(x, mods, layer, g, w_in, lcw, lcb, wg, bg, lam, scw, gnl, gns, gmat, w_out, natural_in):
    batch, seq, _ = x.shape
    ts = TS
    nlb = D_LRU // LANES
    per_layer = lambda shape: _layer_spec(shape, layer)
    rows = MIXER_TILES * ts
    tile_spec = pl.BlockSpec((1, rows, D_MODEL), lambda b, t: (b, t, 0))
    if natural_in:
        x_specs = [pl.BlockSpec((1, rows, LANES), lambda b, t, l=l: (b, t, l))
                   for l in range(D_MODEL // LANES)]
    else:
        x_specs = [tile_spec]
    return pl.pallas_call(
        functools.partial(_mixer_kernel, natural_in=natural_in),
        grid=(batch, seq // rows),
        in_specs=x_specs + [
            pl.BlockSpec((1, 1, 3 * D_MODEL), lambda b, t: (layer * PAD_B + b, 0, 0)),
            per_layer((1, D_MODEL)),
            per_layer((D_MODEL, D_IN)),
            per_layer((LRU_CONV, D_LRU)),
            per_layer((1, D_LRU)),
            per_layer((2, MXU_DIM, 2 * MXU_DIM)),
            per_layer((2, D_LRU)),
            per_layer((1, D_LRU)),
            per_layer((SC_CONV, D_SC)),
            per_layer((1, D_LRU)),
            per_layer((1, D_SC)),
            pl.BlockSpec((MXU_DIM, MXU_DIM), lambda b, t: (0, 0)),
            per_layer((D_LRU + D_SC, D_MODEL)),
        ],
        out_specs=tile_spec,
        out_shape=jax.ShapeDtypeStruct(x.shape, F32),
        scratch_shapes=[
            pltpu.VMEM((SUBLANES, D_LRU), F32),
            pltpu.VMEM((SUBLANES, D_SC), F32),
            pltpu.VMEM((nlb, ts, LANES), F32),
            pltpu.VMEM((nlb, ts, LANES), F32),
            pltpu.VMEM((nlb, GROUPS, LANES), F32),
            pltpu.VMEM((nlb, GROUPS, LANES), F32),
            pltpu.VMEM((nlb, SUBLANES, LANES), F32),
            pltpu.VMEM((nlb, SUBLANES, LANES), F32),
        ],
        compiler_params=pltpu.CompilerParams(
            dimension_semantics=("arbitrary", "arbitrary"), vmem_limit_bytes=VMEM_LIMIT),
        name="mixer",
    )(*([x] * len(x_specs)), mods, g, w_in, lcw, lcb, wg, bg, lam, scw, gnl, gns, gmat, w_out)


def _swiglu(h, w1, w3, w2):
    a = jnp.dot(h, w1, preferred_element_type=F32)
    b = jnp.dot(h, w3, preferred_element_type=F32)
    g = (a * jax.nn.sigmoid(a) * b).astype(BF16)
    return jnp.dot(g, w2, preferred_element_type=F32)


def _mxu_halves(width):
    split = pl.cdiv(width // 2, MXU_DIM) * MXU_DIM
    return slice(0, split), slice(split, width)


def _dense_ffn_kernel(x_ref, mod_ref, g_ref, w1_ref, w3_ref, w2_ref, o_ref):
    x = x_ref[0]
    shift, scale, gate = _split_mod(mod_ref[0])
    h = _mod_norm(x, g_ref[0], scale, shift).astype(BF16)
    f = jnp.zeros(x.shape, F32)
    for cols in _mxu_halves(w1_ref.shape[2]):
        f = f + _swiglu(h, w1_ref[0, :, cols], w3_ref[0, :, cols], w2_ref[0, cols, :])
    o_ref[0] = x + gate * f


def _dense_ffn(x, mods, layer, g, w1, w3, w2, tm=1024):
    batch, seq, _ = x.shape
    d_ff = w1.shape[2]
    j = layer // 2
    return pl.pallas_call(
        _dense_ffn_kernel,
        grid=(batch, seq // tm),
        in_specs=[
            pl.BlockSpec((1, tm, D_MODEL), lambda b, t: (b, t, 0)),
            pl.BlockSpec((1, 1, 3 * D_MODEL), lambda b, t: (layer * PAD_B + b, 0, 0)),
            _layer_spec((1, D_MODEL), layer),
            _layer_spec((D_MODEL, d_ff), j),
            _layer_spec((D_MODEL, d_ff), j),
            _layer_spec((d_ff, D_MODEL), j),
        ],
        out_specs=pl.BlockSpec((1, tm, D_MODEL), lambda b, t: (b, t, 0)),
        out_shape=jax.ShapeDtypeStruct(x.shape, F32),
        compiler_params=pltpu.CompilerParams(
            dimension_semantics=("arbitrary", "arbitrary"), vmem_limit_bytes=VMEM_LIMIT),
        name="dense_ffn",
    )(x, mods, g, w1, w3, w2)


ROUTE_TILE = 1024
CNT_CHUNK = 128
COMBINE_CHUNKS = 4
SLOT_TILE = 512
GATHER_SLOTS = 128
GATHER_CHUNKS = 6
ROW_ALIGN = 16
WINDOW = CNT_CHUNK + ROW_ALIGN


def _top2(logits):
    lane = lax.broadcasted_iota(jnp.int32, logits.shape, 1)
    m1 = jnp.max(logits, axis=-1, keepdims=True)
    i1 = jnp.min(jnp.where(logits == m1, lane, LANES), axis=-1, keepdims=True)
    rest = jnp.where(lane == i1, -jnp.inf, logits)
    m2 = jnp.max(rest, axis=-1, keepdims=True)
    i2 = jnp.min(jnp.where(rest == m2, lane, LANES), axis=-1, keepdims=True)
    e2 = jnp.exp(m2 - m1)
    w_first = 1.0 / (1.0 + e2)
    w_second = e2 / (1.0 + e2)
    first = lane == i1
    second = lane == i2
    weights = jnp.where(first, w_first, 0.0) + jnp.where(second, w_second, 0.0)
    return weights, first | second


def _split_bf16(v):
    hi = v.astype(BF16)
    return hi, (v - hi.astype(F32)).astype(BF16)


def _router_kernel(x_ref, mod_ref, g_ref, wr_ref, br_ref,
                   h_ref, rank_ref, w_ref, rank_row_ref, cnt_ref, carry_ref):
    n = ROUTE_TILE
    n_chunks = n // CNT_CHUNK

    @pl.when(pl.program_id(0) == 0)
    def _():
        carry_ref[...] = jnp.zeros(carry_ref.shape, F32)

    shift, scale, _ = _split_mod(mod_ref[0])
    h = _mod_norm(x_ref[...], g_ref[0], scale, shift)
    h_hi, h_lo = _split_bf16(h)
    w_hi, w_lo = _split_bf16(wr_ref[0])
    both = jnp.dot(h_hi, jnp.concatenate([w_hi, w_lo], axis=1), preferred_element_type=F32)
    logits = (both[:, :LANES] + both[:, LANES:]
              + jnp.dot(h_lo, w_hi, preferred_element_type=F32)) + br_ref[0]
    weights, mask = _top2(logits)
    m = jnp.where(mask, 1.0, 0.0).astype(BF16)
    before = carry_ref[0:1, :]
    chunk = lax.broadcasted_iota(jnp.int32, (n_chunks, n), 0)
    tok = lax.broadcasted_iota(jnp.int32, (n_chunks, n), 1)
    in_earlier_chunk = jnp.where(tok < chunk * CNT_CHUNK, 1.0, 0.0).astype(BF16)
    upto_chunk_end = jnp.where(tok < (chunk + 1) * CNT_CHUNK, 1.0, 0.0).astype(BF16)
    chunk_start = jnp.dot(in_earlier_chunk, m, preferred_element_type=F32) + before
    cnt = jnp.dot(upto_chunk_end, m, preferred_element_type=F32) + before
    row = lax.broadcasted_iota(jnp.int32, (CNT_CHUNK, CNT_CHUNK), 0)
    col = lax.broadcasted_iota(jnp.int32, (CNT_CHUNK, CNT_CHUNK), 1)
    strictly_lower = jnp.where(row > col, 1.0, 0.0).astype(BF16)
    in_chunk = jnp.concatenate(
        [jnp.dot(strictly_lower, m[j * CNT_CHUNK:(j + 1) * CNT_CHUNK, :],
                 preferred_element_type=F32) + chunk_start[j:j + 1, :]
         for j in range(n_chunks)], axis=0)
    rank = jnp.where(mask, in_chunk, -1.0)
    carry_ref[...] = jnp.broadcast_to(cnt[n_chunks - 1:, :], carry_ref.shape)
    h_ref[...] = h_hi
    rank_ref[...] = rank
    w_ref[...] = weights
    rank_row_ref[...] = rank.T[:N_EXPERTS, :]
    cnt_ref[...] = cnt.astype(jnp.int32)


def _dispatch_kernel(tile_e_ref, ls0_ref, c_lo_ref, n_g_ref, h_ref, rank_ref, o_ref):
    i = pl.program_id(0)
    e = tile_e_ref[i]
    n_chunks = rank_ref.shape[1]
    span = GATHER_CHUNKS * CNT_CHUNK
    subs = SLOT_TILE // GATHER_SLOTS

    def gather(sub, k):
        q = i * subs + sub
        slot = (ls0_ref[q]
                + lax.broadcasted_iota(jnp.int32, (GATHER_SLOTS, CNT_CHUNK), 0)).astype(F32)
        wanted = c_lo_ref[q] + k * GATHER_CHUNKS
        start = jnp.minimum(wanted, n_chunks - GATHER_CHUNKS)
        p = jnp.concatenate(
            [jnp.where((start + j >= wanted) & (rank_ref[e, pl.ds(start + j, 1), :] == slot),
                       1.0, 0.0) for j in range(GATHER_CHUNKS)], axis=1).astype(BF16)
        tokens = pl.ds(pl.multiple_of(start * CNT_CHUNK, CNT_CHUNK), span)
        return jnp.dot(p, h_ref[tokens, :], preferred_element_type=F32).astype(BF16)

    for sub in range(subs):
        o_ref[sub * GATHER_SLOTS:(sub + 1) * GATHER_SLOTS, :] = gather(sub, 0)
    for sub in range(subs):
        def more(k, carry, sub=sub):
            o_ref[sub * GATHER_SLOTS:(sub + 1) * GATHER_SLOTS, :] += gather(sub, k)
            return carry

        lax.fori_loop(1, n_g_ref[i * subs + sub], more, 0)


def _expert_kernel(tile_e_ref, n_live_ref, x_ref, w1_ref, w3_ref, w2_ref, o_ref,
                   w13b_ref, w2b_ref):
    k = pl.program_id(0)
    n_tiles = pl.num_programs(0) - 1
    d_ff = w2b_ref.shape[0]
    tile = k - 1
    live = (tile >= 0) & (tile < n_live_ref[0])
    expert_of = lambda t: tile_e_ref[jnp.clip(t, 0, n_tiles - 1)]

    @pl.when((tile == 0) | ((tile > 0) & (expert_of(tile) != expert_of(tile - 1))))
    def _():
        w13b_ref[:, :d_ff] = w1_ref[0, 0].astype(BF16)
        w13b_ref[:, d_ff:] = w3_ref[0, 0].astype(BF16)

    @pl.when(live)
    def _():
        ab = jnp.dot(x_ref[...], w13b_ref[...], preferred_element_type=F32)
        a, b = ab[:, :d_ff], ab[:, d_ff:]
        g = (a * jax.nn.sigmoid(a) * b).astype(BF16)
        o_ref[...] = jnp.dot(g, w2b_ref[...], preferred_element_type=F32).astype(BF16)

    @pl.when((tile >= 0) & jnp.logical_not(live))
    def _():
        o_ref[...] = jnp.zeros(o_ref.shape, BF16)

    @pl.when((k == 0) | ((k < n_tiles) & (expert_of(k) != expert_of(tile))))
    def _():
        w2b_ref[...] = w2_ref[0, 0].astype(BF16)


def _combine_kernel(wstart_ref, base_ref, x_ref, mod_ref, rank_ref, w_ref, fg_ref, *rest, final):
    n_win = COMBINE_CHUNKS * N_EXPERTS
    y_refs, o_ref = rest[:n_win], rest[n_win]
    i = pl.program_id(0)
    lane = lax.broadcasted_iota(jnp.int32, (CNT_CHUNK, WINDOW), 1)
    gate = mod_ref[0][:, 2 * D_MODEL:]
    chunks = []
    for s in range(COMBINE_CHUNKS):
        rows = slice(s * CNT_CHUNK, (s + 1) * CNT_CHUNK)
        rank = rank_ref[rows, :]
        wts = w_ref[rows, :]
        acc = jnp.zeros((CNT_CHUNK, D_MODEL), F32)
        for e in range(N_EXPERTS):
            w = s * N_EXPERTS + e
            rk = rank[:, e:e + 1]
            slot = rk + base_ref[e].astype(F32)
            window_slot = (wstart_ref[i * n_win + w] + lane).astype(F32)
            hit = (rk >= 0.0) & (slot == window_slot)
            p = jnp.where(hit, wts[:, e:e + 1], 0.0).astype(BF16)
            acc = acc + jnp.dot(p, y_refs[w][...], preferred_element_type=F32)
        y = x_ref[rows, :] + gate * acc
        if final:
            ms = jnp.mean(y * y, axis=-1, keepdims=True)
            y = y * lax.rsqrt(ms + EPS) * fg_ref[...]
        chunks.append(y)
    y = jnp.concatenate(chunks, axis=0)
    if final:
        y = jnp.swapaxes(y.reshape(SUBLANES, GROUPS, D_MODEL), 0, 1).reshape(TS, D_MODEL)
    o_ref[...] = y


def _route_metadata(cnt_end, n_tok):
    i32 = jnp.int32
    cnt_end = cnt_end[:, :N_EXPERTS]
    cnt_start = jnp.concatenate([jnp.zeros((1, N_EXPERTS), i32), cnt_end[:-1]], axis=0)
    n_chunks = cnt_end.shape[0]
    total = cnt_end[-1]
    n_tiles = (total + SLOT_TILE - 1) // SLOT_TILE
    tile_end = jnp.cumsum(n_tiles)
    tile_start = tile_end - n_tiles
    base = (tile_start * SLOT_TILE).astype(i32)
    max_tiles = 2 * n_tok // SLOT_TILE + N_EXPERTS
    i = jnp.arange(max_tiles, dtype=i32)
    tile_e = jnp.minimum(jnp.sum(i[:, None] >= tile_end[None, :], axis=1), N_EXPERTS - 1)
    subs = SLOT_TILE // GATHER_SLOTS
    q = jnp.arange(max_tiles * subs, dtype=i32)
    q_e = tile_e[q // subs]
    ls0 = (q // subs - tile_start[q_e]) * SLOT_TILE + (q % subs) * GATHER_SLOTS
    live = (q // subs < tile_end[-1]) & (ls0 < total[q_e])
    ends = cnt_end[:, q_e].T
    starts = cnt_start[:, q_e].T
    c_lo = jnp.sum(ends <= ls0[:, None], axis=1)
    c_hi = jnp.sum(starts < (ls0 + GATHER_SLOTS)[:, None], axis=1) - 1
    n_g = jnp.where(live, (c_hi - c_lo + GATHER_CHUNKS) // GATHER_CHUNKS, 0)
    c_lo = jnp.minimum(c_lo, n_chunks - 1)
    n_slots = max_tiles * SLOT_TILE
    wstart = jnp.minimum((base[None, :] + cnt_start) // ROW_ALIGN * ROW_ALIGN, n_slots - WINDOW)
    as_i32 = lambda a: a.astype(i32)
    return (as_i32(tile_e), as_i32(tile_end[-1:]), as_i32(ls0), as_i32(c_lo), as_i32(n_g), base,
            as_i32(wstart).reshape(-1))


def _moe_ffn(x, mods, layer, g, wr, br, w1, w3, w2, final_gain, final):
    batch, seq, _ = x.shape
    n_tok = batch * seq
    d_ff = w1.shape[3]
    j = layer // 2
    xt = x.reshape(n_tok, D_MODEL)
    params = lambda: pltpu.CompilerParams(
        dimension_semantics=("arbitrary",), vmem_limit_bytes=VMEM_LIMIT)

    route_tiles_per_seq = seq // ROUTE_TILE
    h, rank, wts, rank_row, cnt_end = pl.pallas_call(
        _router_kernel,
        grid=(n_tok // ROUTE_TILE,),
        in_specs=[
            pl.BlockSpec((ROUTE_TILE, D_MODEL), lambda i: (i, 0)),
            pl.BlockSpec((1, 1, 3 * D_MODEL),
                         lambda i: (layer * PAD_B + i // route_tiles_per_seq, 0, 0)),
            _layer_spec((1, D_MODEL), layer),
            _layer_spec((D_MODEL, LANES), j),
            _layer_spec((1, LANES), j),
        ],
        out_specs=[
            pl.BlockSpec((ROUTE_TILE, D_MODEL), lambda i: (i, 0)),
            pl.BlockSpec((ROUTE_TILE, LANES), lambda i: (i, 0)),
            pl.BlockSpec((ROUTE_TILE, LANES), lambda i: (i, 0)),
            pl.BlockSpec((N_EXPERTS, ROUTE_TILE), lambda i: (0, i)),
            pl.BlockSpec((ROUTE_TILE // CNT_CHUNK, LANES), lambda i: (i, 0)),
        ],
        out_shape=[
            jax.ShapeDtypeStruct((n_tok, D_MODEL), BF16),
            jax.ShapeDtypeStruct((n_tok, LANES), F32),
            jax.ShapeDtypeStruct((n_tok, LANES), F32),
            jax.ShapeDtypeStruct((N_EXPERTS, n_tok), F32),
            jax.ShapeDtypeStruct((n_tok // CNT_CHUNK, LANES), jnp.int32),
        ],
        scratch_shapes=[pltpu.VMEM((SUBLANES, LANES), F32)],
        compiler_params=params(),
        name="moe_router",
    )(xt, mods, g, wr, br)

    tile_e, n_live, ls0, c_lo, n_g, base, wstart = _route_metadata(cnt_end, n_tok)
    max_tiles = tile_e.shape[0]
    n_slots = max_tiles * SLOT_TILE

    x_sorted = pl.pallas_call(
        _dispatch_kernel,
        grid_spec=pltpu.PrefetchScalarGridSpec(
            num_scalar_prefetch=4,
            grid=(max_tiles,),
            in_specs=[pl.BlockSpec(memory_space=pltpu.VMEM),
                      pl.BlockSpec(memory_space=pltpu.VMEM)],
            out_specs=pl.BlockSpec((SLOT_TILE, D_MODEL), lambda i, *_: (i, 0)),
        ),
        out_shape=jax.ShapeDtypeStruct((n_slots, D_MODEL), BF16),
        compiler_params=params(),
        name="moe_dispatch",
    )(tile_e, ls0, c_lo, n_g, h, rank_row.reshape(N_EXPERTS, n_tok // CNT_CHUNK, CNT_CHUNK))

    this_expert = lambda k, te, nl: (j, te[jnp.maximum(k - 1, 0)], 0, 0)
    next_expert = lambda k, te, nl: (j, te[jnp.minimum(k, max_tiles - 1)], 0, 0)
    y_sorted = pl.pallas_call(
        _expert_kernel,
        grid_spec=pltpu.PrefetchScalarGridSpec(
            num_scalar_prefetch=2,
            grid=(max_tiles + 1,),
            in_specs=[
                pl.BlockSpec((SLOT_TILE, D_MODEL), lambda k, te, nl: (jnp.maximum(k - 1, 0), 0)),
                pl.BlockSpec((1, 1, D_MODEL, d_ff), this_expert),
                pl.BlockSpec((1, 1, D_MODEL, d_ff), this_expert),
                pl.BlockSpec((1, 1, d_ff, D_MODEL), next_expert),
            ],
            out_specs=pl.BlockSpec((SLOT_TILE, D_MODEL),
                                   lambda k, te, nl: (jnp.maximum(k - 1, 0), 0)),
            scratch_shapes=[pltpu.VMEM((D_MODEL, 2 * d_ff), BF16),
                            pltpu.VMEM((d_ff, D_MODEL), BF16)],
        ),
        out_shape=jax.ShapeDtypeStruct((n_slots, D_MODEL), BF16),
        compiler_params=params(),
        name="moe_experts",
    )(tile_e, n_live, x_sorted, w1, w3, w2)

    rows = COMBINE_CHUNKS * CNT_CHUNK
    assert rows == TS
    steps_per_seq = seq // rows
    n_win = COMBINE_CHUNKS * N_EXPERTS
    window_spec = lambda w: pl.BlockSpec(
        (pl.Element(WINDOW), pl.Element(D_MODEL)),
        lambda i, ws, bs: (pl.multiple_of(ws[i * n_win + w], ROW_ALIGN), 0))
    out = pl.pallas_call(
        functools.partial(_combine_kernel, final=final),
        grid_spec=pltpu.PrefetchScalarGridSpec(
            num_scalar_prefetch=2,
            grid=(n_tok // rows,),
            in_specs=[
                pl.BlockSpec((rows, D_MODEL), lambda i, ws, bs: (i, 0)),
                pl.BlockSpec((1, 1, 3 * D_MODEL),
                             lambda i, ws, bs: (layer * PAD_B + i // steps_per_seq, 0, 0)),
                pl.BlockSpec((rows, LANES), lambda i, ws, bs: (i, 0)),
                pl.BlockSpec((rows, LANES), lambda i, ws, bs: (i, 0)),
                pl.BlockSpec((1, D_MODEL), lambda i, ws, bs: (0, 0)),
            ] + [window_spec(w) for w in range(n_win)],
            out_specs=pl.BlockSpec((rows, D_MODEL), lambda i, ws, bs: (i, 0)),
        ),
        out_shape=jax.ShapeDtypeStruct((n_tok, D_MODEL), F32),
        compiler_params=params(),
        name="moe_combine",
    )(wstart, base, xt, mods, rank, wts, final_gain, *([y_sorted] * n_win))
    return out.reshape(x.shape)


def _block_diag_halves(w):
    depth, heads, hd, _ = w.shape
    per_tile = MXU_DIM // hd
    w = w.reshape(depth, heads // per_tile, per_tile, hd, hd)
    eye = jnp.eye(per_tile, dtype=w.dtype)
    bd = jnp.einsum("dtiab,ij->dtiajb", w, eye)
    return bd.reshape(depth, heads // per_tile, MXU_DIM, MXU_DIM)


def kernel(x, c, mix_norm, mix_mod_w, mix_mod_b, w_in, lru_conv_w, lru_conv_b, lru_wa, lru_ba,
           lru_wi, lru_bi, lru_lambda, sc_conv_w, gn_lru, gn_sc, w_out, ffn_norm, ffn_mod_w,
           ffn_mod_b, dense_w1, dense_w3, dense_w2, router_w, router_b, exp_w1, exp_w3, exp_w2,
           final_norm):
    depth = w_in.shape[0]
    batch = x.shape[0]
    c_pad = jnp.pad(c, ((0, PAD_B - batch), (0, 0)))
    mix_mods = _modulation(c_pad, mix_mod_w, mix_mod_b).reshape(depth * PAD_B, 1, 3 * D_MODEL)
    ffn_mods = _modulation(c_pad, ffn_mod_w, ffn_mod_b).reshape(depth * PAD_B, 1, 3 * D_MODEL)

    row = lambda p: p.reshape(depth, 1, p.shape[-1])
    wg = jnp.concatenate([_block_diag_halves(lru_wa), _block_diag_halves(lru_wi)],
                         axis=-1).astype(BF16)
    bg = jnp.stack([lru_ba.reshape(depth, D_LRU), lru_bi.reshape(depth, D_LRU)], axis=1)
    head = jnp.arange(MXU_DIM) // HEAD_DIM
    gmat = jnp.where(head[:, None] == head[None, :], 1.0 / HEAD_DIM, 0.0).astype(BF16)
    w_in_b = w_in.astype(BF16)
    w_out_b = w_out.astype(BF16)
    dense_w1_b, dense_w3_b, dense_w2_b = (w.astype(BF16) for w in (dense_w1, dense_w3, dense_w2))
    n_moe = router_w.shape[0]
    wr = jnp.pad(router_w, ((0, 0), (0, 0), (0, LANES - N_EXPERTS)))
    br = jnp.pad(router_b, ((0, 0), (0, LANES - N_EXPERTS)),
                 constant_values=-jnp.inf).reshape(n_moe, 1, LANES)
    final_gain = final_norm.reshape(1, D_MODEL)
    mix_g, ffn_g = row(mix_norm), row(ffn_norm)
    lcb, lam, gnl, gns = row(lru_conv_b), row(lru_lambda), row(gn_lru), row(gn_sc)

    for l in range(depth):
        x = _mixer(x, mix_mods, l, mix_g, w_in_b, lru_conv_w, lcb, wg, bg, lam, sc_conv_w, gnl,
                   gns, gmat, w_out_b, natural_in=(l == 0))
        if l % 2 == 0:
            x = _dense_ffn(x, ffn_mods, l, ffn_g, dense_w1_b, dense_w3_b, dense_w2_b)
        else:
            x = _moe_ffn(x, ffn_mods, l, ffn_g, wr, br, exp_w1, exp_w3, exp_w2, final_gain,
                         final=(l == depth - 1))
    if depth % 2 == 1:
        raise NotImplementedError("final norm is fused into the last MoE layer")
    return x
```

```python
import functools

import jax
import jax.numpy as jnp
from jax import lax
from jax.experimental import pallas as pl
from jax.experimental.pallas import tpu as pltpu

F32 = jnp.float32
BF16 = jnp.bfloat16

D_MODEL = 1024
D_LRU = 512
D_SC = 512
D_IN = 2 * D_LRU + 3 * D_SC
HEAD_DIM = 64
LRU_CONV = 4
SC_CONV = 3
LRU_C = 8.0
N_EXPERTS = 8
EPS = 1e-6

SUBLANES = 8
LANES = 128
MXU_DIM = 256
TS = SUBLANES ** 3
GROUPS = TS // SUBLANES
MIXER_TILES = 2
PAD_B = 8
VMEM_LIMIT = 56 * 1024 * 1024


def _mod_norm(x, g, scale, shift):
    ms = jnp.mean(x * x, axis=-1, keepdims=True)
    return x * lax.rsqrt(ms + EPS) * (g * (1.0 + scale)) + shift


def _split_mod(mod):
    return mod[:, :D_MODEL], mod[:, D_MODEL:2 * D_MODEL], mod[:, 2 * D_MODEL:]


def _layer_spec(shape, layer):
    zeros = (0,) * len(shape)
    return pl.BlockSpec((1, *shape), lambda *_: (layer, *zeros), pipeline_mode=pl.Buffered(1))


def _mod_kernel(c_ref, w_ref, b_ref, o_ref):
    c = c_ref[...]
    s = (c * jax.nn.sigmoid(c)).astype(BF16)
    o_ref[0] = jnp.dot(s, w_ref[0].astype(BF16), preferred_element_type=F32) + b_ref[0]


def _modulation(c_pad, w, b):
    depth = w.shape[0]
    n_parts = 3
    return pl.pallas_call(
        _mod_kernel,
        grid=(depth, n_parts),
        in_specs=[
            pl.BlockSpec((PAD_B, D_MODEL), lambda l, j: (0, 0)),
            pl.BlockSpec((1, D_MODEL, D_MODEL), lambda l, j: (l, 0, j)),
            pl.BlockSpec((1, 1, D_MODEL), lambda l, j: (l, 0, j)),
        ],
        out_specs=pl.BlockSpec((1, PAD_B, D_MODEL), lambda l, j: (l, 0, j)),
        out_shape=jax.ShapeDtypeStruct((depth, PAD_B, 3 * D_MODEL), F32),
        compiler_params=pltpu.CompilerParams(
            dimension_semantics=("arbitrary", "arbitrary"), vmem_limit_bytes=VMEM_LIMIT),
        name="adaln_mod",
    )(c_pad, w, b.reshape(depth, 1, 3 * D_MODEL))


def _scan8(a, u, row_in_group):
    for d in (1, 2, 4):
        keep = row_in_group >= d
        a_sh = jnp.where(keep, pltpu.roll(a, d, axis=0), 1.0)
        u_sh = jnp.where(keep, pltpu.roll(u, d, axis=0), 0.0)
        u = a * u_sh + u
        a = a * a_sh
    return a, u


def _row_in_group(rows, cols):
    return lax.broadcasted_iota(jnp.int32, (rows, cols), 0) % SUBLANES


def _shift_rows_down(x, first_row):
    row = lax.broadcasted_iota(jnp.int32, x.shape, 0)
    return jnp.where(row == 0, first_row, pltpu.roll(x, 1, axis=0))


def _group_rms(y, gmat_ref, gain):
    y2 = (y * y).astype(BF16)
    ms = jnp.concatenate(
        [jnp.dot(y2[:, :MXU_DIM], gmat_ref[...], preferred_element_type=F32),
         jnp.dot(y2[:, MXU_DIM:], gmat_ref[...], preferred_element_type=F32)], axis=-1)
    return y * lax.rsqrt(ms + EPS) * gain


def _mixer_kernel(*refs, natural_in):
    n_x = D_MODEL // LANES if natural_in else 1
    x_refs, refs = refs[:n_x], refs[n_x:]
    (mod_ref, g_ref, win_ref, lcw_ref, lcb_ref, wg_ref, bg_ref, lam_ref, scw_ref, gnl_ref,
     gns_ref, gmat_ref, wout_ref, o_ref, xl_buf, v_buf, a0_ref, u0_ref, a1_ref, u1_ref, b64_ref,
     carry_ref) = refs

    @pl.when(pl.program_id(1) == 0)
    def _():
        xl_buf[...] = jnp.zeros(xl_buf.shape, F32)
        v_buf[...] = jnp.zeros(v_buf.shape, F32)
        carry_ref[...] = jnp.zeros(carry_ref.shape, F32)

    for sub in range(MIXER_TILES):
        r0 = sub * TS
        if natural_in:
            x = jnp.concatenate(
                [jnp.concatenate([r[0, pl.ds(r0 + s, GROUPS, stride=SUBLANES), :] for r in x_refs],
                                 axis=1) for s in range(SUBLANES)], axis=0)
        else:
            x = x_refs[0][0, r0:r0 + TS, :]
        o_ref[0, r0:r0 + TS, :] = _mixer_tile(
            x, mod_ref[0], g_ref, win_ref, lcw_ref, lcb_ref, wg_ref, bg_ref, lam_ref, scw_ref,
            gnl_ref, gns_ref, gmat_ref, wout_ref, xl_buf, v_buf, a0_ref, u0_ref, a1_ref, u1_ref,
            b64_ref, carry_ref)


def _slabs(y):
    return [y[s * GROUPS:(s + 1) * GROUPS, :] for s in range(SUBLANES)]


def _causal_conv(y, w_ref, hist_ref):
    taps = w_ref.shape[1]
    slabs = _slabs(y)
    early = {j - SUBLANES: _shift_rows_down(slabs[j], hist_ref[j:j + 1, :])
             for j in range(SUBLANES - taps + 1, SUBLANES)}
    at = lambda j: slabs[j] if j >= 0 else early[j]
    out = []
    for s in range(SUBLANES):
        acc = w_ref[0, 0:1, :] * at(s - (taps - 1))
        for k in range(1, taps):
            acc = acc + w_ref[0, k:k + 1, :] * at(s - (taps - 1) + k)
        out.append(acc)
    for j in range(SUBLANES - taps + 1, SUBLANES):
        hist_ref[j:j + 1, :] = slabs[j][GROUPS - 1:, :]
    return jnp.concatenate(out, axis=0)


def _mixer_tile(x, mod, g_ref, win_ref, lcw_ref, lcb_ref, wg_ref, bg_ref, lam_ref, scw_ref,
                gnl_ref, gns_ref, gmat_ref, wout_ref, xl_buf, v_buf, a0_ref, u0_ref, a1_ref,
                u1_ref, b64_ref, carry_ref):
    n2 = GROUPS // SUBLANES
    shift, scale, gate = _split_mod(mod)
    h = _mod_norm(x, g_ref[0], scale, shift).astype(BF16)
    def in_proj(lo, hi):
        return jnp.dot(h, win_ref[0, :, lo:hi], preferred_element_type=F32)

    xc = _causal_conv(in_proj(0, D_LRU), lcw_ref, xl_buf) + lcb_ref[0]

    cx_sc = in_proj(2 * D_LRU + D_SC, D_IN)
    xcb = xc.astype(BF16)
    g0 = jnp.dot(xcb[:, :MXU_DIM], wg_ref[0, 0], preferred_element_type=F32)
    g1 = jnp.dot(xcb[:, MXU_DIM:], wg_ref[0, 1], preferred_element_type=F32)
    cv = _causal_conv(cx_sc[:, :D_SC] * cx_sc[:, D_SC:], scw_ref, v_buf)

    b_sc = in_proj(2 * D_LRU, 2 * D_LRU + D_SC)
    r = jax.nn.sigmoid(jnp.concatenate([g0[:, :MXU_DIM], g1[:, :MXU_DIM]], axis=-1)
                       + bg_ref[0, 0:1, :])
    i = jax.nn.sigmoid(jnp.concatenate([g0[:, MXU_DIM:], g1[:, MXU_DIM:]], axis=-1)
                       + bg_ref[0, 1:2, :])
    log_a = r * ((-LRU_C) * jax.nn.softplus(-lam_ref[0]))
    a = jnp.exp(log_a)
    uin = jnp.sqrt(-jnp.tanh(log_a) * (1.0 + a * a)) * (i * xc)
    yn_sc = _group_rms(b_sc * cv, gmat_ref, gns_ref[0]).astype(BF16)
    out_sc = jnp.dot(yn_sc, wout_ref[0, D_LRU:, :], preferred_element_type=F32)

    gate_lru = jax.nn.gelu(in_proj(D_LRU, 2 * D_LRU))

    last = SUBLANES - 1
    hs_blocks = []
    for lb in range(D_LRU // LANES):
        lanes = slice(lb * LANES, (lb + 1) * LANES)
        h_prev = carry_ref[lb, last:last + 1, :]
        acc_a = acc_u = None
        for s, (a_s, u_s) in enumerate(zip(_slabs(a[:, lanes]), _slabs(uin[:, lanes]))):
            if s == 0:
                acc_a, acc_u = a_s, u_s
            else:
                acc_u = a_s * acc_u + u_s
                acc_a = a_s * acc_a
            a0_ref[lb, s * GROUPS:(s + 1) * GROUPS, :] = acc_a
            u0_ref[lb, s * GROUPS:(s + 1) * GROUPS, :] = acc_u
        a1, u1 = _scan8(acc_a, acc_u, _row_in_group(GROUPS, LANES))
        a1_ref[lb] = a1
        u1_ref[lb] = u1
        a2, u2 = _scan8(a1_ref[lb, pl.ds(last, n2, stride=SUBLANES), :],
                        u1_ref[lb, pl.ds(last, n2, stride=SUBLANES), :],
                        _row_in_group(n2, LANES))
        h2 = u2 + a2 * h_prev
        carry_ref[lb] = h2
        b64_ref[lb] = _shift_rows_down(h2, h_prev)
        before64 = jnp.concatenate(
            [jnp.broadcast_to(b64_ref[lb, j:j + 1, :], (SUBLANES, LANES)) for j in range(n2)],
            axis=0)
        h1 = u1 + a1 * before64
        before8 = _shift_rows_down(h1, h_prev)
        hs_blocks.append(jnp.concatenate(
            [u0_ref[lb, s * GROUPS:(s + 1) * GROUPS, :]
             + a0_ref[lb, s * GROUPS:(s + 1) * GROUPS, :] * before8 for s in range(SUBLANES)],
            axis=0))
    hs = jnp.concatenate(hs_blocks, axis=-1)

    yn_lru = _group_rms(gate_lru * hs, gmat_ref, gnl_ref[0]).astype(BF16)
    out = out_sc + jnp.dot(yn_lru, wout_ref[0, :D_LRU, :], preferred_element_type=F32)
    return x + gate * out


def _mixer(x, mods, layer, g, w_in, lcw, lcb, wg, bg, lam, scw, gnl, gns, gmat, w_out, natural_in):
    batch, seq, _ = x.shape
    ts = TS
    nlb = D_LRU // LANES
    per_layer = lambda shape: _layer_spec(shape, layer)
    rows = MIXER_TILES * ts
    tile_spec = pl.BlockSpec((1, rows, D_MODEL), lambda b, t: (b, t, 0))
    if natural_in:
        x_specs = [pl.BlockSpec((1, rows, LANES), lambda b, t, l=l: (b, t, l))
                   for l in range(D_MODEL // LANES)]
    else:
        x_specs = [tile_spec]
    return pl.pallas_call(
        functools.partial(_mixer_kernel, natural_in=natural_in),
        grid=(batch, seq // rows),
        in_specs=x_specs + [
            pl.BlockSpec((1, 1, 3 * D_MODEL), lambda b, t: (layer * PAD_B + b, 0, 0)),
            per_layer((1, D_MODEL)),
            per_layer((D_MODEL, D_IN)),
            per_layer((LRU_CONV, D_LRU)),
            per_layer((1, D_LRU)),
            per_layer((2, MXU_DIM, 2 * MXU_DIM)),
            per_layer((2, D_LRU)),
            per_layer((1, D_LRU)),
            per_layer((SC_CONV, D_SC)),
            per_layer((1, D_LRU)),
            per_layer((1, D_SC)),
            pl.BlockSpec((MXU_DIM, MXU_DIM), lambda b, t: (0, 0)),
            per_layer((D_LRU + D_SC, D_MODEL)),
        ],
        out_specs=tile_spec,
        out_shape=jax.ShapeDtypeStruct(x.shape, F32),
        scratch_shapes=[
            pltpu.VMEM((SUBLANES, D_LRU), F32),
            pltpu.VMEM((SUBLANES, D_SC), F32),
            pltpu.VMEM((nlb, ts, LANES), F32),
            pltpu.VMEM((nlb, ts, LANES), F32),
            pltpu.VMEM((nlb, GROUPS, LANES), F32),
            pltpu.VMEM((nlb, GROUPS, LANES), F32),
            pltpu.VMEM((nlb, SUBLANES, LANES), F32),
            pltpu.VMEM((nlb, SUBLANES, LANES), F32),
        ],
        compiler_params=pltpu.CompilerParams(
            dimension_semantics=("arbitrary", "arbitrary"), vmem_limit_bytes=VMEM_LIMIT),
        name="mixer",
    )(*([x] * len(x_specs)), mods, g, w_in, lcw, lcb, wg, bg, lam, scw, gnl, gns, gmat, w_out)


def _swiglu(h, w1, w3, w2):
    a = jnp.dot(h, w1, preferred_element_type=F32)
    b = jnp.dot(h, w3, preferred_element_type=F32)
    g = (a * jax.nn.sigmoid(a) * b).astype(BF16)
    return jnp.dot(g, w2, preferred_element_type=F32)


def _mxu_halves(width):
    split = pl.cdiv(width // 2, MXU_DIM) * MXU_DIM
    return slice(0, split), slice(split, width)


def _dense_ffn_kernel(x_ref, mod_ref, g_ref, w1_ref, w3_ref, w2_ref, o_ref):
    x = x_ref[0]
    shift, scale, gate = _split_mod(mod_ref[0])
    h = _mod_norm(x, g_ref[0], scale, shift).astype(BF16)
    f = jnp.zeros(x.shape, F32)
    for cols in _mxu_halves(w1_ref.shape[2]):
        f = f + _swiglu(h, w1_ref[0, :, cols], w3_ref[0, :, cols], w2_ref[0, cols, :])
    o_ref[0] = x + gate * f


def _dense_ffn(x, mods, layer, g, w1, w3, w2, tm=1024):
    batch, seq, _ = x.shape
    d_ff = w1.shape[2]
    j = layer // 2
    return pl.pallas_call(
        _dense_ffn_kernel,
        grid=(batch, seq // tm),
        in_specs=[
            pl.BlockSpec((1, tm, D_MODEL), lambda b, t: (b, t, 0)),
            pl.BlockSpec((1, 1, 3 * D_MODEL), lambda b, t: (layer * PAD_B + b, 0, 0)),
            _layer_spec((1, D_MODEL), layer),
            _layer_spec((D_MODEL, d_ff), j),
            _layer_spec((D_MODEL, d_ff), j),
            _layer_spec((d_ff, D_MODEL), j),
        ],
        out_specs=pl.BlockSpec((1, tm, D_MODEL), lambda b, t: (b, t, 0)),
        out_shape=jax.ShapeDtypeStruct(x.shape, F32),
        compiler_params=pltpu.CompilerParams(
            dimension_semantics=("arbitrary", "arbitrary"), vmem_limit_bytes=VMEM_LIMIT),
        name="dense_ffn",
    )(x, mods, g, w1, w3, w2)


ROUTE_TILE = 1024
CNT_CHUNK = 128
COMBINE_CHUNKS = 4
SLOT_TILE = 512
GATHER_SLOTS = 128
GATHER_CHUNKS = 6
ROW_ALIGN = 16
WINDOW = CNT_CHUNK + ROW_ALIGN


def _top2(logits):
    lane = lax.broadcasted_iota(jnp.int32, logits.shape, 1)
    m1 = jnp.max(logits, axis=-1, keepdims=True)
    i1 = jnp.min(jnp.where(logits == m1, lane, LANES), axis=-1, keepdims=True)
    rest = jnp.where(lane == i1, -jnp.inf, logits)
    m2 = jnp.max(rest, axis=-1, keepdims=True)
    i2 = jnp.min(jnp.where(rest == m2, lane, LANES), axis=-1, keepdims=True)
    e2 = jnp.exp(m2 - m1)
    w_first = 1.0 / (1.0 + e2)
    w_second = e2 / (1.0 + e2)
    first = lane == i1
    second = lane == i2
    weights = jnp.where(first, w_first, 0.0) + jnp.where(second, w_second, 0.0)
    return weights, first | second


def _split_bf16(v):
    hi = v.astype(BF16)
    return hi, (v - hi.astype(F32)).astype(BF16)


def _router_kernel(x_ref, mod_ref, g_ref, wr_ref, br_ref,
                   h_ref, rank_ref, w_ref, rank_row_ref, cnt_ref, carry_ref):
    n = ROUTE_TILE
    n_chunks = n // CNT_CHUNK

    @pl.when(pl.program_id(0) == 0)
    def _():
        carry_ref[...] = jnp.zeros(carry_ref.shape, F32)

    shift, scale, _ = _split_mod(mod_ref[0])
    h = _mod_norm(x_ref[...], g_ref[0], scale, shift)
    h_hi, h_lo = _split_bf16(h)
    w_hi, w_lo = _split_bf16(wr_ref[0])
    both = jnp.dot(h_hi, jnp.concatenate([w_hi, w_lo], axis=1), preferred_element_type=F32)
    logits = (both[:, :LANES] + both[:, LANES:]
              + jnp.dot(h_lo, w_hi, preferred_element_type=F32)) + br_ref[0]
    weights, mask = _top2(logits)
    m = jnp.where(mask, 1.0, 0.0).astype(BF16)
    before = carry_ref[0:1, :]
    chunk = lax.broadcasted_iota(jnp.int32, (n_chunks, n), 0)
    tok = lax.broadcasted_iota(jnp.int32, (n_chunks, n), 1)
    in_earlier_chunk = jnp.where(tok < chunk * CNT_CHUNK, 1.0, 0.0).astype(BF16)
    upto_chunk_end = jnp.where(tok < (chunk + 1) * CNT_CHUNK, 1.0, 0.0).astype(BF16)
    chunk_start = jnp.dot(in_earlier_chunk, m, preferred_element_type=F32) + before
    cnt = jnp.dot(upto_chunk_end, m, preferred_element_type=F32) + before
    row = lax.broadcasted_iota(jnp.int32, (CNT_CHUNK, CNT_CHUNK), 0)
    col = lax.broadcasted_iota(jnp.int32, (CNT_CHUNK, CNT_CHUNK), 1)
    strictly_lower = jnp.where(row > col, 1.0, 0.0).astype(BF16)
    in_chunk = jnp.concatenate(
        [jnp.dot(strictly_lower, m[j * CNT_CHUNK:(j + 1) * CNT_CHUNK, :],
                 preferred_element_type=F32) + chunk_start[j:j + 1, :]
         for j in range(n_chunks)], axis=0)
    rank = jnp.where(mask, in_chunk, -1.0)
    carry_ref[...] = jnp.broadcast_to(cnt[n_chunks - 1:, :], carry_ref.shape)
    h_ref[...] = h_hi
    rank_ref[...] = rank
    w_ref[...] = weights
    rank_row_ref[...] = rank.T[:N_EXPERTS, :]
    cnt_ref[...] = cnt.astype(jnp.int32)


def _dispatch_kernel(tile_e_ref, ls0_ref, c_lo_ref, n_g_ref, h_ref, rank_ref, *o_refs):
    i = pl.program_id(0)
    e = tile_e_ref[i]
    n_chunks = rank_ref.shape[1]
    span = GATHER_CHUNKS * CNT_CHUNK
    subs = SLOT_TILE // GATHER_SLOTS

    def gather(sub, k):
        q = i * subs + sub
        slot = (ls0_ref[q]
                + lax.broadcasted_iota(jnp.int32, (GATHER_SLOTS, CNT_CHUNK), 0)).astype(F32)
        wanted = c_lo_ref[q] + k * GATHER_CHUNKS
        start = jnp.minimum(wanted, n_chunks - GATHER_CHUNKS)
        p = jnp.concatenate(
            [jnp.where((start + j >= wanted) & (rank_ref[e, pl.ds(start + j, 1), :] == slot),
                       1.0, 0.0) for j in range(GATHER_CHUNKS)], axis=1).astype(BF16)
        tokens = pl.ds(pl.multiple_of(start * CNT_CHUNK, CNT_CHUNK), span)
        return jnp.dot(p, h_ref[tokens, :], preferred_element_type=F32).astype(BF16)

    width = D_MODEL // len(o_refs)
    for sub in range(subs):
        rows = gather(sub, 0)
        for c, o_ref in enumerate(o_refs):
            o_ref[sub * GATHER_SLOTS:(sub + 1) * GATHER_SLOTS, :] = rows[:, c * width:(c + 1) * width]
    for sub in range(subs):
        def more(k, carry, sub=sub):
            rows = gather(sub, k)
            for c, o_ref in enumerate(o_refs):
                o_ref[sub * GATHER_SLOTS:(sub + 1) * GATHER_SLOTS, :] += (
                    rows[:, c * width:(c + 1) * width])
            return carry

        lax.fori_loop(1, n_g_ref[i * subs + sub], more, 0)


def _expert_kernel(tile_e_ref, n_live_ref, xa_ref, xb_ref, w1_ref, w3_ref, w2_ref, o_ref,
                   w13b_ref, w2b_ref):
    k = pl.program_id(0)
    n_tiles = pl.num_programs(0) - 1
    d_ff = w2b_ref.shape[0]
    tile = k - 1
    live = (tile >= 0) & (tile < n_live_ref[0])
    expert_of = lambda t: tile_e_ref[jnp.clip(t, 0, n_tiles - 1)]

    @pl.when((tile == 0) | ((tile > 0) & (expert_of(tile) != expert_of(tile - 1))))
    def _():
        w13b_ref[:, :d_ff] = w1_ref[0, 0].astype(BF16)
        w13b_ref[:, d_ff:] = w3_ref[0, 0].astype(BF16)

    @pl.when(live)
    def _():
        x = jnp.concatenate([xa_ref[...], xb_ref[...]], axis=1)
        ab = jnp.dot(x, w13b_ref[...], preferred_element_type=F32)
        a, b = ab[:, :d_ff], ab[:, d_ff:]
        g = (a * jax.nn.sigmoid(a) * b).astype(BF16)
        o_ref[...] = jnp.dot(g, w2b_ref[...], preferred_element_type=F32).astype(BF16)

    @pl.when((tile >= 0) & jnp.logical_not(live))
    def _():
        o_ref[...] = jnp.zeros(o_ref.shape, BF16)

    @pl.when((k == 0) | ((k < n_tiles) & (expert_of(k) != expert_of(tile))))
    def _():
        w2b_ref[...] = w2_ref[0, 0].astype(BF16)


def _combine_kernel(wstart_ref, base_ref, x_ref, mod_ref, rank_ref, w_ref, fg_ref, *rest, final):
    n_win = COMBINE_CHUNKS * N_EXPERTS
    y_refs, o_ref = rest[:n_win], rest[n_win]
    i = pl.program_id(0)
    lane = lax.broadcasted_iota(jnp.int32, (CNT_CHUNK, WINDOW), 1)
    gate = mod_ref[0][:, 2 * D_MODEL:]
    chunks = []
    for s in range(COMBINE_CHUNKS):
        rows = slice(s * CNT_CHUNK, (s + 1) * CNT_CHUNK)
        rank = rank_ref[rows, :]
        wts = w_ref[rows, :]
        acc = jnp.zeros((CNT_CHUNK, D_MODEL), F32)
        for e in range(N_EXPERTS):
            w = s * N_EXPERTS + e
            rk = rank[:, e:e + 1]
            slot = rk + base_ref[e].astype(F32)
            window_slot = (wstart_ref[i * n_win + w] + lane).astype(F32)
            hit = (rk >= 0.0) & (slot == window_slot)
            p = jnp.where(hit, wts[:, e:e + 1], 0.0).astype(BF16)
            acc = acc + jnp.dot(p, y_refs[w][...], preferred_element_type=F32)
        y = x_ref[rows, :] + gate * acc
        if final:
            ms = jnp.mean(y * y, axis=-1, keepdims=True)
            y = y * lax.rsqrt(ms + EPS) * fg_ref[...]
        chunks.append(y)
    y = jnp.concatenate(chunks, axis=0)
    if final:
        y = jnp.swapaxes(y.reshape(SUBLANES, GROUPS, D_MODEL), 0, 1).reshape(TS, D_MODEL)
    o_ref[...] = y


def _route_metadata(cnt_end, n_tok):
    i32 = jnp.int32
    cnt_end = cnt_end[:, :N_EXPERTS]
    cnt_start = jnp.concatenate([jnp.zeros((1, N_EXPERTS), i32), cnt_end[:-1]], axis=0)
    n_chunks = cnt_end.shape[0]
    total = cnt_end[-1]
    n_tiles = (total + SLOT_TILE - 1) // SLOT_TILE
    tile_end = jnp.cumsum(n_tiles)
    tile_start = tile_end - n_tiles
    base = (tile_start * SLOT_TILE).astype(i32)
    max_tiles = 2 * n_tok // SLOT_TILE + N_EXPERTS
    i = jnp.arange(max_tiles, dtype=i32)
    tile_e = jnp.minimum(jnp.sum(i[:, None] >= tile_end[None, :], axis=1), N_EXPERTS - 1)
    subs = SLOT_TILE // GATHER_SLOTS
    q = jnp.arange(max_tiles * subs, dtype=i32)
    q_e = tile_e[q // subs]
    ls0 = (q // subs - tile_start[q_e]) * SLOT_TILE + (q % subs) * GATHER_SLOTS
    live = (q // subs < tile_end[-1]) & (ls0 < total[q_e])
    ends = cnt_end[:, q_e].T
    starts = cnt_start[:, q_e].T
    c_lo = jnp.sum(ends <= ls0[:, None], axis=1)
    c_hi = jnp.sum(starts < (ls0 + GATHER_SLOTS)[:, None], axis=1) - 1
    n_g = jnp.where(live, (c_hi - c_lo + GATHER_CHUNKS) // GATHER_CHUNKS, 0)
    c_lo = jnp.minimum(c_lo, n_chunks - 1)
    n_slots = max_tiles * SLOT_TILE
    wstart = jnp.minimum((base[None, :] + cnt_start) // ROW_ALIGN * ROW_ALIGN, n_slots - WINDOW)
    as_i32 = lambda a: a.astype(i32)
    return (as_i32(tile_e), as_i32(tile_end[-1:]), as_i32(ls0), as_i32(c_lo), as_i32(n_g), base,
            as_i32(wstart).reshape(-1))


def _moe_ffn(x, mods, layer, g, wr, br, w1, w3, w2, final_gain, final):
    batch, seq, _ = x.shape
    n_tok = batch * seq
    d_ff = w1.shape[3]
    j = layer // 2
    xt = x.reshape(n_tok, D_MODEL)
    params = lambda: pltpu.CompilerParams(
        dimension_semantics=("arbitrary",), vmem_limit_bytes=VMEM_LIMIT)

    route_tiles_per_seq = seq // ROUTE_TILE
    h, rank, wts, rank_row, cnt_end = pl.pallas_call(
        _router_kernel,
        grid=(n_tok // ROUTE_TILE,),
        in_specs=[
            pl.BlockSpec((ROUTE_TILE, D_MODEL), lambda i: (i, 0)),
            pl.BlockSpec((1, 1, 3 * D_MODEL),
                         lambda i: (layer * PAD_B + i // route_tiles_per_seq, 0, 0)),
            _layer_spec((1, D_MODEL), layer),
            _layer_spec((D_MODEL, LANES), j),
            _layer_spec((1, LANES), j),
        ],
        out_specs=[
            pl.BlockSpec((ROUTE_TILE, D_MODEL), lambda i: (i, 0)),
            pl.BlockSpec((ROUTE_TILE, LANES), lambda i: (i, 0)),
            pl.BlockSpec((ROUTE_TILE, LANES), lambda i: (i, 0)),
            pl.BlockSpec((N_EXPERTS, ROUTE_TILE), lambda i: (0, i)),
            pl.BlockSpec((ROUTE_TILE // CNT_CHUNK, LANES), lambda i: (i, 0)),
        ],
        out_shape=[
            jax.ShapeDtypeStruct((n_tok, D_MODEL), BF16),
            jax.ShapeDtypeStruct((n_tok, LANES), F32),
            jax.ShapeDtypeStruct((n_tok, LANES), F32),
            jax.ShapeDtypeStruct((N_EXPERTS, n_tok), F32),
            jax.ShapeDtypeStruct((n_tok // CNT_CHUNK, LANES), jnp.int32),
        ],
        scratch_shapes=[pltpu.VMEM((SUBLANES, LANES), F32)],
        compiler_params=params(),
        name="moe_router",
    )(xt, mods, g, wr, br)

    tile_e, n_live, ls0, c_lo, n_g, base, wstart = _route_metadata(cnt_end, n_tok)
    max_tiles = tile_e.shape[0]
    n_slots = max_tiles * SLOT_TILE

    x_sorted = pl.pallas_call(
        _dispatch_kernel,
        grid_spec=pltpu.PrefetchScalarGridSpec(
            num_scalar_prefetch=4,
            grid=(max_tiles,),
            in_specs=[pl.BlockSpec(memory_space=pltpu.VMEM),
                      pl.BlockSpec(memory_space=pltpu.VMEM)],
            out_specs=[pl.BlockSpec((SLOT_TILE, D_MODEL // 2), lambda i, *_: (i, 0))] * 2,
        ),
        out_shape=[jax.ShapeDtypeStruct((n_slots, D_MODEL // 2), BF16)] * 2,
        compiler_params=params(),
        name="moe_dispatch",
    )(tile_e, ls0, c_lo, n_g, h, rank_row.reshape(N_EXPERTS, n_tok // CNT_CHUNK, CNT_CHUNK))

    this_expert = lambda k, te, nl: (j, te[jnp.maximum(k - 1, 0)], 0, 0)
    next_expert = lambda k, te, nl: (j, te[jnp.minimum(k, max_tiles - 1)], 0, 0)
    y_sorted = pl.pallas_call(
        _expert_kernel,
        grid_spec=pltpu.PrefetchScalarGridSpec(
            num_scalar_prefetch=2,
            grid=(max_tiles + 1,),
            in_specs=[
                pl.BlockSpec((SLOT_TILE, D_MODEL // 2), lambda k, te, nl: (jnp.maximum(k - 1, 0), 0)),
                pl.BlockSpec((SLOT_TILE, D_MODEL // 2), lambda k, te, nl: (jnp.maximum(k - 1, 0), 0)),
                pl.BlockSpec((1, 1, D_MODEL, d_ff), this_expert),
                pl.BlockSpec((1, 1, D_MODEL, d_ff), this_expert),
                pl.BlockSpec((1, 1, d_ff, D_MODEL), next_expert),
            ],
            out_specs=pl.BlockSpec((SLOT_TILE, D_MODEL),
                                   lambda k, te, nl: (jnp.maximum(k - 1, 0), 0)),
            scratch_shapes=[pltpu.VMEM((D_MODEL, 2 * d_ff), BF16),
                            pltpu.VMEM((d_ff, D_MODEL), BF16)],
        ),
        out_shape=jax.ShapeDtypeStruct((n_slots, D_MODEL), BF16),
        compiler_params=params(),
        name="moe_experts",
    )(tile_e, n_live, *x_sorted, w1, w3, w2)

    rows = COMBINE_CHUNKS * CNT_CHUNK
    assert rows == TS
    steps_per_seq = seq // rows
    n_win = COMBINE_CHUNKS * N_EXPERTS
    window_spec = lambda w: pl.BlockSpec(
        (pl.Element(WINDOW), pl.Element(D_MODEL)),
        lambda i, ws, bs: (pl.multiple_of(ws[i * n_win + w], ROW_ALIGN), 0))
    out = pl.pallas_call(
        functools.partial(_combine_kernel, final=final),
        grid_spec=pltpu.PrefetchScalarGridSpec(
            num_scalar_prefetch=2,
            grid=(n_tok // rows,),
            in_specs=[
                pl.BlockSpec((rows, D_MODEL), lambda i, ws, bs: (i, 0)),
                pl.BlockSpec((1, 1, 3 * D_MODEL),
                             lambda i, ws, bs: (layer * PAD_B + i // steps_per_seq, 0, 0)),
                pl.BlockSpec((rows, LANES), lambda i, ws, bs: (i, 0)),
                pl.BlockSpec((rows, LANES), lambda i, ws, bs: (i, 0)),
                pl.BlockSpec((1, D_MODEL), lambda i, ws, bs: (0, 0)),
            ] + [window_spec(w) for w in range(n_win)],
            out_specs=pl.BlockSpec((rows, D_MODEL), lambda i, ws, bs: (i, 0)),
        ),
        out_shape=jax.ShapeDtypeStruct((n_tok, D_MODEL), F32),
        compiler_params=params(),
        name="moe_combine",
    )(wstart, base, xt, mods, rank, wts, final_gain, *([y_sorted] * n_win))
    return out.reshape(x.shape)


def _block_diag_halves(w):
    depth, heads, hd, _ = w.shape
    per_tile = MXU_DIM // hd
    w = w.reshape(depth, heads // per_tile, per_tile, hd, hd)
    eye = jnp.eye(per_tile, dtype=w.dtype)
    bd = jnp.einsum("dtiab,ij->dtiajb", w, eye)
    return bd.reshape(depth, heads // per_tile, MXU_DIM, MXU_DIM)


def kernel(x, c, mix_norm, mix_mod_w, mix_mod_b, w_in, lru_conv_w, lru_conv_b, lru_wa, lru_ba,
           lru_wi, lru_bi, lru_lambda, sc_conv_w, gn_lru, gn_sc, w_out, ffn_norm, ffn_mod_w,
           ffn_mod_b, dense_w1, dense_w3, dense_w2, router_w, router_b, exp_w1, exp_w3, exp_w2,
           final_norm):
    depth = w_in.shape[0]
    batch = x.shape[0]
    c_pad = jnp.pad(c, ((0, PAD_B - batch), (0, 0)))
    mix_mods = _modulation(c_pad, mix_mod_w, mix_mod_b).reshape(depth * PAD_B, 1, 3 * D_MODEL)
    ffn_mods = _modulation(c_pad, ffn_mod_w, ffn_mod_b).reshape(depth * PAD_B, 1, 3 * D_MODEL)

    row = lambda p: p.reshape(depth, 1, p.shape[-1])
    wg = jnp.concatenate([_block_diag_halves(lru_wa), _block_diag_halves(lru_wi)],
                         axis=-1).astype(BF16)
    bg = jnp.stack([lru_ba.reshape(depth, D_LRU), lru_bi.reshape(depth, D_LRU)], axis=1)
    head = jnp.arange(MXU_DIM) // HEAD_DIM
    gmat = jnp.where(head[:, None] == head[None, :], 1.0 / HEAD_DIM, 0.0).astype(BF16)
    w_in_b = w_in.astype(BF16)
    w_out_b = w_out.astype(BF16)
    dense_w1_b, dense_w3_b, dense_w2_b = (w.astype(BF16) for w in (dense_w1, dense_w3, dense_w2))
    n_moe = router_w.shape[0]
    wr = jnp.pad(router_w, ((0, 0), (0, 0), (0, LANES - N_EXPERTS)))
    br = jnp.pad(router_b, ((0, 0), (0, LANES - N_EXPERTS)),
                 constant_values=-jnp.inf).reshape(n_moe, 1, LANES)
    final_gain = final_norm.reshape(1, D_MODEL)
    mix_g, ffn_g = row(mix_norm), row(ffn_norm)
    lcb, lam, gnl, gns = row(lru_conv_b), row(lru_lambda), row(gn_lru), row(gn_sc)

    for l in range(depth):
        x = _mixer(x, mix_mods, l, mix_g, w_in_b, lru_conv_w, lcb, wg, bg, lam, sc_conv_w, gnl,
                   gns, gmat, w_out_b, natural_in=(l == 0))
        if l % 2 == 0:
            x = _dense_ffn(x, ffn_mods, l, ffn_g, dense_w1_b, dense_w3_b, dense_w2_b)
        else:
            x = _moe_ffn(x, ffn_mods, l, ffn_g, wr, br, exp_w1, exp_w3, exp_w2, final_gain,
                         final=(l == depth - 1))
    if depth % 2 == 1:
        raise NotImplementedError("final norm is fused into the last MoE layer")
    return x
```

```python
import functools

import jax
import jax.numpy as jnp
from jax import lax
from jax.experimental import pallas as pl
from jax.experimental.pallas import tpu as pltpu

F32 = jnp.float32
BF16 = jnp.bfloat16

D_MODEL = 1024
D_LRU = 512
D_SC = 512
D_IN = 2 * D_LRU + 3 * D_SC
HEAD_DIM = 64
LRU_CONV = 4
SC_CONV = 3
LRU_C = 8.0
N_EXPERTS = 8
EPS = 1e-6

SUBLANES = 8
LANES = 128
MXU_DIM = 256
TS = SUBLANES ** 3
GROUPS = TS // SUBLANES
MIXER_TILES = 4
PAD_B = 8
VMEM_LIMIT = 56 * 1024 * 1024


def _mod_norm(x, g, scale, shift):
    ms = jnp.mean(x * x, axis=-1, keepdims=True)
    return x * lax.rsqrt(ms + EPS) * (g * (1.0 + scale)) + shift


def _split_mod(mod):
    return mod[:, :D_MODEL], mod[:, D_MODEL:2 * D_MODEL], mod[:, 2 * D_MODEL:]


def _layer_spec(shape, layer):
    zeros = (0,) * len(shape)
    return pl.BlockSpec((1, *shape), lambda *_: (layer, *zeros), pipeline_mode=pl.Buffered(1))


def _mod_kernel(c_ref, w_ref, b_ref, o_ref):
    c = c_ref[...]
    s = (c * jax.nn.sigmoid(c)).astype(BF16)
    o_ref[0] = jnp.dot(s, w_ref[0].astype(BF16), preferred_element_type=F32) + b_ref[0]


def _modulation(c_pad, w, b):
    depth = w.shape[0]
    n_parts = 3
    return pl.pallas_call(
        _mod_kernel,
        grid=(depth, n_parts),
        in_specs=[
            pl.BlockSpec((PAD_B, D_MODEL), lambda l, j: (0, 0)),
            pl.BlockSpec((1, D_MODEL, D_MODEL), lambda l, j: (l, 0, j)),
            pl.BlockSpec((1, 1, D_MODEL), lambda l, j: (l, 0, j)),
        ],
        out_specs=pl.BlockSpec((1, PAD_B, D_MODEL), lambda l, j: (l, 0, j)),
        out_shape=jax.ShapeDtypeStruct((depth, PAD_B, 3 * D_MODEL), F32),
        compiler_params=pltpu.CompilerParams(
            dimension_semantics=("arbitrary", "arbitrary"), vmem_limit_bytes=VMEM_LIMIT),
        name="adaln_mod",
    )(c_pad, w, b.reshape(depth, 1, 3 * D_MODEL))


def _scan8(a, u, row_in_group):
    for d in (1, 2, 4):
        keep = row_in_group >= d
        a_sh = jnp.where(keep, pltpu.roll(a, d, axis=0), 1.0)
        u_sh = jnp.where(keep, pltpu.roll(u, d, axis=0), 0.0)
        u = a * u_sh + u
        a = a * a_sh
    return a, u


def _row_in_group(rows, cols):
    return lax.broadcasted_iota(jnp.int32, (rows, cols), 0) % SUBLANES


def _shift_rows_down(x, first_row):
    row = lax.broadcasted_iota(jnp.int32, x.shape, 0)
    return jnp.where(row == 0, first_row, pltpu.roll(x, 1, axis=0))


def _group_rms(y, gmat_ref, gain):
    y2 = (y * y).astype(BF16)
    ms = jnp.concatenate(
        [jnp.dot(y2[:, :MXU_DIM], gmat_ref[...], preferred_element_type=F32),
         jnp.dot(y2[:, MXU_DIM:], gmat_ref[...], preferred_element_type=F32)], axis=-1)
    return y * lax.rsqrt(ms + EPS) * gain


def _mixer_kernel(*refs, natural_in):
    n_x = D_MODEL // LANES if natural_in else 1
    x_refs, refs = refs[:n_x], refs[n_x:]
    (mod_ref, g_ref, win_ref, lcw_ref, lcb_ref, wg_ref, bg_ref, lam_ref, scw_ref, gnl_ref,
     gns_ref, gmat_ref, wout_ref, o_ref, xl_buf, v_buf, a0_ref, u0_ref, a1_ref, u1_ref, b64_ref,
     carry_ref) = refs

    @pl.when(pl.program_id(1) == 0)
    def _():
        xl_buf[...] = jnp.zeros(xl_buf.shape, F32)
        v_buf[...] = jnp.zeros(v_buf.shape, F32)
        carry_ref[...] = jnp.zeros(carry_ref.shape, F32)

    for sub in range(MIXER_TILES):
        r0 = sub * TS
        if natural_in:
            x = jnp.concatenate(
                [jnp.concatenate([r[0, pl.ds(r0 + s, GROUPS, stride=SUBLANES), :] for r in x_refs],
                                 axis=1) for s in range(SUBLANES)], axis=0)
        else:
            x = x_refs[0][0, r0:r0 + TS, :]
        o_ref[0, r0:r0 + TS, :] = _mixer_tile(
            x, mod_ref[0], g_ref, win_ref, lcw_ref, lcb_ref, wg_ref, bg_ref, lam_ref, scw_ref,
            gnl_ref, gns_ref, gmat_ref, wout_ref, xl_buf, v_buf, a0_ref, u0_ref, a1_ref, u1_ref,
            b64_ref, carry_ref)


def _slabs(y):
    return [y[s * GROUPS:(s + 1) * GROUPS, :] for s in range(SUBLANES)]


def _causal_conv(y, w_ref, hist_ref):
    taps = w_ref.shape[1]
    slabs = _slabs(y)
    early = {j - SUBLANES: _shift_rows_down(slabs[j], hist_ref[j:j + 1, :])
             for j in range(SUBLANES - taps + 1, SUBLANES)}
    at = lambda j: slabs[j] if j >= 0 else early[j]
    out = []
    for s in range(SUBLANES):
        acc = w_ref[0, 0:1, :] * at(s - (taps - 1))
        for k in range(1, taps):
            acc = acc + w_ref[0, k:k + 1, :] * at(s - (taps - 1) + k)
        out.append(acc)
    for j in range(SUBLANES - taps + 1, SUBLANES):
        hist_ref[j:j + 1, :] = slabs[j][GROUPS - 1:, :]
    return jnp.concatenate(out, axis=0)


def _mixer_tile(x, mod, g_ref, win_ref, lcw_ref, lcb_ref, wg_ref, bg_ref, lam_ref, scw_ref,
                gnl_ref, gns_ref, gmat_ref, wout_ref, xl_buf, v_buf, a0_ref, u0_ref, a1_ref,
                u1_ref, b64_ref, carry_ref):
    n2 = GROUPS // SUBLANES
    shift, scale, gate = _split_mod(mod)
    h = _mod_norm(x, g_ref[0], scale, shift).astype(BF16)
    def in_proj(lo, hi):
        return jnp.dot(h, win_ref[0, :, lo:hi], preferred_element_type=F32)

    xc = _causal_conv(in_proj(0, D_LRU), lcw_ref, xl_buf) + lcb_ref[0]

    cx_sc = in_proj(2 * D_LRU + D_SC, D_IN)
    xcb = xc.astype(BF16)
    g0 = jnp.dot(xcb[:, :MXU_DIM], wg_ref[0, 0], preferred_element_type=F32)
    g1 = jnp.dot(xcb[:, MXU_DIM:], wg_ref[0, 1], preferred_element_type=F32)
    cv = _causal_conv(cx_sc[:, :D_SC] * cx_sc[:, D_SC:], scw_ref, v_buf)

    b_sc = in_proj(2 * D_LRU, 2 * D_LRU + D_SC)
    r = jax.nn.sigmoid(jnp.concatenate([g0[:, :MXU_DIM], g1[:, :MXU_DIM]], axis=-1)
                       + bg_ref[0, 0:1, :])
    i = jax.nn.sigmoid(jnp.concatenate([g0[:, MXU_DIM:], g1[:, MXU_DIM:]], axis=-1)
                       + bg_ref[0, 1:2, :])
    log_a = r * ((-LRU_C) * jax.nn.softplus(-lam_ref[0]))
    a = jnp.exp(log_a)
    uin = jnp.sqrt(-jnp.tanh(log_a) * (1.0 + a * a)) * (i * xc)
    yn_sc = _group_rms(b_sc * cv, gmat_ref, gns_ref[0]).astype(BF16)
    out_sc = jnp.dot(yn_sc, wout_ref[0, D_LRU:, :], preferred_element_type=F32)

    gate_lru = jax.nn.gelu(in_proj(D_LRU, 2 * D_LRU))

    last = SUBLANES - 1
    hs_blocks = []
    for lb in range(D_LRU // LANES):
        lanes = slice(lb * LANES, (lb + 1) * LANES)
        h_prev = carry_ref[lb, last:last + 1, :]
        acc_a = acc_u = None
        for s, (a_s, u_s) in enumerate(zip(_slabs(a[:, lanes]), _slabs(uin[:, lanes]))):
            if s == 0:
                acc_a, acc_u = a_s, u_s
            else:
                acc_u = a_s * acc_u + u_s
                acc_a = a_s * acc_a
            a0_ref[lb, s * GROUPS:(s + 1) * GROUPS, :] = acc_a
            u0_ref[lb, s * GROUPS:(s + 1) * GROUPS, :] = acc_u
        a1, u1 = _scan8(acc_a, acc_u, _row_in_group(GROUPS, LANES))
        a1_ref[lb] = a1
        u1_ref[lb] = u1
        a2, u2 = _scan8(a1_ref[lb, pl.ds(last, n2, stride=SUBLANES), :],
                        u1_ref[lb, pl.ds(last, n2, stride=SUBLANES), :],
                        _row_in_group(n2, LANES))
        h2 = u2 + a2 * h_prev
        carry_ref[lb] = h2
        b64_ref[lb] = _shift_rows_down(h2, h_prev)
        before64 = jnp.concatenate(
            [jnp.broadcast_to(b64_ref[lb, j:j + 1, :], (SUBLANES, LANES)) for j in range(n2)],
            axis=0)
        h1 = u1 + a1 * before64
        before8 = _shift_rows_down(h1, h_prev)
        hs_blocks.append(jnp.concatenate(
            [u0_ref[lb, s * GROUPS:(s + 1) * GROUPS, :]
             + a0_ref[lb, s * GROUPS:(s + 1) * GROUPS, :] * before8 for s in range(SUBLANES)],
            axis=0))
    hs = jnp.concatenate(hs_blocks, axis=-1)

    yn_lru = _group_rms(gate_lru * hs, gmat_ref, gnl_ref[0]).astype(BF16)
    out = out_sc + jnp.dot(yn_lru, wout_ref[0, :D_LRU, :], preferred_element_type=F32)
    return x + gate * out


def _mixer(x, mods, layer, g, w_in, lcw, lcb, wg, bg, lam, scw, gnl, gns, gmat, w_out, natural_in):
    batch, seq, _ = x.shape
    ts = TS
    nlb = D_LRU // LANES
    per_layer = lambda shape: _layer_spec(shape, layer)
    rows = MIXER_TILES * ts
    tile_spec = pl.BlockSpec((1, rows, D_MODEL), lambda b, t: (b, t, 0))
    if natural_in:
        x_specs = [pl.BlockSpec((1, rows, LANES), lambda b, t, l=l: (b, t, l))
                   for l in range(D_MODEL // LANES)]
    else:
        x_specs = [tile_spec]
    return pl.pallas_call(
        functools.partial(_mixer_kernel, natural_in=natural_in),
        grid=(batch, seq // rows),
        in_specs=x_specs + [
            pl.BlockSpec((1, 1, 3 * D_MODEL), lambda b, t: (layer * PAD_B + b, 0, 0)),
            per_layer((1, D_MODEL)),
            per_layer((D_MODEL, D_IN)),
            per_layer((LRU_CONV, D_LRU)),
            per_layer((1, D_LRU)),
            per_layer((2, MXU_DIM, 2 * MXU_DIM)),
            per_layer((2, D_LRU)),
            per_layer((1, D_LRU)),
            per_layer((SC_CONV, D_SC)),
            per_layer((1, D_LRU)),
            per_layer((1, D_SC)),
            pl.BlockSpec((MXU_DIM, MXU_DIM), lambda b, t: (0, 0)),
            per_layer((D_LRU + D_SC, D_MODEL)),
        ],
        out_specs=tile_spec,
        out_shape=jax.ShapeDtypeStruct(x.shape, F32),
        scratch_shapes=[
            pltpu.VMEM((SUBLANES, D_LRU), F32),
            pltpu.VMEM((SUBLANES, D_SC), F32),
            pltpu.VMEM((nlb, ts, LANES), F32),
            pltpu.VMEM((nlb, ts, LANES), F32),
            pltpu.VMEM((nlb, GROUPS, LANES), F32),
            pltpu.VMEM((nlb, GROUPS, LANES), F32),
            pltpu.VMEM((nlb, SUBLANES, LANES), F32),
            pltpu.VMEM((nlb, SUBLANES, LANES), F32),
        ],
        compiler_params=pltpu.CompilerParams(
            dimension_semantics=("arbitrary", "arbitrary"), vmem_limit_bytes=VMEM_LIMIT),
        name="mixer",
    )(*([x] * len(x_specs)), mods, g, w_in, lcw, lcb, wg, bg, lam, scw, gnl, gns, gmat, w_out)


def _swiglu(h, w1, w3, w2):
    a = jnp.dot(h, w1, preferred_element_type=F32)
    b = jnp.dot(h, w3, preferred_element_type=F32)
    g = (a * jax.nn.sigmoid(a) * b).astype(BF16)
    return jnp.dot(g, w2, preferred_element_type=F32)


def _mxu_halves(width):
    split = pl.cdiv(width // 2, MXU_DIM) * MXU_DIM
    return slice(0, split), slice(split, width)


def _dense_ffn_kernel(x_ref, mod_ref, g_ref, w1_ref, w3_ref, w2_ref, o_ref):
    x = x_ref[0]
    shift, scale, gate = _split_mod(mod_ref[0])
    h = _mod_norm(x, g_ref[0], scale, shift).astype(BF16)
    f = jnp.zeros(x.shape, F32)
    for cols in _mxu_halves(w1_ref.shape[2]):
        f = f + _swiglu(h, w1_ref[0, :, cols], w3_ref[0, :, cols], w2_ref[0, cols, :])
    o_ref[0] = x + gate * f


def _dense_ffn(x, mods, layer, g, w1, w3, w2, tm=1024):
    batch, seq, _ = x.shape
    d_ff = w1.shape[2]
    j = layer // 2
    return pl.pallas_call(
        _dense_ffn_kernel,
        grid=(batch, seq // tm),
        in_specs=[
            pl.BlockSpec((1, tm, D_MODEL), lambda b, t: (b, t, 0)),
            pl.BlockSpec((1, 1, 3 * D_MODEL), lambda b, t: (layer * PAD_B + b, 0, 0)),
            _layer_spec((1, D_MODEL), layer),
            _layer_spec((D_MODEL, d_ff), j),
            _layer_spec((D_MODEL, d_ff), j),
            _layer_spec((d_ff, D_MODEL), j),
        ],
        out_specs=pl.BlockSpec((1, tm, D_MODEL), lambda b, t: (b, t, 0)),
        out_shape=jax.ShapeDtypeStruct(x.shape, F32),
        compiler_params=pltpu.CompilerParams(
            dimension_semantics=("arbitrary", "arbitrary"), vmem_limit_bytes=VMEM_LIMIT),
        name="dense_ffn",
    )(x, mods, g, w1, w3, w2)


ROUTE_TILE = 1024
CNT_CHUNK = 128
COMBINE_CHUNKS = 4
SLOT_TILE = 512
GATHER_SLOTS = 128
GATHER_CHUNKS = 6
ROW_ALIGN = 16
WINDOW = CNT_CHUNK + ROW_ALIGN


def _top2(logits):
    lane = lax.broadcasted_iota(jnp.int32, logits.shape, 1)
    m1 = jnp.max(logits, axis=-1, keepdims=True)
    i1 = jnp.min(jnp.where(logits == m1, lane, LANES), axis=-1, keepdims=True)
    rest = jnp.where(lane == i1, -jnp.inf, logits)
    m2 = jnp.max(rest, axis=-1, keepdims=True)
    i2 = jnp.min(jnp.where(rest == m2, lane, LANES), axis=-1, keepdims=True)
    e2 = jnp.exp(m2 - m1)
    w_first = 1.0 / (1.0 + e2)
    w_second = e2 / (1.0 + e2)
    first = lane == i1
    second = lane == i2
    weights = jnp.where(first, w_first, 0.0) + jnp.where(second, w_second, 0.0)
    return weights, first | second


def _split_bf16(v):
    hi = v.astype(BF16)
    return hi, (v - hi.astype(F32)).astype(BF16)


def _router_kernel(x_ref, mod_ref, g_ref, wr_ref, br_ref,
                   h_ref, rank_ref, w_ref, rank_row_ref, cnt_ref, carry_ref):
    n = ROUTE_TILE
    n_chunks = n // CNT_CHUNK

    @pl.when(pl.program_id(0) == 0)
    def _():
        carry_ref[...] = jnp.zeros(carry_ref.shape, F32)

    shift, scale, _ = _split_mod(mod_ref[0])
    h = _mod_norm(x_ref[...], g_ref[0], scale, shift)
    h_hi, h_lo = _split_bf16(h)
    w_hi, w_lo = _split_bf16(wr_ref[0])
    both = jnp.dot(h_hi, jnp.concatenate([w_hi, w_lo], axis=1), preferred_element_type=F32)
    logits = (both[:, :LANES] + both[:, LANES:]
              + jnp.dot(h_lo, w_hi, preferred_element_type=F32)) + br_ref[0]
    weights, mask = _top2(logits)
    m = jnp.where(mask, 1.0, 0.0).astype(BF16)
    before = carry_ref[0:1, :]
    chunk = lax.broadcasted_iota(jnp.int32, (n_chunks, n), 0)
    tok = lax.broadcasted_iota(jnp.int32, (n_chunks, n), 1)
    in_earlier_chunk = jnp.where(tok < chunk * CNT_CHUNK, 1.0, 0.0).astype(BF16)
    upto_chunk_end = jnp.where(tok < (chunk + 1) * CNT_CHUNK, 1.0, 0.0).astype(BF16)
    chunk_start = jnp.dot(in_earlier_chunk, m, preferred_element_type=F32) + before
    cnt = jnp.dot(upto_chunk_end, m, preferred_element_type=F32) + before
    row = lax.broadcasted_iota(jnp.int32, (CNT_CHUNK, CNT_CHUNK), 0)
    col = lax.broadcasted_iota(jnp.int32, (CNT_CHUNK, CNT_CHUNK), 1)
    strictly_lower = jnp.where(row > col, 1.0, 0.0).astype(BF16)
    in_chunk = jnp.concatenate(
        [jnp.dot(strictly_lower, m[j * CNT_CHUNK:(j + 1) * CNT_CHUNK, :],
                 preferred_element_type=F32) + chunk_start[j:j + 1, :]
         for j in range(n_chunks)], axis=0)
    rank = jnp.where(mask, in_chunk, -1.0)
    carry_ref[...] = jnp.broadcast_to(cnt[n_chunks - 1:, :], carry_ref.shape)
    h_ref[...] = h_hi
    rank_ref[...] = rank
    w_ref[...] = weights
    rank_row_ref[...] = rank.T[:N_EXPERTS, :]
    cnt_ref[...] = cnt.astype(jnp.int32)


def _dispatch_kernel(tile_e_ref, ls0_ref, c_lo_ref, n_g_ref, h_ref, rank_ref, o_ref):
    i = pl.program_id(0)
    e = tile_e_ref[i]
    n_chunks = rank_ref.shape[1]
    span = GATHER_CHUNKS * CNT_CHUNK
    subs = SLOT_TILE // GATHER_SLOTS

    def gather(sub, k):
        q = i * subs + sub
        slot = (ls0_ref[q]
                + lax.broadcasted_iota(jnp.int32, (GATHER_SLOTS, CNT_CHUNK), 0)).astype(F32)
        wanted = c_lo_ref[q] + k * GATHER_CHUNKS
        start = jnp.minimum(wanted, n_chunks - GATHER_CHUNKS)
        p = jnp.concatenate(
            [jnp.where((start + j >= wanted) & (rank_ref[e, pl.ds(start + j, 1), :] == slot),
                       1.0, 0.0) for j in range(GATHER_CHUNKS)], axis=1).astype(BF16)
        tokens = pl.ds(pl.multiple_of(start * CNT_CHUNK, CNT_CHUNK), span)
        return jnp.dot(p, h_ref[tokens, :], preferred_element_type=F32).astype(BF16)

    for sub in range(subs):
        o_ref[sub * GATHER_SLOTS:(sub + 1) * GATHER_SLOTS, :] = gather(sub, 0)
    for sub in range(subs):
        def more(k, carry, sub=sub):
            o_ref[sub * GATHER_SLOTS:(sub + 1) * GATHER_SLOTS, :] += gather(sub, k)
            return carry

        lax.fori_loop(1, n_g_ref[i * subs + sub], more, 0)


def _expert_kernel(tile_e_ref, n_live_ref, x_ref, w1_ref, w3_ref, w2_ref, o_ref,
                   w13b_ref, w2b_ref):
    k = pl.program_id(0)
    n_tiles = pl.num_programs(0) - 1
    d_ff = w2b_ref.shape[0]
    tile = k - 1
    live = (tile >= 0) & (tile < n_live_ref[0])
    expert_of = lambda t: tile_e_ref[jnp.clip(t, 0, n_tiles - 1)]

    @pl.when((tile == 0) | ((tile > 0) & (expert_of(tile) != expert_of(tile - 1))))
    def _():
        w13b_ref[:, :d_ff] = w1_ref[0, 0].astype(BF16)
        w13b_ref[:, d_ff:] = w3_ref[0, 0].astype(BF16)

    @pl.when(live)
    def _():
        ab = jnp.dot(x_ref[...], w13b_ref[...], preferred_element_type=F32)
        a, b = ab[:, :d_ff], ab[:, d_ff:]
        g = (a * jax.nn.sigmoid(a) * b).astype(BF16)
        o_ref[...] = jnp.dot(g, w2b_ref[...], preferred_element_type=F32).astype(BF16)

    @pl.when((tile >= 0) & jnp.logical_not(live))
    def _():
        o_ref[...] = jnp.zeros(o_ref.shape, BF16)

    @pl.when((k == 0) | ((k < n_tiles) & (expert_of(k) != expert_of(tile))))
    def _():
        w2b_ref[...] = w2_ref[0, 0].astype(BF16)


def _combine_kernel(wstart_ref, base_ref, x_ref, mod_ref, rank_ref, w_ref, fg_ref, *rest, final):
    n_win = COMBINE_CHUNKS * N_EXPERTS
    y_refs, o_ref = rest[:n_win], rest[n_win]
    i = pl.program_id(0)
    lane = lax.broadcasted_iota(jnp.int32, (CNT_CHUNK, WINDOW), 1)
    gate = mod_ref[0][:, 2 * D_MODEL:]
    chunks = []
    for s in range(COMBINE_CHUNKS):
        rows = slice(s * CNT_CHUNK, (s + 1) * CNT_CHUNK)
        rank = rank_ref[rows, :]
        wts = w_ref[rows, :]
        acc = jnp.zeros((CNT_CHUNK, D_MODEL), F32)
        for e in range(N_EXPERTS):
            w = s * N_EXPERTS + e
            rk = rank[:, e:e + 1]
            slot = rk + base_ref[e].astype(F32)
            window_slot = (wstart_ref[i * n_win + w] + lane).astype(F32)
            hit = (rk >= 0.0) & (slot == window_slot)
            p = jnp.where(hit, wts[:, e:e + 1], 0.0).astype(BF16)
            acc = acc + jnp.dot(p, y_refs[w][...], preferred_element_type=F32)
        y = x_ref[rows, :] + gate * acc
        if final:
            ms = jnp.mean(y * y, axis=-1, keepdims=True)
            y = y * lax.rsqrt(ms + EPS) * fg_ref[...]
        chunks.append(y)
    y = jnp.concatenate(chunks, axis=0)
    if final:
        y = jnp.swapaxes(y.reshape(SUBLANES, GROUPS, D_MODEL), 0, 1).reshape(TS, D_MODEL)
    o_ref[...] = y


def _route_metadata(cnt_end, n_tok):
    i32 = jnp.int32
    cnt_end = cnt_end[:, :N_EXPERTS]
    cnt_start = jnp.concatenate([jnp.zeros((1, N_EXPERTS), i32), cnt_end[:-1]], axis=0)
    n_chunks = cnt_end.shape[0]
    total = cnt_end[-1]
    n_tiles = (total + SLOT_TILE - 1) // SLOT_TILE
    tile_end = jnp.cumsum(n_tiles)
    tile_start = tile_end - n_tiles
    base = (tile_start * SLOT_TILE).astype(i32)
    max_tiles = 2 * n_tok // SLOT_TILE + N_EXPERTS
    i = jnp.arange(max_tiles, dtype=i32)
    tile_e = jnp.minimum(jnp.sum(i[:, None] >= tile_end[None, :], axis=1), N_EXPERTS - 1)
    subs = SLOT_TILE // GATHER_SLOTS
    q = jnp.arange(max_tiles * subs, dtype=i32)
    q_e = tile_e[q // subs]
    ls0 = (q // subs - tile_start[q_e]) * SLOT_TILE + (q % subs) * GATHER_SLOTS
    live = (q // subs < tile_end[-1]) & (ls0 < total[q_e])
    ends = cnt_end[:, q_e].T
    starts = cnt_start[:, q_e].T
    c_lo = jnp.sum(ends <= ls0[:, None], axis=1)
    c_hi = jnp.sum(starts < (ls0 + GATHER_SLOTS)[:, None], axis=1) - 1
    n_g = jnp.where(live, (c_hi - c_lo + GATHER_CHUNKS) // GATHER_CHUNKS, 0)
    c_lo = jnp.minimum(c_lo, n_chunks - 1)
    n_slots = max_tiles * SLOT_TILE
    wstart = jnp.minimum((base[None, :] + cnt_start) // ROW_ALIGN * ROW_ALIGN, n_slots - WINDOW)
    as_i32 = lambda a: a.astype(i32)
    return (as_i32(tile_e), as_i32(tile_end[-1:]), as_i32(ls0), as_i32(c_lo), as_i32(n_g), base,
            as_i32(wstart).reshape(-1))


def _moe_ffn(x, mods, layer, g, wr, br, w1, w3, w2, final_gain, final):
    batch, seq, _ = x.shape
    n_tok = batch * seq
    d_ff = w1.shape[3]
    j = layer // 2
    xt = x.reshape(n_tok, D_MODEL)
    params = lambda: pltpu.CompilerParams(
        dimension_semantics=("arbitrary",), vmem_limit_bytes=VMEM_LIMIT)

    route_tiles_per_seq = seq // ROUTE_TILE
    h, rank, wts, rank_row, cnt_end = pl.pallas_call(
        _router_kernel,
        grid=(n_tok // ROUTE_TILE,),
        in_specs=[
            pl.BlockSpec((ROUTE_TILE, D_MODEL), lambda i: (i, 0)),
            pl.BlockSpec((1, 1, 3 * D_MODEL),
                         lambda i: (layer * PAD_B + i // route_tiles_per_seq, 0, 0)),
            _layer_spec((1, D_MODEL), layer),
            _layer_spec((D_MODEL, LANES), j),
            _layer_spec((1, LANES), j),
        ],
        out_specs=[
            pl.BlockSpec((ROUTE_TILE, D_MODEL), lambda i: (i, 0)),
            pl.BlockSpec((ROUTE_TILE, LANES), lambda i: (i, 0)),
            pl.BlockSpec((ROUTE_TILE, LANES), lambda i: (i, 0)),
            pl.BlockSpec((N_EXPERTS, ROUTE_TILE), lambda i: (0, i)),
            pl.BlockSpec((ROUTE_TILE // CNT_CHUNK, LANES), lambda i: (i, 0)),
        ],
        out_shape=[
            jax.ShapeDtypeStruct((n_tok, D_MODEL), BF16),
            jax.ShapeDtypeStruct((n_tok, LANES), F32),
            jax.ShapeDtypeStruct((n_tok, LANES), F32),
            jax.ShapeDtypeStruct((N_EXPERTS, n_tok), F32),
            jax.ShapeDtypeStruct((n_tok // CNT_CHUNK, LANES), jnp.int32),
        ],
        scratch_shapes=[pltpu.VMEM((SUBLANES, LANES), F32)],
        compiler_params=params(),
        name="moe_router",
    )(xt, mods, g, wr, br)

    tile_e, n_live, ls0, c_lo, n_g, base, wstart = _route_metadata(cnt_end, n_tok)
    max_tiles = tile_e.shape[0]
    n_slots = max_tiles * SLOT_TILE

    x_sorted = pl.pallas_call(
        _dispatch_kernel,
        grid_spec=pltpu.PrefetchScalarGridSpec(
            num_scalar_prefetch=4,
            grid=(max_tiles,),
            in_specs=[pl.BlockSpec(memory_space=pltpu.VMEM),
                      pl.BlockSpec(memory_space=pltpu.VMEM)],
            out_specs=pl.BlockSpec((SLOT_TILE, D_MODEL), lambda i, *_: (i, 0)),
        ),
        out_shape=jax.ShapeDtypeStruct((n_slots, D_MODEL), BF16),
        compiler_params=params(),
        name="moe_dispatch",
    )(tile_e, ls0, c_lo, n_g, h, rank_row.reshape(N_EXPERTS, n_tok // CNT_CHUNK, CNT_CHUNK))

    this_expert = lambda k, te, nl: (j, te[jnp.maximum(k - 1, 0)], 0, 0)
    next_expert = lambda k, te, nl: (j, te[jnp.minimum(k, max_tiles - 1)], 0, 0)
    y_sorted = pl.pallas_call(
        _expert_kernel,
        grid_spec=pltpu.PrefetchScalarGridSpec(
            num_scalar_prefetch=2,
            grid=(max_tiles + 1,),
            in_specs=[
                pl.BlockSpec((SLOT_TILE, D_MODEL), lambda k, te, nl: (jnp.maximum(k - 1, 0), 0)),
                pl.BlockSpec((1, 1, D_MODEL, d_ff), this_expert),
                pl.BlockSpec((1, 1, D_MODEL, d_ff), this_expert),
                pl.BlockSpec((1, 1, d_ff, D_MODEL), next_expert),
            ],
            out_specs=pl.BlockSpec((SLOT_TILE, D_MODEL),
                                   lambda k, te, nl: (jnp.maximum(k - 1, 0), 0)),
            scratch_shapes=[pltpu.VMEM((D_MODEL, 2 * d_ff), BF16),
                            pltpu.VMEM((d_ff, D_MODEL), BF16)],
        ),
        out_shape=jax.ShapeDtypeStruct((n_slots, D_MODEL), BF16),
        compiler_params=params(),
        name="moe_experts",
    )(tile_e, n_live, x_sorted, w1, w3, w2)

    rows = COMBINE_CHUNKS * CNT_CHUNK
    assert rows == TS
    steps_per_seq = seq // rows
    n_win = COMBINE_CHUNKS * N_EXPERTS
    window_spec = lambda w: pl.BlockSpec(
        (pl.Element(WINDOW), pl.Element(D_MODEL)),
        lambda i, ws, bs: (pl.multiple_of(ws[i * n_win + w], ROW_ALIGN), 0))
    out = pl.pallas_call(
        functools.partial(_combine_kernel, final=final),
        grid_spec=pltpu.PrefetchScalarGridSpec(
            num_scalar_prefetch=2,
            grid=(n_tok // rows,),
            in_specs=[
                pl.BlockSpec((rows, D_MODEL), lambda i, ws, bs: (i, 0)),
                pl.BlockSpec((1, 1, 3 * D_MODEL),
                             lambda i, ws, bs: (layer * PAD_B + i // steps_per_seq, 0, 0)),
                pl.BlockSpec((rows, LANES), lambda i, ws, bs: (i, 0)),
                pl.BlockSpec((rows, LANES), lambda i, ws, bs: (i, 0)),
                pl.BlockSpec((1, D_MODEL), lambda i, ws, bs: (0, 0)),
            ] + [window_spec(w) for w in range(n_win)],
            out_specs=pl.BlockSpec((rows, D_MODEL), lambda i, ws, bs: (i, 0)),
        ),
        out_shape=jax.ShapeDtypeStruct((n_tok, D_MODEL), F32),
        compiler_params=params(),
        name="moe_combine",
    )(wstart, base, xt, mods, rank, wts, final_gain, *([y_sorted] * n_win))
    return out.reshape(x.shape)


def _block_diag_halves(w):
    depth, heads, hd, _ = w.shape
    per_tile = MXU_DIM // hd
    w = w.reshape(depth, heads // per_tile, per_tile, hd, hd)
    eye = jnp.eye(per_tile, dtype=w.dtype)
    bd = jnp.einsum("dtiab,ij->dtiajb", w, eye)
    return bd.reshape(depth, heads // per_tile, MXU_DIM, MXU_DIM)


def kernel(x, c, mix_norm, mix_mod_w, mix_mod_b, w_in, lru_conv_w, lru_conv_b, lru_wa, lru_ba,
           lru_wi, lru_bi, lru_lambda, sc_conv_w, gn_lru, gn_sc, w_out, ffn_norm, ffn_mod_w,
           ffn_mod_b, dense_w1, dense_w3, dense_w2, router_w, router_b, exp_w1, exp_w3, exp_w2,
           final_norm):
    depth = w_in.shape[0]
    batch = x.shape[0]
    c_pad = jnp.pad(c, ((0, PAD_B - batch), (0, 0)))
    mix_mods = _modulation(c_pad, mix_mod_w, mix_mod_b).reshape(depth * PAD_B, 1, 3 * D_MODEL)
    ffn_mods = _modulation(c_pad, ffn_mod_w, ffn_mod_b).reshape(depth * PAD_B, 1, 3 * D_MODEL)

    row = lambda p: p.reshape(depth, 1, p.shape[-1])
    wg = jnp.concatenate([_block_diag_halves(lru_wa), _block_diag_halves(lru_wi)],
                         axis=-1).astype(BF16)
    bg = jnp.stack([lru_ba.reshape(depth, D_LRU), lru_bi.reshape(depth, D_LRU)], axis=1)
    head = jnp.arange(MXU_DIM) // HEAD_DIM
    gmat = jnp.where(head[:, None] == head[None, :], 1.0 / HEAD_DIM, 0.0).astype(BF16)
    w_in_b = w_in.astype(BF16)
    w_out_b = w_out.astype(BF16)
    dense_w1_b, dense_w3_b, dense_w2_b = (w.astype(BF16) for w in (dense_w1, dense_w3, dense_w2))
    n_moe = router_w.shape[0]
    wr = jnp.pad(router_w, ((0, 0), (0, 0), (0, LANES - N_EXPERTS)))
    br = jnp.pad(router_b, ((0, 0), (0, LANES - N_EXPERTS)),
                 constant_values=-jnp.inf).reshape(n_moe, 1, LANES)
    final_gain = final_norm.reshape(1, D_MODEL)
    mix_g, ffn_g = row(mix_norm), row(ffn_norm)
    lcb, lam, gnl, gns = row(lru_conv_b), row(lru_lambda), row(gn_lru), row(gn_sc)

    for l in range(depth):
        x = _mixer(x, mix_mods, l, mix_g, w_in_b, lru_conv_w, lcb, wg, bg, lam, sc_conv_w, gnl,
                   gns, gmat, w_out_b, natural_in=(l == 0))
        if l % 2 == 0:
            x = _dense_ffn(x, ffn_mods, l, ffn_g, dense_w1_b, dense_w3_b, dense_w2_b)
        else:
            x = _moe_ffn(x, ffn_mods, l, ffn_g, wr, br, exp_w1, exp_w3, exp_w2, final_gain,
                         final=(l == depth - 1))
    if depth % 2 == 1:
        raise NotImplementedError("final norm is fused into the last MoE layer")
    return x
```

```python
import functools

import jax
import jax.numpy as jnp
from jax import lax
from jax.experimental import pallas as pl
from jax.experimental.pallas import tpu as pltpu

F32 = jnp.float32
BF16 = jnp.bfloat16

D_MODEL = 1024
D_LRU = 512
D_SC = 512
D_IN = 2 * D_LRU + 3 * D_SC
HEAD_DIM = 64
LRU_CONV = 4
SC_CONV = 3
LRU_C = 8.0
N_EXPERTS = 8
EPS = 1e-6

SUBLANES = 8
LANES = 128
MXU_DIM = 256
TS = SUBLANES ** 3
GROUPS = TS // SUBLANES
MIXER_TILES = 2
PAD_B = 8
VMEM_LIMIT = 56 * 1024 * 1024


def _mod_norm(x, g, scale, shift):
    ms = jnp.mean(x * x, axis=-1, keepdims=True)
    return x * lax.rsqrt(ms + EPS) * (g * (1.0 + scale)) + shift


def _split_mod(mod):
    return mod[:, :D_MODEL], mod[:, D_MODEL:2 * D_MODEL], mod[:, 2 * D_MODEL:]


def _layer_spec(shape, layer):
    zeros = (0,) * len(shape)
    return pl.BlockSpec((1, *shape), lambda *_: (layer, *zeros), pipeline_mode=pl.Buffered(1))


def _mod_kernel(c_ref, w_ref, b_ref, o_ref):
    c = c_ref[...]
    s = (c * jax.nn.sigmoid(c)).astype(BF16)
    o_ref[0] = jnp.dot(s, w_ref[0].astype(BF16), preferred_element_type=F32) + b_ref[0]


def _modulation(c_pad, w, b):
    depth = w.shape[0]
    n_parts = 3
    return pl.pallas_call(
        _mod_kernel,
        grid=(depth, n_parts),
        in_specs=[
            pl.BlockSpec((PAD_B, D_MODEL), lambda l, j: (0, 0)),
            pl.BlockSpec((1, D_MODEL, D_MODEL), lambda l, j: (l, 0, j)),
            pl.BlockSpec((1, 1, D_MODEL), lambda l, j: (l, 0, j)),
        ],
        out_specs=pl.BlockSpec((1, PAD_B, D_MODEL), lambda l, j: (l, 0, j)),
        out_shape=jax.ShapeDtypeStruct((depth, PAD_B, 3 * D_MODEL), F32),
        compiler_params=pltpu.CompilerParams(
            dimension_semantics=("arbitrary", "arbitrary"), vmem_limit_bytes=VMEM_LIMIT),
        name="adaln_mod",
    )(c_pad, w, b.reshape(depth, 1, 3 * D_MODEL))


def _scan8(a, u, row_in_group):
    for d in (1, 2, 4):
        keep = row_in_group >= d
        a_sh = jnp.where(keep, pltpu.roll(a, d, axis=0), 1.0)
        u_sh = jnp.where(keep, pltpu.roll(u, d, axis=0), 0.0)
        u = a * u_sh + u
        a = a * a_sh
    return a, u


def _row_in_group(rows, cols):
    return lax.broadcasted_iota(jnp.int32, (rows, cols), 0) % SUBLANES


def _shift_rows_down(x, first_row):
    row = lax.broadcasted_iota(jnp.int32, x.shape, 0)
    return jnp.where(row == 0, first_row, pltpu.roll(x, 1, axis=0))


def _group_rms(y, gmat_ref, gain):
    y2 = (y * y).astype(BF16)
    ms = jnp.concatenate(
        [jnp.dot(y2[:, :MXU_DIM], gmat_ref[...], preferred_element_type=F32),
         jnp.dot(y2[:, MXU_DIM:], gmat_ref[...], preferred_element_type=F32)], axis=-1)
    return y * lax.rsqrt(ms + EPS) * gain


def _mixer_kernel(*refs, natural_in):
    n_x = D_MODEL // LANES if natural_in else 1
    x_refs, refs = refs[:n_x], refs[n_x:]
    (mod_ref, g_ref, win_ref, lcw_ref, lcb_ref, wg_ref, bg_ref, lam_ref, scw_ref, gnl_ref,
     gns_ref, gmat_ref, wout_ref, o_ref, xl_buf, v_buf, a0_ref, u0_ref, a1_ref, u1_ref, b64_ref,
     carry_ref) = refs

    @pl.when(pl.program_id(1) == 0)
    def _():
        xl_buf[...] = jnp.zeros(xl_buf.shape, F32)
        v_buf[...] = jnp.zeros(v_buf.shape, F32)
        carry_ref[...] = jnp.zeros(carry_ref.shape, F32)

    for sub in range(MIXER_TILES):
        r0 = sub * TS
        if natural_in:
            x = jnp.concatenate(
                [jnp.concatenate([r[0, pl.ds(r0 + s, GROUPS, stride=SUBLANES), :] for r in x_refs],
                                 axis=1) for s in range(SUBLANES)], axis=0)
        else:
            x = x_refs[0][0, r0:r0 + TS, :]
        o_ref[0, r0:r0 + TS, :] = _mixer_tile(
            x, mod_ref[0], g_ref, win_ref, lcw_ref, lcb_ref, wg_ref, bg_ref, lam_ref, scw_ref,
            gnl_ref, gns_ref, gmat_ref, wout_ref, xl_buf, v_buf, a0_ref, u0_ref, a1_ref, u1_ref,
            b64_ref, carry_ref)


def _slabs(y):
    return [y[s * GROUPS:(s + 1) * GROUPS, :] for s in range(SUBLANES)]


def _causal_conv(y, w_ref, hist_ref):
    taps = w_ref.shape[1]
    slabs = _slabs(y)
    early = {j - SUBLANES: _shift_rows_down(slabs[j], hist_ref[j:j + 1, :])
             for j in range(SUBLANES - taps + 1, SUBLANES)}
    at = lambda j: slabs[j] if j >= 0 else early[j]
    out = []
    for s in range(SUBLANES):
        acc = w_ref[0, 0:1, :] * at(s - (taps - 1))
        for k in range(1, taps):
            acc = acc + w_ref[0, k:k + 1, :] * at(s - (taps - 1) + k)
        out.append(acc)
    for j in range(SUBLANES - taps + 1, SUBLANES):
        hist_ref[j:j + 1, :] = slabs[j][GROUPS - 1:, :]
    return jnp.concatenate(out, axis=0)


def _mixer_tile(x, mod, g_ref, win_ref, lcw_ref, lcb_ref, wg_ref, bg_ref, lam_ref, scw_ref,
                gnl_ref, gns_ref, gmat_ref, wout_ref, xl_buf, v_buf, a0_ref, u0_ref, a1_ref,
                u1_ref, b64_ref, carry_ref):
    n2 = GROUPS // SUBLANES
    shift, scale, gate = _split_mod(mod)
    h = _mod_norm(x, g_ref[0], scale, shift).astype(BF16)
    def in_proj(lo, hi):
        return jnp.dot(h, win_ref[0, :, lo:hi], preferred_element_type=F32)

    xc = _causal_conv(in_proj(0, D_LRU), lcw_ref, xl_buf) + lcb_ref[0]

    cx_sc = in_proj(2 * D_LRU + D_SC, D_IN)
    xcb = xc.astype(BF16)
    g0 = jnp.dot(xcb[:, :MXU_DIM], wg_ref[0, 0], preferred_element_type=F32)
    g1 = jnp.dot(xcb[:, MXU_DIM:], wg_ref[0, 1], preferred_element_type=F32)
    cv = _causal_conv(cx_sc[:, :D_SC] * cx_sc[:, D_SC:], scw_ref, v_buf)

    b_sc = in_proj(2 * D_LRU, 2 * D_LRU + D_SC)
    r = jax.nn.sigmoid(jnp.concatenate([g0[:, :MXU_DIM], g1[:, :MXU_DIM]], axis=-1)
                       + bg_ref[0, 0:1, :])
    i = jax.nn.sigmoid(jnp.concatenate([g0[:, MXU_DIM:], g1[:, MXU_DIM:]], axis=-1)
                       + bg_ref[0, 1:2, :])
    log_a = r * ((-LRU_C) * jax.nn.softplus(-lam_ref[0]))
    a = jnp.exp(log_a)
    uin = jnp.sqrt(-jnp.tanh(log_a) * (1.0 + a * a)) * (i * xc)
    yn_sc = _group_rms(b_sc * cv, gmat_ref, gns_ref[0]).astype(BF16)
    out_sc = jnp.dot(yn_sc, wout_ref[0, D_LRU:, :], preferred_element_type=F32)

    gate_lru = jax.nn.gelu(in_proj(D_LRU, 2 * D_LRU))

    last = SUBLANES - 1
    hs_blocks = []
    for lb in range(D_LRU // LANES):
        lanes = slice(lb * LANES, (lb + 1) * LANES)
        h_prev = carry_ref[lb, last:last + 1, :]
        acc_a = acc_u = None
        for s, (a_s, u_s) in enumerate(zip(_slabs(a[:, lanes]), _slabs(uin[:, lanes]))):
            if s == 0:
                acc_a, acc_u = a_s, u_s
            else:
                acc_u = a_s * acc_u + u_s
                acc_a = a_s * acc_a
            a0_ref[lb, s * GROUPS:(s + 1) * GROUPS, :] = acc_a
            u0_ref[lb, s * GROUPS:(s + 1) * GROUPS, :] = acc_u
        a1, u1 = _scan8(acc_a, acc_u, _row_in_group(GROUPS, LANES))
        a1_ref[lb] = a1
        u1_ref[lb] = u1
        a2, u2 = _scan8(a1_ref[lb, pl.ds(last, n2, stride=SUBLANES), :],
                        u1_ref[lb, pl.ds(last, n2, stride=SUBLANES), :],
                        _row_in_group(n2, LANES))
        h2 = u2 + a2 * h_prev
        carry_ref[lb] = h2
        b64_ref[lb] = _shift_rows_down(h2, h_prev)
        before64 = jnp.concatenate(
            [jnp.broadcast_to(b64_ref[lb, j:j + 1, :], (SUBLANES, LANES)) for j in range(n2)],
            axis=0)
        h1 = u1 + a1 * before64
        before8 = _shift_rows_down(h1, h_prev)
        hs_blocks.append(jnp.concatenate(
            [u0_ref[lb, s * GROUPS:(s + 1) * GROUPS, :]
             + a0_ref[lb, s * GROUPS:(s + 1) * GROUPS, :] * before8 for s in range(SUBLANES)],
            axis=0))
    hs = jnp.concatenate(hs_blocks, axis=-1)

    yn_lru = _group_rms(gate_lru * hs, gmat_ref, gnl_ref[0]).astype(BF16)
    out = out_sc + jnp.dot(yn_lru, wout_ref[0, :D_LRU, :], preferred_element_type=F32)
    return x + gate * out


def _mixer(x, mods, layer, g, w_in, lcw, lcb, wg, bg, lam, scw, gnl, gns, gmat, w_out, natural_in):
    batch, seq, _ = x.shape
    ts = TS
    nlb = D_LRU // LANES
    per_layer = lambda shape: _layer_spec(shape, layer)
    rows = MIXER_TILES * ts
    tile_spec = pl.BlockSpec((1, rows, D_MODEL), lambda b, t: (b, t, 0))
    if natural_in:
        x_specs = [pl.BlockSpec((1, rows, LANES), lambda b, t, l=l: (b, t, l))
                   for l in range(D_MODEL // LANES)]
    else:
        x_specs = [tile_spec]
    return pl.pallas_call(
        functools.partial(_mixer_kernel, natural_in=natural_in),
        grid=(batch, seq // rows),
        in_specs=x_specs + [
            pl.BlockSpec((1, 1, 3 * D_MODEL), lambda b, t: (layer * PAD_B + b, 0, 0)),
            per_layer((1, D_MODEL)),
            per_layer((D_MODEL, D_IN)),
            per_layer((LRU_CONV, D_LRU)),
            per_layer((1, D_LRU)),
            per_layer((2, MXU_DIM, 2 * MXU_DIM)),
            per_layer((2, D_LRU)),
            per_layer((1, D_LRU)),
            per_layer((SC_CONV, D_SC)),
            per_layer((1, D_LRU)),
            per_layer((1, D_SC)),
            pl.BlockSpec((MXU_DIM, MXU_DIM), lambda b, t: (0, 0)),
            per_layer((D_LRU + D_SC, D_MODEL)),
        ],
        out_specs=tile_spec,
        out_shape=jax.ShapeDtypeStruct(x.shape, F32),
        scratch_shapes=[
            pltpu.VMEM((SUBLANES, D_LRU), F32),
            pltpu.VMEM((SUBLANES, D_SC), F32),
            pltpu.VMEM((nlb, ts, LANES), F32),
            pltpu.VMEM((nlb, ts, LANES), F32),
            pltpu.VMEM((nlb, GROUPS, LANES), F32),
            pltpu.VMEM((nlb, GROUPS, LANES), F32),
            pltpu.VMEM((nlb, SUBLANES, LANES), F32),
            pltpu.VMEM((nlb, SUBLANES, LANES), F32),
        ],
        compiler_params=pltpu.CompilerParams(
            dimension_semantics=("arbitrary", "arbitrary"), vmem_limit_bytes=VMEM_LIMIT),
        name="mixer",
    )(*([x] * len(x_specs)), mods, g, w_in, lcw, lcb, wg, bg, lam, scw, gnl, gns, gmat, w_out)


def _swiglu(h, w1, w3, w2):
    a = jnp.dot(h, w1, preferred_element_type=F32)
    b = jnp.dot(h, w3, preferred_element_type=F32)
    g = (a * jax.nn.sigmoid(a) * b).astype(BF16)
    return jnp.dot(g, w2, preferred_element_type=F32)


def _mxu_halves(width):
    split = pl.cdiv(width // 2, MXU_DIM) * MXU_DIM
    return slice(0, split), slice(split, width)


def _dense_ffn_kernel(x_ref, mod_ref, g_ref, w1_ref, w3_ref, w2_ref, o_ref):
    x = x_ref[0]
    shift, scale, gate = _split_mod(mod_ref[0])
    h = _mod_norm(x, g_ref[0], scale, shift).astype(BF16)
    f = jnp.zeros(x.shape, F32)
    for cols in _mxu_halves(w1_ref.shape[2]):
        f = f + _swiglu(h, w1_ref[0, :, cols], w3_ref[0, :, cols], w2_ref[0, cols, :])
    o_ref[0] = x + gate * f


def _dense_ffn(x, mods, layer, g, w1, w3, w2, tm=1024):
    batch, seq, _ = x.shape
    d_ff = w1.shape[2]
    j = layer // 2
    return pl.pallas_call(
        _dense_ffn_kernel,
        grid=(batch, seq // tm),
        in_specs=[
            pl.BlockSpec((1, tm, D_MODEL), lambda b, t: (b, t, 0)),
            pl.BlockSpec((1, 1, 3 * D_MODEL), lambda b, t: (layer * PAD_B + b, 0, 0)),
            _layer_spec((1, D_MODEL), layer),
            _layer_spec((D_MODEL, d_ff), j),
            _layer_spec((D_MODEL, d_ff), j),
            _layer_spec((d_ff, D_MODEL), j),
        ],
        out_specs=pl.BlockSpec((1, tm, D_MODEL), lambda b, t: (b, t, 0)),
        out_shape=jax.ShapeDtypeStruct(x.shape, F32),
        compiler_params=pltpu.CompilerParams(
            dimension_semantics=("arbitrary", "arbitrary"), vmem_limit_bytes=VMEM_LIMIT),
        name="dense_ffn",
    )(x, mods, g, w1, w3, w2)


ROUTE_TILE = 1024
CNT_CHUNK = 128
COMBINE_CHUNKS = 4
SLOT_TILE = 512
GATHER_SLOTS = 128
GATHER_CHUNKS = 6
ROW_ALIGN = 16
WINDOW = CNT_CHUNK + ROW_ALIGN


def _top2(logits):
    lane = lax.broadcasted_iota(jnp.int32, logits.shape, 1)
    m1 = jnp.max(logits, axis=-1, keepdims=True)
    i1 = jnp.min(jnp.where(logits == m1, lane, LANES), axis=-1, keepdims=True)
    rest = jnp.where(lane == i1, -jnp.inf, logits)
    m2 = jnp.max(rest, axis=-1, keepdims=True)
    i2 = jnp.min(jnp.where(rest == m2, lane, LANES), axis=-1, keepdims=True)
    e2 = jnp.exp(m2 - m1)
    w_first = 1.0 / (1.0 + e2)
    w_second = e2 / (1.0 + e2)
    first = lane == i1
    second = lane == i2
    weights = jnp.where(first, w_first, 0.0) + jnp.where(second, w_second, 0.0)
    return weights, first | second


def _split_bf16(v):
    hi = v.astype(BF16)
    return hi, (v - hi.astype(F32)).astype(BF16)


def _router_kernel(x_ref, mod_ref, g_ref, wr_ref, br_ref,
                   h_ref, rank_ref, rank_row_ref, cnt_ref, carry_ref):
    n = ROUTE_TILE
    n_chunks = n // CNT_CHUNK

    @pl.when(pl.program_id(0) == 0)
    def _():
        carry_ref[...] = jnp.zeros(carry_ref.shape, F32)

    shift, scale, _ = _split_mod(mod_ref[0])
    h = _mod_norm(x_ref[...], g_ref[0], scale, shift)
    h_hi, h_lo = _split_bf16(h)
    w_hi, w_lo = _split_bf16(wr_ref[0])
    both = jnp.dot(h_hi, jnp.concatenate([w_hi, w_lo], axis=1), preferred_element_type=F32)
    logits = (both[:, :LANES] + both[:, LANES:]
              + jnp.dot(h_lo, w_hi, preferred_element_type=F32)) + br_ref[0]
    weights, mask = _top2(logits)
    m = jnp.where(mask, 1.0, 0.0).astype(BF16)
    before = carry_ref[0:1, :]
    chunk = lax.broadcasted_iota(jnp.int32, (n_chunks, n), 0)
    tok = lax.broadcasted_iota(jnp.int32, (n_chunks, n), 1)
    in_earlier_chunk = jnp.where(tok < chunk * CNT_CHUNK, 1.0, 0.0).astype(BF16)
    upto_chunk_end = jnp.where(tok < (chunk + 1) * CNT_CHUNK, 1.0, 0.0).astype(BF16)
    chunk_start = jnp.dot(in_earlier_chunk, m, preferred_element_type=F32) + before
    cnt = jnp.dot(upto_chunk_end, m, preferred_element_type=F32) + before
    row = lax.broadcasted_iota(jnp.int32, (CNT_CHUNK, CNT_CHUNK), 0)
    col = lax.broadcasted_iota(jnp.int32, (CNT_CHUNK, CNT_CHUNK), 1)
    strictly_lower = jnp.where(row > col, 1.0, 0.0).astype(BF16)
    in_chunk = jnp.concatenate(
        [jnp.dot(strictly_lower, m[j * CNT_CHUNK:(j + 1) * CNT_CHUNK, :],
                 preferred_element_type=F32) + chunk_start[j:j + 1, :]
         for j in range(n_chunks)], axis=0)
    rank = jnp.where(mask, in_chunk, -1.0)
    carry_ref[...] = jnp.broadcast_to(cnt[n_chunks - 1:, :], carry_ref.shape)
    h_ref[...] = h_hi
    lane = lax.broadcasted_iota(jnp.int32, rank.shape, 1)
    rank_ref[...] = jnp.where(lane < N_EXPERTS, rank, pltpu.roll(weights, N_EXPERTS, axis=1))
    rank_row_ref[...] = rank.T[:N_EXPERTS, :]
    cnt_ref[...] = cnt.astype(jnp.int32)


def _dispatch_kernel(tile_e_ref, ls0_ref, c_lo_ref, n_g_ref, h_ref, rank_ref, o_ref):
    i = pl.program_id(0)
    e = tile_e_ref[i]
    n_chunks = rank_ref.shape[1]
    span = GATHER_CHUNKS * CNT_CHUNK
    subs = SLOT_TILE // GATHER_SLOTS

    def gather(sub, k):
        q = i * subs + sub
        slot = (ls0_ref[q]
                + lax.broadcasted_iota(jnp.int32, (GATHER_SLOTS, CNT_CHUNK), 0)).astype(F32)
        wanted = c_lo_ref[q] + k * GATHER_CHUNKS
        start = jnp.minimum(wanted, n_chunks - GATHER_CHUNKS)
        p = jnp.concatenate(
            [jnp.where((start + j >= wanted) & (rank_ref[e, pl.ds(start + j, 1), :] == slot),
                       1.0, 0.0) for j in range(GATHER_CHUNKS)], axis=1).astype(BF16)
        tokens = pl.ds(pl.multiple_of(start * CNT_CHUNK, CNT_CHUNK), span)
        return jnp.dot(p, h_ref[tokens, :], preferred_element_type=F32).astype(BF16)

    for sub in range(subs):
        o_ref[sub * GATHER_SLOTS:(sub + 1) * GATHER_SLOTS, :] = gather(sub, 0)
    for sub in range(subs):
        def more(k, carry, sub=sub):
            o_ref[sub * GATHER_SLOTS:(sub + 1) * GATHER_SLOTS, :] += gather(sub, k)
            return carry

        lax.fori_loop(1, n_g_ref[i * subs + sub], more, 0)


def _expert_kernel(tile_e_ref, n_live_ref, x_ref, w1_ref, w3_ref, w2_ref, o_ref,
                   w13b_ref, w2b_ref):
    k = pl.program_id(0)
    n_tiles = pl.num_programs(0) - 1
    d_ff = w2b_ref.shape[0]
    tile = k - 1
    live = (tile >= 0) & (tile < n_live_ref[0])
    expert_of = lambda t: tile_e_ref[jnp.clip(t, 0, n_tiles - 1)]

    @pl.when((tile == 0) | ((tile > 0) & (expert_of(tile) != expert_of(tile - 1))))
    def _():
        w13b_ref[:, :d_ff] = w1_ref[0, 0].astype(BF16)
        w13b_ref[:, d_ff:] = w3_ref[0, 0].astype(BF16)

    @pl.when(live)
    def _():
        ab = jnp.dot(x_ref[...], w13b_ref[...], preferred_element_type=F32)
        a, b = ab[:, :d_ff], ab[:, d_ff:]
        g = (a * jax.nn.sigmoid(a) * b).astype(BF16)
        o_ref[...] = jnp.dot(g, w2b_ref[...], preferred_element_type=F32).astype(BF16)

    @pl.when((tile >= 0) & jnp.logical_not(live))
    def _():
        o_ref[...] = jnp.zeros(o_ref.shape, BF16)

    @pl.when((k == 0) | ((k < n_tiles) & (expert_of(k) != expert_of(tile))))
    def _():
        w2b_ref[...] = w2_ref[0, 0].astype(BF16)


def _combine_kernel(wstart_ref, base_ref, x_ref, mod_ref, rank_ref, fg_ref, *rest, final):
    n_win = COMBINE_CHUNKS * N_EXPERTS
    y_refs, o_ref = rest[:n_win], rest[n_win]
    i = pl.program_id(0)
    lane = lax.broadcasted_iota(jnp.int32, (CNT_CHUNK, WINDOW), 1)
    gate = mod_ref[0][:, 2 * D_MODEL:]
    chunks = []
    for s in range(COMBINE_CHUNKS):
        rows = slice(s * CNT_CHUNK, (s + 1) * CNT_CHUNK)
        rank = rank_ref[rows, :]
        acc = jnp.zeros((CNT_CHUNK, D_MODEL), F32)
        for e in range(N_EXPERTS):
            w = s * N_EXPERTS + e
            rk = rank[:, e:e + 1]
            slot = rk + base_ref[e].astype(F32)
            window_slot = (wstart_ref[i * n_win + w] + lane).astype(F32)
            hit = (rk >= 0.0) & (slot == window_slot)
            wt = rank[:, N_EXPERTS + e:N_EXPERTS + e + 1]
            p = jnp.where(hit, wt, 0.0).astype(BF16)
            acc = acc + jnp.dot(p, y_refs[w][...], preferred_element_type=F32)
        y = x_ref[rows, :] + gate * acc
        if final:
            ms = jnp.mean(y * y, axis=-1, keepdims=True)
            y = y * lax.rsqrt(ms + EPS) * fg_ref[...]
        chunks.append(y)
    y = jnp.concatenate(chunks, axis=0)
    if final:
        y = jnp.swapaxes(y.reshape(SUBLANES, GROUPS, D_MODEL), 0, 1).reshape(TS, D_MODEL)
    o_ref[...] = y


def _route_metadata(cnt_end, n_tok):
    i32 = jnp.int32
    cnt_end = cnt_end[:, :N_EXPERTS]
    cnt_start = jnp.concatenate([jnp.zeros((1, N_EXPERTS), i32), cnt_end[:-1]], axis=0)
    n_chunks = cnt_end.shape[0]
    total = cnt_end[-1]
    n_tiles = (total + SLOT_TILE - 1) // SLOT_TILE
    tile_end = jnp.cumsum(n_tiles)
    tile_start = tile_end - n_tiles
    base = (tile_start * SLOT_TILE).astype(i32)
    max_tiles = 2 * n_tok // SLOT_TILE + N_EXPERTS
    i = jnp.arange(max_tiles, dtype=i32)
    tile_e = jnp.minimum(jnp.sum(i[:, None] >= tile_end[None, :], axis=1), N_EXPERTS - 1)
    subs = SLOT_TILE // GATHER_SLOTS
    q = jnp.arange(max_tiles * subs, dtype=i32)
    q_e = tile_e[q // subs]
    ls0 = (q // subs - tile_start[q_e]) * SLOT_TILE + (q % subs) * GATHER_SLOTS
    live = (q // subs < tile_end[-1]) & (ls0 < total[q_e])
    ends = cnt_end[:, q_e].T
    starts = cnt_start[:, q_e].T
    c_lo = jnp.sum(ends <= ls0[:, None], axis=1)
    c_hi = jnp.sum(starts < (ls0 + GATHER_SLOTS)[:, None], axis=1) - 1
    n_g = jnp.where(live, (c_hi - c_lo + GATHER_CHUNKS) // GATHER_CHUNKS, 0)
    c_lo = jnp.minimum(c_lo, n_chunks - 1)
    n_slots = max_tiles * SLOT_TILE
    wstart = jnp.minimum((base[None, :] + cnt_start) // ROW_ALIGN * ROW_ALIGN, n_slots - WINDOW)
    as_i32 = lambda a: a.astype(i32)
    return (as_i32(tile_e), as_i32(tile_end[-1:]), as_i32(ls0), as_i32(c_lo), as_i32(n_g), base,
            as_i32(wstart).reshape(-1))


def _moe_ffn(x, mods, layer, g, wr, br, w1, w3, w2, final_gain, final):
    batch, seq, _ = x.shape
    n_tok = batch * seq
    d_ff = w1.shape[3]
    j = layer // 2
    xt = x.reshape(n_tok, D_MODEL)
    params = lambda: pltpu.CompilerParams(
        dimension_semantics=("arbitrary",), vmem_limit_bytes=VMEM_LIMIT)

    route_tiles_per_seq = seq // ROUTE_TILE
    h, rank, rank_row, cnt_end = pl.pallas_call(
        _router_kernel,
        grid=(n_tok // ROUTE_TILE,),
        in_specs=[
            pl.BlockSpec((ROUTE_TILE, D_MODEL), lambda i: (i, 0)),
            pl.BlockSpec((1, 1, 3 * D_MODEL),
                         lambda i: (layer * PAD_B + i // route_tiles_per_seq, 0, 0)),
            _layer_spec((1, D_MODEL), layer),
            _layer_spec((D_MODEL, LANES), j),
            _layer_spec((1, LANES), j),
        ],
        out_specs=[
            pl.BlockSpec((ROUTE_TILE, D_MODEL), lambda i: (i, 0)),
            pl.BlockSpec((ROUTE_TILE, LANES), lambda i: (i, 0)),
            pl.BlockSpec((N_EXPERTS, ROUTE_TILE), lambda i: (0, i)),
            pl.BlockSpec((ROUTE_TILE // CNT_CHUNK, LANES), lambda i: (i, 0)),
        ],
        out_shape=[
            jax.ShapeDtypeStruct((n_tok, D_MODEL), BF16),
            jax.ShapeDtypeStruct((n_tok, LANES), F32),
            jax.ShapeDtypeStruct((N_EXPERTS, n_tok), F32),
            jax.ShapeDtypeStruct((n_tok // CNT_CHUNK, LANES), jnp.int32),
        ],
        scratch_shapes=[pltpu.VMEM((SUBLANES, LANES), F32)],
        compiler_params=params(),
        name="moe_router",
    )(xt, mods, g, wr, br)

    tile_e, n_live, ls0, c_lo, n_g, base, wstart = _route_metadata(cnt_end, n_tok)
    max_tiles = tile_e.shape[0]
    n_slots = max_tiles * SLOT_TILE

    x_sorted = pl.pallas_call(
        _dispatch_kernel,
        grid_spec=pltpu.PrefetchScalarGridSpec(
            num_scalar_prefetch=4,
            grid=(max_tiles,),
            in_specs=[pl.BlockSpec(memory_space=pltpu.VMEM),
                      pl.BlockSpec(memory_space=pltpu.VMEM)],
            out_specs=pl.BlockSpec((SLOT_TILE, D_MODEL), lambda i, *_: (i, 0)),
        ),
        out_shape=jax.ShapeDtypeStruct((n_slots, D_MODEL), BF16),
        compiler_params=params(),
        name="moe_dispatch",
    )(tile_e, ls0, c_lo, n_g, h, rank_row.reshape(N_EXPERTS, n_tok // CNT_CHUNK, CNT_CHUNK))

    this_expert = lambda k, te, nl: (j, te[jnp.maximum(k - 1, 0)], 0, 0)
    next_expert = lambda k, te, nl: (j, te[jnp.minimum(k, max_tiles - 1)], 0, 0)
    y_sorted = pl.pallas_call(
        _expert_kernel,
        grid_spec=pltpu.PrefetchScalarGridSpec(
            num_scalar_prefetch=2,
            grid=(max_tiles + 1,),
            in_specs=[
                pl.BlockSpec((SLOT_TILE, D_MODEL), lambda k, te, nl: (jnp.maximum(k - 1, 0), 0)),
                pl.BlockSpec((1, 1, D_MODEL, d_ff), this_expert),
                pl.BlockSpec((1, 1, D_MODEL, d_ff), this_expert),
                pl.BlockSpec((1, 1, d_ff, D_MODEL), next_expert),
            ],
            out_specs=pl.BlockSpec((SLOT_TILE, D_MODEL),
                                   lambda k, te, nl: (jnp.maximum(k - 1, 0), 0)),
            scratch_shapes=[pltpu.VMEM((D_MODEL, 2 * d_ff), BF16),
                            pltpu.VMEM((d_ff, D_MODEL), BF16)],
        ),
        out_shape=jax.ShapeDtypeStruct((n_slots, D_MODEL), BF16),
        compiler_params=params(),
        name="moe_experts",
    )(tile_e, n_live, x_sorted, w1, w3, w2)

    rows = COMBINE_CHUNKS * CNT_CHUNK
    assert rows == TS
    steps_per_seq = seq // rows
    n_win = COMBINE_CHUNKS * N_EXPERTS
    window_spec = lambda w: pl.BlockSpec(
        (pl.Element(WINDOW), pl.Element(D_MODEL)),
        lambda i, ws, bs: (pl.multiple_of(ws[i * n_win + w], ROW_ALIGN), 0))
    out = pl.pallas_call(
        functools.partial(_combine_kernel, final=final),
        grid_spec=pltpu.PrefetchScalarGridSpec(
            num_scalar_prefetch=2,
            grid=(n_tok // rows,),
            in_specs=[
                pl.BlockSpec((rows, D_MODEL), lambda i, ws, bs: (i, 0)),
                pl.BlockSpec((1, 1, 3 * D_MODEL),
                             lambda i, ws, bs: (layer * PAD_B + i // steps_per_seq, 0, 0)),
                pl.BlockSpec((rows, LANES), lambda i, ws, bs: (i, 0)),
                pl.BlockSpec((1, D_MODEL), lambda i, ws, bs: (0, 0)),
            ] + [window_spec(w) for w in range(n_win)],
            out_specs=pl.BlockSpec((rows, D_MODEL), lambda i, ws, bs: (i, 0)),
        ),
        out_shape=jax.ShapeDtypeStruct((n_tok, D_MODEL), F32),
        compiler_params=params(),
        name="moe_combine",
    )(wstart, base, xt, mods, rank, final_gain, *([y_sorted] * n_win))
    return out.reshape(x.shape)


def _block_diag_halves(w):
    depth, heads, hd, _ = w.shape
    per_tile = MXU_DIM // hd
    w = w.reshape(depth, heads // per_tile, per_tile, hd, hd)
    eye = jnp.eye(per_tile, dtype=w.dtype)
    bd = jnp.einsum("dtiab,ij->dtiajb", w, eye)
    return bd.reshape(depth, heads // per_tile, MXU_DIM, MXU_DIM)


def kernel(x, c, mix_norm, mix_mod_w, mix_mod_b, w_in, lru_conv_w, lru_conv_b, lru_wa, lru_ba,
           lru_wi, lru_bi, lru_lambda, sc_conv_w, gn_lru, gn_sc, w_out, ffn_norm, ffn_mod_w,
           ffn_mod_b, dense_w1, dense_w3, dense_w2, router_w, router_b, exp_w1, exp_w3, exp_w2,
           final_norm):
    depth = w_in.shape[0]
    batch = x.shape[0]
    c_pad = jnp.pad(c, ((0, PAD_B - batch), (0, 0)))
    mix_mods = _modulation(c_pad, mix_mod_w, mix_mod_b).reshape(depth * PAD_B, 1, 3 * D_MODEL)
    ffn_mods = _modulation(c_pad, ffn_mod_w, ffn_mod_b).reshape(depth * PAD_B, 1, 3 * D_MODEL)

    row = lambda p: p.reshape(depth, 1, p.shape[-1])
    wg = jnp.concatenate([_block_diag_halves(lru_wa), _block_diag_halves(lru_wi)],
                         axis=-1).astype(BF16)
    bg = jnp.stack([lru_ba.reshape(depth, D_LRU), lru_bi.reshape(depth, D_LRU)], axis=1)
    head = jnp.arange(MXU_DIM) // HEAD_DIM
    gmat = jnp.where(head[:, None] == head[None, :], 1.0 / HEAD_DIM, 0.0).astype(BF16)
    w_in_b = w_in.astype(BF16)
    w_out_b = w_out.astype(BF16)
    dense_w1_b, dense_w3_b, dense_w2_b = (w.astype(BF16) for w in (dense_w1, dense_w3, dense_w2))
    n_moe = router_w.shape[0]
    wr = jnp.pad(router_w, ((0, 0), (0, 0), (0, LANES - N_EXPERTS)))
    br = jnp.pad(router_b, ((0, 0), (0, LANES - N_EXPERTS)),
                 constant_values=-jnp.inf).reshape(n_moe, 1, LANES)
    final_gain = final_norm.reshape(1, D_MODEL)
    mix_g, ffn_g = row(mix_norm), row(ffn_norm)
    lcb, lam, gnl, gns = row(lru_conv_b), row(lru_lambda), row(gn_lru), row(gn_sc)

    for l in range(depth):
        x = _mixer(x, mix_mods, l, mix_g, w_in_b, lru_conv_w, lcb, wg, bg, lam, sc_conv_w, gnl,
                   gns, gmat, w_out_b, natural_in=(l == 0))
        if l % 2 == 0:
            x = _dense_ffn(x, ffn_mods, l, ffn_g, dense_w1_b, dense_w3_b, dense_w2_b)
        else:
            x = _moe_ffn(x, ffn_mods, l, ffn_g, wr, br, exp_w1, exp_w3, exp_w2, final_gain,
                         final=(l == depth - 1))
    if depth % 2 == 1:
        raise NotImplementedError("final norm is fused into the last MoE layer")
    return x
```
